```python
import math
import jax, jax.numpy as jnp
from jax import lax
import numpy as np

D_MODEL = 4096
BATCH = 4
SEQ = 4096
DEPTH = 1

HEAD_DIM = 128
ATTN_GROUPS = ((128, 1), (512, 4), (2048, 16))
HEADS_PER_GROUP = 8
N_ATTN_HEADS = HEADS_PER_GROUP * len(ATTN_GROUPS)
ATTN_WIDTH = N_ATTN_HEADS * HEAD_DIM
ATTN_OUT_WIDTH = HEADS_PER_GROUP * HEAD_DIM
ROT_DIM = HEAD_DIM // 4
ROPE_THETA = 500000.0
ATTN_BLOCK = 128
LRU_WIDTH = 2048
LRU_BLOCKS = 8
LRU_BLOCK_W = LRU_WIDTH // LRU_BLOCKS
CONV_WIDTH = 4
RG_C = 8.0
N_GROUPS = 8
EXPERTS_PER_GROUP = 8
N_EXPERTS = N_GROUPS * EXPERTS_PER_GROUP
TOP_K = 2
D_EXPERT = 512
MOE_BLOCK = 128
LN_EPS = 1e-5
DEEPNORM_ALPHA = (2 * DEPTH) ** 0.25
DEEPNORM_BETA = (8 * DEPTH) ** -0.25
Q_OFF = 0
K_OFF = ATTN_WIDTH
V_OFF = 2 * ATTN_WIDTH
RX_OFF = 3 * ATTN_WIDTH
RG_OFF = RX_OFF + LRU_WIDTH
GATE_OFF = RG_OFF + LRU_WIDTH
IN_COLS = GATE_OFF + 2 * D_MODEL

kernel_name = 'hybrid_dilated_attn_rglru_hmoe_block'


def layer_norm(t, g, b):
    tf = t.astype(jnp.float32)
    mu = jnp.mean(tf, axis=-1, keepdims=True)
    var = jnp.mean(jnp.square(tf - mu), axis=-1, keepdims=True)
    y = (tf - mu) * lax.rsqrt(var + LN_EPS) * g.astype(jnp.float32) + b.astype(jnp.float32)
    return y.astype(t.dtype)


def partial_rope(t, positions):
    half = ROT_DIM // 2
    inv_freq = jnp.power(jnp.float32(ROPE_THETA), -jnp.arange(half, dtype=jnp.float32) * 2.0 / ROT_DIM)
    ang = positions.astype(jnp.float32)[..., None] * inv_freq
    cos = jnp.cos(ang)[:, :, None, :]
    sin = jnp.sin(ang)[:, :, None, :]
    tf = t.astype(jnp.float32)
    t1, t2, rest = tf[..., :half], tf[..., half:ROT_DIM], tf[..., ROT_DIM:]
    out = jnp.concatenate([t1 * cos - t2 * sin, t2 * cos + t1 * sin, rest], axis=-1)
    return out.astype(t.dtype)


def dilated_window_attention(q, k, v, window, dilation):
    B, S, Hg, hd = q.shape
    L = S // dilation
    span = window // dilation
    nb = -(-L // ATTN_BLOCK)
    Lp = nb * ATTN_BLOCK

    def to_blocks(t):
        t = t.reshape(B, L, dilation, Hg, hd).transpose(0, 2, 3, 1, 4)
        t = jnp.pad(t, ((0, 0), (0, 0), (0, 0), (0, Lp - L), (0, 0)))
        return t.reshape(B, dilation, Hg, nb, ATTN_BLOCK, hd)

    def with_prev(t):
        prev = jnp.pad(t, ((0, 0), (0, 0), (0, 0), (1, 0), (0, 0), (0, 0)))[:, :, :, :nb]
        return jnp.concatenate([prev, t], axis=4)

    qb = to_blocks(q)
    kk = with_prev(to_blocks(k))
    vv = with_prev(to_blocks(v))
    s = jnp.einsum('brhnqc,brhnkc->brhnqk', qb, kk).astype(jnp.float32)
    blk = jnp.arange(nb)[:, None] * ATTN_BLOCK
    qpos = blk + jnp.arange(ATTN_BLOCK)[None, :]
    kpos = blk - ATTN_BLOCK + jnp.arange(2 * ATTN_BLOCK)[None, :]
    rel = qpos[:, :, None] - kpos[:, None, :]
    valid = (rel >= 0) & (rel <= span) & (kpos[:, None, :] >= 0)
    s = jnp.where(valid, s, -jnp.inf)
    lse = jax.nn.logsumexp(s, axis=-1)
    p = jnp.exp(s - lse[..., None])
    o = jnp.einsum('brhnqk,brhnkc->brhnqc', p.astype(v.dtype), vv)
    o = o.reshape(B, dilation, Hg, Lp, hd)[:, :, :, :L].transpose(0, 3, 1, 2, 4).reshape(B, S, Hg, hd)
    lse = lse.reshape(B, dilation, Hg, Lp)[:, :, :, :L].transpose(0, 3, 1, 2).reshape(B, S, Hg)
    return o, lse


def causal_depthwise_conv(t, conv_w, conv_b):
    W = t.shape[-1]
    y = lax.conv_general_dilated(t, conv_w.reshape(CONV_WIDTH, 1, W).astype(t.dtype), window_strides=(1,),
                                 padding=[(CONV_WIDTH - 1, 0)], dimension_numbers=('NWC', 'WIO', 'NWC'),
                                 feature_group_count=W)
    return y + conv_b


def rg_lru(xr, w_a, b_a, w_x, b_x, lam):
    B, S, W = xr.shape
    xf = xr.astype(jnp.float32)
    xb = xf.reshape(B, S, LRU_BLOCKS, LRU_BLOCK_W)
    r = jax.nn.sigmoid(jnp.einsum('bsni,nij->bsnj', xb, w_a.astype(jnp.float32)) + b_a.astype(jnp.float32)).reshape(B, S, W)
    i = jax.nn.sigmoid(jnp.einsum('bsni,nij->bsnj', xb, w_x.astype(jnp.float32)) + b_x.astype(jnp.float32)).reshape(B, S, W)
    log_a = -RG_C * r * jax.nn.softplus(-lam.astype(jnp.float32))
    a = jnp.exp(log_a)
    u = jnp.sqrt(-jnp.expm1(2.0 * log_a)) * (i * xf)

    def combine(left, right):
        al, bl = left
        ar, br = right
        return ar * al, ar * bl + br

    _, h = lax.associative_scan(combine, (a, u), axis=1)
    return h


def token_mixer(x, positions, w_in, b_gate, conv_w, conv_b, w_rg_a, b_rg_a, w_rg_x, b_rg_x,
                lru_lambda, w_attn_proj, w_rec_proj, w_out):
    B, S, _ = x.shape
    proj = jnp.einsum('bsd,dc->bsc', x, w_in)
    q = proj[..., Q_OFF:K_OFF].reshape(B, S, N_ATTN_HEADS, HEAD_DIM)
    k = proj[..., K_OFF:V_OFF].reshape(B, S, N_ATTN_HEADS, HEAD_DIM)
    v = proj[..., V_OFF:RX_OFF].reshape(B, S, N_ATTN_HEADS, HEAD_DIM)
    q = partial_rope(q, positions) * (HEAD_DIM ** -0.5)
    k = partial_rope(k, positions)
    outs, lses = [], []
    for g, (window, dilation) in enumerate(ATTN_GROUPS):
        hs = slice(g * HEADS_PER_GROUP, (g + 1) * HEADS_PER_GROUP)
        o, l = dilated_window_attention(q[:, :, hs], k[:, :, hs], v[:, :, hs], window, dilation)
        outs.append(o)
        lses.append(l)
    mix_w = jax.nn.softmax(jnp.stack(lses, axis=0), axis=0)
    attn = jnp.einsum('gbsh,gbshc->bshc', mix_w, jnp.stack(outs, axis=0).astype(jnp.float32))
    attn = attn.reshape(B, S, ATTN_OUT_WIDTH).astype(x.dtype)
    xr = causal_depthwise_conv(proj[..., RX_OFF:RG_OFF], conv_w, conv_b)
    gate_r = jax.nn.gelu(proj[..., RG_OFF:GATE_OFF].astype(jnp.float32))
    h = rg_lru(xr, w_rg_a, b_rg_a, w_rg_x, b_rg_x, lru_lambda)
    rec = (h * gate_r).astype(x.dtype)
    gates = jax.nn.sigmoid((proj[..., GATE_OFF:].reshape(B, S, 2, D_MODEL) + b_gate).astype(jnp.float32))
    y_a = jnp.einsum('bsc,cd->bsd', attn, w_attn_proj).astype(jnp.float32)
    y_r = jnp.einsum('bsc,cd->bsd', rec, w_rec_proj).astype(jnp.float32)
    merged = (gates[:, :, 0] * y_a + gates[:, :, 1] * y_r).astype(x.dtype)
    return jnp.einsum('bsd,de->bse', merged, w_out)


def hierarchical_moe(xt, w_router_group, b_router_group, w_router_expert, b_router_expert, w_gate, w_up, w_down):
    T, D = xt.shape
    logits_g = (xt @ w_router_group + b_router_group).astype(jnp.float32)
    p_g = jax.nn.softmax(logits_g, axis=-1)
    g_idx = jnp.argmax(logits_g, axis=-1)
    p_grp = jnp.take_along_axis(p_g, g_idx[:, None], axis=-1)
    logits_e = (xt @ w_router_expert + b_router_expert).astype(jnp.float32).reshape(T, N_GROUPS, EXPERTS_PER_GROUP)
    le = jnp.take_along_axis(logits_e, g_idx[:, None, None], axis=1)[:, 0]
    top_v, top_i = lax.top_k(le, TOP_K)
    weight = (p_grp * jax.nn.softmax(top_v, axis=-1)).astype(xt.dtype)
    expert_id = g_idx[:, None].astype(jnp.int32) * EXPERTS_PER_GROUP + top_i.astype(jnp.int32)

    A = T * TOP_K
    flat_e = expert_id.reshape(-1)
    flat_tok = jnp.broadcast_to(jnp.arange(T, dtype=jnp.int32)[:, None], (T, TOP_K)).reshape(-1)
    flat_w = weight.reshape(-1)
    order = jnp.argsort(flat_e)
    se = flat_e[order]
    counts = jnp.bincount(flat_e, length=N_EXPERTS)
    starts = jnp.cumsum(counts) - counts
    padded = (counts + MOE_BLOCK - 1) // MOE_BLOCK * MOE_BLOCK
    pends = jnp.cumsum(padded)
    pstarts = pends - padded
    dest = pstarts[se] + jnp.arange(A, dtype=jnp.int32) - starts[se]
    n_blocks = (A + MOE_BLOCK - 1) // MOE_BLOCK + N_EXPERTS
    rows = n_blocks * MOE_BLOCK
    row_tok = jnp.full((rows,), T, dtype=jnp.int32).at[dest].set(flat_tok[order])
    row_w = jnp.zeros((rows,), xt.dtype).at[dest].set(flat_w[order])
    block_e = jnp.minimum(jnp.searchsorted(pends, jnp.arange(n_blocks) * MOE_BLOCK, side='right'), N_EXPERTS - 1)
    xpad = jnp.concatenate([xt, jnp.zeros((1, D), xt.dtype)], axis=0)

    def run_block(args):
        tok, w, e = args
        xb = xpad[tok]
        hid = jax.nn.silu(xb @ w_gate[e]) * (xb @ w_up[e])
        return (hid @ w_down[e]) * w[:, None]

    outs = lax.map(run_block, (row_tok.reshape(n_blocks, MOE_BLOCK), row_w.reshape(n_blocks, MOE_BLOCK), block_e))
    y = jnp.zeros((T + 1, D), outs.dtype).at[row_tok].add(outs.reshape(rows, D))
    return y[:T]


def setup_inputs(seed: int = 0) -> dict:
    key = jax.random.key(seed)
    ks = jax.random.split(key, 26)
    L_ = DEPTH
    f32 = jnp.float32
    nrm = lambda k, shape, scale: jax.random.normal(k, shape, f32) * scale
    x = jax.random.normal(ks[0], (BATCH, SEQ, D_MODEL), f32)
    positions = (jax.random.randint(ks[1], (BATCH, 1), 0, 1024) + jnp.arange(SEQ)[None, :]).astype(jnp.int32)
    w_in = nrm(ks[2], (L_, D_MODEL, IN_COLS), D_MODEL ** -0.5)
    w_in = w_in.at[:, :, V_OFF:RX_OFF].multiply(DEEPNORM_BETA)
    b_gate = nrm(ks[3], (L_, 2, D_MODEL), 0.02)
    conv_w = nrm(ks[4], (L_, CONV_WIDTH, LRU_WIDTH), CONV_WIDTH ** -0.5)
    conv_b = nrm(ks[5], (L_, LRU_WIDTH), 0.02)
    w_rg_a = nrm(ks[6], (L_, LRU_BLOCKS, LRU_BLOCK_W, LRU_BLOCK_W), LRU_BLOCK_W ** -0.5)
    b_rg_a = nrm(ks[7], (L_, LRU_BLOCKS, LRU_BLOCK_W), 0.02)
    w_rg_x = nrm(ks[8], (L_, LRU_BLOCKS, LRU_BLOCK_W, LRU_BLOCK_W), LRU_BLOCK_W ** -0.5)
    b_rg_x = nrm(ks[9], (L_, LRU_BLOCKS, LRU_BLOCK_W), 0.02)
    a_c = jax.random.uniform(ks[10], (L_, LRU_WIDTH), f32, 0.9, 0.999)
    a0 = a_c ** (1.0 / RG_C)
    lru_lambda = jnp.log(a0) - jnp.log1p(-a0)
    w_attn_proj = nrm(ks[11], (L_, ATTN_OUT_WIDTH, D_MODEL), ATTN_OUT_WIDTH ** -0.5)
    w_rec_proj = nrm(ks[12], (L_, LRU_WIDTH, D_MODEL), LRU_WIDTH ** -0.5)
    w_out = nrm(ks[13], (L_, D_MODEL, D_MODEL), D_MODEL ** -0.5 * DEEPNORM_BETA)
    ln1_g = 1.0 + nrm(ks[14], (L_, D_MODEL), 0.02)
    ln1_b = nrm(ks[15], (L_, D_MODEL), 0.02)
    w_router_group = nrm(ks[16], (L_, D_MODEL, N_GROUPS), D_MODEL ** -0.5)
    b_router_group = nrm(ks[17], (L_, N_GROUPS), 0.01)
    w_router_expert = nrm(ks[18], (L_, D_MODEL, N_EXPERTS), D_MODEL ** -0.5)
    b_router_expert = nrm(ks[19], (L_, N_EXPERTS), 0.01)
    w_gate = nrm(ks[20], (L_, N_EXPERTS, D_MODEL, D_EXPERT), D_MODEL ** -0.5)
    w_up = nrm(ks[21], (L_, N_EXPERTS, D_MODEL, D_EXPERT), D_MODEL ** -0.5)
    w_down = nrm(ks[22], (L_, N_EXPERTS, D_EXPERT, D_MODEL), D_EXPERT ** -0.5 * DEEPNORM_BETA)
    ln2_g = 1.0 + nrm(ks[23], (L_, D_MODEL), 0.02)
    ln2_b = nrm(ks[24], (L_, D_MODEL), 0.02)
    return {'x': x, 'positions': positions, 'w_in': w_in, 'b_gate': b_gate, 'conv_w': conv_w, 'conv_b': conv_b,
            'w_rg_a': w_rg_a, 'b_rg_a': b_rg_a, 'w_rg_x': w_rg_x, 'b_rg_x': b_rg_x, 'lru_lambda': lru_lambda,
            'w_attn_proj': w_attn_proj, 'w_rec_proj': w_rec_proj, 'w_out': w_out, 'ln1_g': ln1_g, 'ln1_b': ln1_b,
            'w_router_group': w_router_group, 'b_router_group': b_router_group,
            'w_router_expert': w_router_expert, 'b_router_expert': b_router_expert,
            'w_gate': w_gate, 'w_up': w_up, 'w_down': w_down, 'ln2_g': ln2_g, 'ln2_b': ln2_b}


def reference(x, positions, w_in, b_gate, conv_w, conv_b, w_rg_a, b_rg_a, w_rg_x, b_rg_x, lru_lambda,
              w_attn_proj, w_rec_proj, w_out, ln1_g, ln1_b, w_router_group, b_router_group,
              w_router_expert, b_router_expert, w_gate, w_up, w_down, ln2_g, ln2_b):
    h = x
    B, S, D = x.shape
    for layer in range(DEPTH):
        mix = token_mixer(h, positions, w_in[layer], b_gate[layer], conv_w[layer], conv_b[layer],
                          w_rg_a[layer], b_rg_a[layer], w_rg_x[layer], b_rg_x[layer], lru_lambda[layer],
                          w_attn_proj[layer], w_rec_proj[layer], w_out[layer])
        h = layer_norm(DEEPNORM_ALPHA * h + mix, ln1_g[layer], ln1_b[layer])
        ff = hierarchical_moe(h.reshape(B * S, D), w_router_group[layer], b_router_group[layer],
                              w_router_expert[layer], b_router_expert[layer],
                              w_gate[layer], w_up[layer], w_down[layer]).reshape(B, S, D)
        h = layer_norm(DEEPNORM_ALPHA * h + ff, ln2_g[layer], ln2_b[layer])
    return h
```

```python
import functools

import jax
import jax.numpy as jnp
from jax import lax
from jax.experimental import pallas as pl
from jax.experimental.pallas import tpu as pltpu

D_MODEL = 4096
HEAD_DIM = 128
ATTN_GROUPS = ((128, 1), (512, 4), (2048, 16))
HEADS_PER_GROUP = 8
N_ATTN_HEADS = HEADS_PER_GROUP * len(ATTN_GROUPS)
ATTN_WIDTH = N_ATTN_HEADS * HEAD_DIM
ATTN_OUT_WIDTH = HEADS_PER_GROUP * HEAD_DIM
ROT_DIM = HEAD_DIM // 4
ROPE_THETA = 500000.0
ATTN_BLOCK = 128
LRU_WIDTH = 2048
LRU_BLOCKS = 8
LRU_BLOCK_W = LRU_WIDTH // LRU_BLOCKS
CONV_WIDTH = 4
RG_C = 8.0
N_GROUPS = 8
EXPERTS_PER_GROUP = 8
N_EXPERTS = N_GROUPS * EXPERTS_PER_GROUP
TOP_K = 2
D_EXPERT = 512
LN_EPS = 1e-5
DEPTH = 1
DEEPNORM_ALPHA = (2 * DEPTH) ** 0.25
K_OFF = ATTN_WIDTH
V_OFF = 2 * ATTN_WIDTH
RX_OFF = 3 * ATTN_WIDTH
RG_OFF = RX_OFF + LRU_WIDTH
GATE_OFF = RG_OFF + LRU_WIDTH
IN_COLS = GATE_OFF + 2 * D_MODEL

LANES = 128
COL_TILE = 1024
VMEM_LIMIT = 56 * 1024 * 1024

F32 = jnp.float32
BF16 = jnp.bfloat16


def _params(*sem):
    return pltpu.CompilerParams(dimension_semantics=sem, vmem_limit_bytes=VMEM_LIMIT)


def _rope_table_kernel(pos_ref, invf_ref, c_ref, sa_ref, sb_ref):
    half = ROT_DIM // 2
    ang = pos_ref[...].astype(F32) * invf_ref[...]
    lane = lax.broadcasted_iota(jnp.int32, ang.shape, 1)
    cos = jnp.cos(ang)
    sin = jnp.sin(ang)
    c_ref[...] = jnp.where(lane < ROT_DIM, cos, 1.0)
    sa_ref[...] = jnp.where(lane < half, -sin, 0.0)
    sb_ref[...] = jnp.where((lane >= half) & (lane < ROT_DIM), sin, 0.0)


def _rope_tables(positions):
    T = positions.size
    tm = 2048
    half = ROT_DIM // 2
    inv_freq = jnp.power(jnp.float32(ROPE_THETA), -jnp.arange(half, dtype=F32) * 2.0 / ROT_DIM)
    invf = jnp.zeros((1, LANES), F32).at[0, :ROT_DIM].set(jnp.concatenate([inv_freq, inv_freq]))
    tab = jax.ShapeDtypeStruct((T, LANES), F32)
    return pl.pallas_call(
        _rope_table_kernel,
        grid=(T // tm,),
        in_specs=[pl.BlockSpec((tm, 1), lambda i: (i, 0)), pl.BlockSpec((1, LANES), lambda i: (0, 0))],
        out_specs=[pl.BlockSpec((tm, LANES), lambda i: (i, 0))] * 3,
        out_shape=[tab, tab, tab],
        compiler_params=_params("arbitrary"),
        name="rope_tables",
    )(positions.reshape(T, 1), invf)


def _in_proj_kernel(x_ref, w_ref, c_ref, sa_ref, sb_ref, o_ref):
    n = pl.program_id(1)
    acc = jnp.dot(x_ref[...], w_ref[...], preferred_element_type=F32)
    n_q = ATTN_WIDTH // COL_TILE
    is_rot = n < 2 * n_q

    @pl.when(is_rot)
    def _():
        scale = jnp.where(n < n_q, HEAD_DIM ** -0.5, 1.0).astype(F32)
        c, sa, sb = c_ref[...], sa_ref[...], sb_ref[...]
        for h in range(COL_TILE // HEAD_DIM):
            t = acc[:, h * HEAD_DIM:(h + 1) * HEAD_DIM]
            r = t * c + pltpu.roll(t, HEAD_DIM - ROT_DIM // 2, 1) * sa + pltpu.roll(t, ROT_DIM // 2, 1) * sb
            o_ref[:, h * HEAD_DIM:(h + 1) * HEAD_DIM] = (r * scale).astype(o_ref.dtype)

    @pl.when(jnp.logical_not(is_rot))
    def _():
        o_ref[...] = acc.astype(o_ref.dtype)


def _in_proj(xb, wb, c, sa, sb):
    T = xb.shape[0]
    tm = 1024
    tab_spec = pl.BlockSpec((tm, LANES), lambda m, n: (m, 0))
    return pl.pallas_call(
        _in_proj_kernel,
        grid=(T // tm, IN_COLS // COL_TILE),
        in_specs=[pl.BlockSpec((tm, D_MODEL), lambda m, n: (m, 0)),
                  pl.BlockSpec((D_MODEL, COL_TILE), lambda m, n: (0, n)),
                  tab_spec, tab_spec, tab_spec],
        out_specs=pl.BlockSpec((tm, COL_TILE), lambda m, n: (m, n)),
        out_shape=jax.ShapeDtypeStruct((T, IN_COLS), BF16),
        compiler_params=_params("parallel", "arbitrary"),
        name="in_proj",
    )(xb, wb, c, sa, sb)


def _attn_kernel(q_ref, kp_ref, kc_ref, vp_ref, vc_ref, o_ref, lse_ref):
    n = pl.program_id(2)
    row = lax.broadcasted_iota(jnp.int32, (ATTN_BLOCK, ATTN_BLOCK), 0)
    col = lax.broadcasted_iota(jnp.int32, (ATTN_BLOCK, ATTN_BLOCK), 1)
    mask_c = col <= row
    mask_p = jnp.logical_and(col >= row, n > 0)
    dn = (((1,), (1,)), ((), ()))
    lse_all = jnp.zeros((ATTN_BLOCK, LANES), F32)
    for h in range(HEADS_PER_GROUP):
        hs = slice(h * HEAD_DIM, (h + 1) * HEAD_DIM)
        q = q_ref[0, :, hs]
        sp = lax.dot_general(q, kp_ref[0, :, hs], dn, preferred_element_type=F32)
        sc = lax.dot_general(q, kc_ref[0, :, hs], dn, preferred_element_type=F32)
        sp = jnp.where(mask_p, sp, -jnp.inf)
        sc = jnp.where(mask_c, sc, -jnp.inf)
        m = jnp.maximum(jnp.max(sp, axis=1, keepdims=True), jnp.max(sc, axis=1, keepdims=True))
        pp = jnp.exp(sp - m)
        pc = jnp.exp(sc - m)
        l = jnp.sum(pp, axis=1, keepdims=True) + jnp.sum(pc, axis=1, keepdims=True)
        o = (jnp.dot(pp.astype(BF16), vp_ref[0, :, hs], preferred_element_type=F32)
             + jnp.dot(pc.astype(BF16), vc_ref[0, :, hs], preferred_element_type=F32))
        o_ref[0, :, hs] = (o / l).astype(o_ref.dtype)
        lse_all = jnp.where(col == h, m + jnp.log(l), lse_all)
    lse_ref[0] = lse_all


def _attention_group(proj, batch, seq, g, dilation):
    L = seq // dilation
    nb = L // ATTN_BLOCK
    cb = IN_COLS // COL_TILE
    kb, vb = K_OFF // COL_TILE, V_OFF // COL_TILE
    pv = proj.reshape(batch, L, dilation * IN_COLS)
    blk = (1, ATTN_BLOCK, COL_TILE)
    cur = lambda off: pl.BlockSpec(blk, lambda b, r, n: (b, n, r * cb + off + g))
    prev = lambda off: pl.BlockSpec(blk, lambda b, r, n: (b, jnp.maximum(n - 1, 0), r * cb + off + g))
    o, lse = pl.pallas_call(
        _attn_kernel,
        grid=(batch, dilation, nb),
        in_specs=[cur(0), prev(kb), cur(kb), prev(vb), cur(vb)],
        out_specs=[pl.BlockSpec(blk, lambda b, r, n: (b, n, r)),
                   pl.BlockSpec((1, ATTN_BLOCK, LANES), lambda b, r, n: (b, n, r))],
        out_shape=[jax.ShapeDtypeStruct((batch, L, dilation * ATTN_OUT_WIDTH), BF16),
                   jax.ShapeDtypeStruct((batch, L, dilation * LANES), F32)],
        compiler_params=_params("parallel", "parallel", "arbitrary"),
        name=f"attn_d{dilation}",
    )(pv, pv, pv, pv, pv)
    return o.reshape(batch * seq, ATTN_OUT_WIDTH), lse.reshape(batch * seq, LANES)


_TAIL = 8


def _rglru_kernel(rx_ref, rg_ref, cw_ref, cb_ref, wa_ref, ba_ref, wx_ref, bx_ref, lam_ref, o_ref,
                  xbuf, a_buf, h_buf, carry):
    tt = rx_ref.shape[0]
    t_idx = pl.program_id(2)

    @pl.when(t_idx == 0)
    def _():
        xbuf[0:_TAIL, :] = jnp.zeros((_TAIL, xbuf.shape[1]), F32)
        carry[...] = jnp.zeros(carry.shape, F32)

    x = rx_ref[...].astype(F32)
    xbuf[_TAIL:_TAIL + tt, :] = x
    xr = cb_ref[...] + cw_ref[0:1, :] * xbuf[_TAIL - 3:_TAIL - 3 + tt, :]
    for j in range(1, CONV_WIDTH):
        xr = xr + cw_ref[j:j + 1, :] * xbuf[_TAIL - 3 + j:_TAIL - 3 + j + tt, :]
    xbuf[0:_TAIL, :] = x[tt - _TAIL:tt, :]

    softplus_neg_lam = jnp.log1p(jnp.exp(-lam_ref[...]))
    for blk in range(xr.shape[1] // LRU_BLOCK_W):
        cs = slice(blk * LRU_BLOCK_W, (blk + 1) * LRU_BLOCK_W)
        xb = xr[:, cs]
        xb16 = xb.astype(BF16)
        r = jax.nn.sigmoid(jnp.dot(xb16, wa_ref[blk], preferred_element_type=F32) + ba_ref[:, cs])
        i = jax.nn.sigmoid(jnp.dot(xb16, wx_ref[blk], preferred_element_type=F32) + bx_ref[:, cs])
        log_a = (-RG_C) * r * softplus_neg_lam[:, cs]
        a = jnp.exp(log_a)
        a_buf[:, cs] = a
        h_buf[:, cs] = jnp.sqrt(1.0 - a * a) * (i * xb)

    def step(t, h):
        h = a_buf[pl.ds(t, 1), :] * h + h_buf[pl.ds(t, 1), :]
        h_buf[pl.ds(t, 1), :] = h
        return h

    carry[...] = lax.fori_loop(0, tt, step, carry[...], unroll=8)
    o_ref[...] = (h_buf[...] * jax.nn.gelu(rg_ref[...].astype(F32))).astype(o_ref.dtype)


def _rglru(proj, batch, seq, conv_w, conv_b, wa, ba, wx, bx, lam):
    tt = 512
    nt = seq // tt
    hw = COL_TILE
    assert hw % LRU_BLOCK_W == 0 and RX_OFF % hw == 0 and RG_OFF % hw == 0
    gb = hw // LRU_BLOCK_W
    row = lambda off: pl.BlockSpec((tt, hw), lambda b, c, t: (b * nt + t, off // hw + c))
    vec = pl.BlockSpec((1, hw), lambda b, c, t: (0, c))
    wspec = pl.BlockSpec((gb, LRU_BLOCK_W, LRU_BLOCK_W), lambda b, c, t: (c, 0, 0))
    return pl.pallas_call(
        _rglru_kernel,
        grid=(batch, LRU_WIDTH // hw, nt),
        in_specs=[row(RX_OFF), row(RG_OFF), pl.BlockSpec((CONV_WIDTH, hw), lambda b, c, t: (0, c)), vec,
                  wspec, vec, wspec, vec, vec],
        out_specs=pl.BlockSpec((tt, hw), lambda b, c, t: (b * nt + t, c)),
        out_shape=jax.ShapeDtypeStruct((batch * seq, LRU_WIDTH), BF16),
        scratch_shapes=[pltpu.VMEM((tt + _TAIL, hw), F32), pltpu.VMEM((tt, hw), F32),
                        pltpu.VMEM((tt, hw), F32), pltpu.VMEM((1, hw), F32)],
        compiler_params=_params("parallel", "parallel", "arbitrary"),
        name="rg_lru",
    )(proj, proj, conv_w, conv_b.reshape(1, -1), wa, ba.reshape(1, -1), wx, bx.reshape(1, -1), lam.reshape(1, -1))


def _merge_kernel(o0_ref, o1_ref, o2_ref, l0_ref, l1_ref, l2_ref, rec_ref, g0_ref, g1_ref, bg_ref,
                  wa_ref, wr_ref, out_ref, attn_buf):
    @pl.when(pl.program_id(1) == 0)
    def _():
        l0, l1, l2 = l0_ref[...], l1_ref[...], l2_ref[...]
        for h in range(HEADS_PER_GROUP):
            hs = slice(h * HEAD_DIM, (h + 1) * HEAD_DIM)
            a0, a1, a2 = l0[:, h:h + 1], l1[:, h:h + 1], l2[:, h:h + 1]
            m = jnp.maximum(jnp.maximum(a0, a1), a2)
            e0, e1, e2 = jnp.exp(a0 - m), jnp.exp(a1 - m), jnp.exp(a2 - m)
            inv = 1.0 / (e0 + e1 + e2)
            mixed = ((e0 * inv) * o0_ref[:, hs].astype(F32) + (e1 * inv) * o1_ref[:, hs].astype(F32)
                     + (e2 * inv) * o2_ref[:, hs].astype(F32))
            attn_buf[:, hs] = mixed.astype(attn_buf.dtype)

    ya = jnp.dot(attn_buf[...], wa_ref[...], preferred_element_type=F32)
    yr = jnp.dot(rec_ref[...], wr_ref[...], preferred_element_type=F32)
    gate_a = jax.nn.sigmoid(g0_ref[...].astype(F32) + bg_ref[0:1, :])
    gate_r = jax.nn.sigmoid(g1_ref[...].astype(F32) + bg_ref[1:2, :])
    out_ref[...] = (gate_a * ya + gate_r * yr).astype(out_ref.dtype)


def _merge(os_, lses, rec, proj, b_gate, wa, wr):
    T = rec.shape[0]
    tm, tn = 512, COL_TILE
    g0 = GATE_OFF // tn
    g1 = (GATE_OFF + D_MODEL) // tn
    o_spec = pl.BlockSpec((tm, ATTN_OUT_WIDTH), lambda m, n: (m, 0))
    l_spec = pl.BlockSpec((tm, LANES), lambda m, n: (m, 0))
    return pl.pallas_call(
        _merge_kernel,
        grid=(T // tm, D_MODEL // tn),
        in_specs=[o_spec, o_spec, o_spec, l_spec, l_spec, l_spec,
                  pl.BlockSpec((tm, LRU_WIDTH), lambda m, n: (m, 0)),
                  pl.BlockSpec((tm, tn), lambda m, n: (m, g0 + n)),
                  pl.BlockSpec((tm, tn), lambda m, n: (m, g1 + n)),
                  pl.BlockSpec((2, tn), lambda m, n: (0, n)),
                  pl.BlockSpec((ATTN_OUT_WIDTH, tn), lambda m, n: (0, n)),
                  pl.BlockSpec((LRU_WIDTH, tn), lambda m, n: (0, n))],
        out_specs=pl.BlockSpec((tm, tn), lambda m, n: (m, n)),
        out_shape=jax.ShapeDtypeStruct((T, D_MODEL), BF16),
        scratch_shapes=[pltpu.VMEM((tm, ATTN_OUT_WIDTH), BF16)],
        compiler_params=_params("parallel", "arbitrary"),
        name="merge",
    )(*os_, *lses, rec, proj, proj, b_gate, wa, wr)


def _layer_norm(z, g, b):
    mu = jnp.mean(z, axis=-1, keepdims=True)
    zc = z - mu
    var = jnp.mean(zc * zc, axis=-1, keepdims=True)
    return zc * lax.rsqrt(var + LN_EPS) * g + b


def _out_proj_kernel(mg_ref, w_ref, x_ref, g_ref, b_ref, wrh_ref, wrl_ref, br_ref, h_ref, lg_ref, acc):
    n = pl.program_id(1)
    nn = pl.num_programs(1)
    acc[n] = jnp.dot(mg_ref[...], w_ref[...], preferred_element_type=F32)

    @pl.when(n == nn - 1)
    def _():
        tn = acc.shape[2]
        parts = [DEEPNORM_ALPHA * x_ref[:, j * tn:(j + 1) * tn] + acc[j] for j in range(acc.shape[0])]
        z = jnp.concatenate(parts, axis=1)
        y = _layer_norm(z, g_ref[...], b_ref[...])
        h_ref[...] = y
        y_hi = y.astype(BF16)
        y_lo = (y - y_hi.astype(F32)).astype(BF16)
        lg = (jnp.dot(y_hi, wrh_ref[...], preferred_element_type=F32)
              + jnp.dot(y_hi, wrl_ref[...], preferred_element_type=F32)
              + jnp.dot(y_lo, wrh_ref[...], preferred_element_type=F32))
        lg_ref[...] = lg + br_ref[...]


def _out_proj(merged, w_out, x2, g, b, wr_hi, wr_lo, br):
    T = merged.shape[0]
    tm, tn = 256, 1024
    row = lambda m, n: (m, 0)
    const = lambda m, n: (0, 0)
    return pl.pallas_call(
        _out_proj_kernel,
        grid=(T // tm, D_MODEL // tn),
        in_specs=[pl.BlockSpec((tm, D_MODEL), row),
                  pl.BlockSpec((D_MODEL, tn), lambda m, n: (0, n)),
                  pl.BlockSpec((tm, D_MODEL), row),
                  pl.BlockSpec((1, D_MODEL), const), pl.BlockSpec((1, D_MODEL), const),
                  pl.BlockSpec((D_MODEL, LANES), const), pl.BlockSpec((D_MODEL, LANES), const),
                  pl.BlockSpec((1, LANES), const)],
        out_specs=[pl.BlockSpec((tm, D_MODEL), row), pl.BlockSpec((tm, LANES), row)],
        out_shape=[jax.ShapeDtypeStruct((T, D_MODEL), F32), jax.ShapeDtypeStruct((T, LANES), F32)],
        scratch_shapes=[pltpu.VMEM((D_MODEL // tn, tm, tn), F32)],
        compiler_params=_params("parallel", "arbitrary"),
        name="out_proj_ln1",
    )(merged, w_out, x2, g.reshape(1, -1), b.reshape(1, -1), wr_hi, wr_lo, br)


MOE_TM = 256


def _moe_kernel(be_ref, nv_ref, tok_ref, h_hbm, rw_ref, wg_ref, wu_ref, wd_ref, out_ref, xbuf, sem):
    b = pl.program_id(0)
    nv = nv_ref[0]

    def row_copy(tok, slot, r):
        return pltpu.make_async_copy(h_hbm.at[pl.ds(tok, 1), :], xbuf.at[slot, pl.ds(r, 1), :], sem.at[slot])

    def start_gather(blk, slot):
        def body(r, c):
            row_copy(tok_ref[blk * MOE_TM + r], slot, r).start()
            return c
        lax.fori_loop(0, MOE_TM, body, 0, unroll=8)

    def wait_gather(slot):
        def body(r, c):
            row_copy(0, slot, r).wait()
            return c
        lax.fori_loop(0, MOE_TM, body, 0, unroll=8)

    @pl.when(b == 0)
    def _():
        start_gather(0, 0)

    @pl.when(b + 1 < nv)
    def _():
        start_gather(b + 1, (b + 1) % 2)

    @pl.when(b < nv)
    def _():
        slot = b % 2
        wait_gather(slot)
        x = xbuf[slot].astype(BF16)
        gate = jnp.dot(x, wg_ref[0], preferred_element_type=F32)
        up = jnp.dot(x, wu_ref[0], preferred_element_type=F32)
        hid = (gate * jax.nn.sigmoid(gate) * up).astype(BF16)
        y = jnp.dot(hid, wd_ref[0], preferred_element_type=F32)
        out_ref[...] = y * rw_ref[...]

    @pl.when(b >= nv)
    def _():
        out_ref[...] = jnp.zeros(out_ref.shape, out_ref.dtype)


def _moe_experts(h1, row_tok, row_w, block_e, n_valid, wg, wu, wd):
    rows = row_tok.shape[0]
    nb = rows // MOE_TM
    wspec = lambda shape: pl.BlockSpec((1,) + shape, lambda b, be, nv, tok: (be[b], 0, 0))
    grid_spec = pltpu.PrefetchScalarGridSpec(
        num_scalar_prefetch=3,
        grid=(nb,),
        in_specs=[pl.BlockSpec(memory_space=pl.ANY),
                  pl.BlockSpec((MOE_TM, 1), lambda b, be, nv, tok: (b, 0)),
                  wspec((D_MODEL, D_EXPERT)), wspec((D_MODEL, D_EXPERT)), wspec((D_EXPERT, D_MODEL))],
        out_specs=pl.BlockSpec((MOE_TM, D_MODEL), lambda b, be, nv, tok: (b, 0)),
        scratch_shapes=[pltpu.VMEM((2, MOE_TM, D_MODEL), F32), pltpu.SemaphoreType.DMA((2,))],
    )
    return pl.pallas_call(
        _moe_kernel,
        grid_spec=grid_spec,
        out_shape=jax.ShapeDtypeStruct((rows, D_MODEL), F32),
        compiler_params=_params("arbitrary"),
        name="moe_experts",
    )(block_e, n_valid, row_tok, h1, row_w.reshape(rows, 1), wg, wu, wd)


FIN_TM = 256


def _final_kernel(pos_ref, h_ref, y_hbm, g_ref, b_ref, out_ref, ybuf, sem):
    i = pl.program_id(0)
    n = pl.num_programs(0)

    def row_copy(src_row, slot, r):
        return pltpu.make_async_copy(y_hbm.at[pl.ds(src_row, 1), :], ybuf.at[slot, pl.ds(r, 1), :], sem.at[slot])

    def start_gather(tile, slot):
        def body(r, c):
            for k in range(TOP_K):
                row_copy(pos_ref[(tile * FIN_TM + r) * TOP_K + k], slot, k * FIN_TM + r).start()
            return c
        lax.fori_loop(0, FIN_TM, body, 0, unroll=4)

    def wait_gather(slot):
        def body(r, c):
            row_copy(0, slot, r).wait()
            return c
        lax.fori_loop(0, TOP_K * FIN_TM, body, 0, unroll=8)

    @pl.when(i == 0)
    def _():
        start_gather(0, 0)

    @pl.when(i + 1 < n)
    def _():
        start_gather(i + 1, (i + 1) % 2)

    slot = i % 2
    wait_gather(slot)
    ff = ybuf[slot, 0:FIN_TM, :]
    for k in range(1, TOP_K):
        ff = ff + ybuf[slot, k * FIN_TM:(k + 1) * FIN_TM, :]
    z = DEEPNORM_ALPHA * h_ref[...] + ff
    out_ref[...] = _layer_norm(z, g_ref[...], b_ref[...])


def _moe_combine_ln2(h1, moe_rows, pos, g, b):
    T = h1.shape[0]
    const = lambda i, pos: (0, 0)
    grid_spec = pltpu.PrefetchScalarGridSpec(
        num_scalar_prefetch=1,
        grid=(T // FIN_TM,),
        in_specs=[pl.BlockSpec((FIN_TM, D_MODEL), lambda i, pos: (i, 0)),
                  pl.BlockSpec(memory_space=pl.ANY),
                  pl.BlockSpec((1, D_MODEL), const), pl.BlockSpec((1, D_MODEL), const)],
        out_specs=pl.BlockSpec((FIN_TM, D_MODEL), lambda i, pos: (i, 0)),
        scratch_shapes=[pltpu.VMEM((2, TOP_K * FIN_TM, D_MODEL), F32), pltpu.SemaphoreType.DMA((2,))],
    )
    return pl.pallas_call(
        _final_kernel,
        grid_spec=grid_spec,
        out_shape=jax.ShapeDtypeStruct((T, D_MODEL), F32),
        compiler_params=_params("arbitrary"),
        name="moe_combine_ln2",
    )(pos, h1, moe_rows, g.reshape(1, -1), b.reshape(1, -1))


def _routing(logits):
    T = logits.shape[0]
    logits_g = logits[:, :N_GROUPS]
    logits_e = logits[:, N_GROUPS:N_GROUPS + N_EXPERTS].reshape(T, N_GROUPS, EXPERTS_PER_GROUP)
    p_g = jax.nn.softmax(logits_g, axis=-1)
    g_idx = jnp.argmax(logits_g, axis=-1)
    p_grp = jnp.take_along_axis(p_g, g_idx[:, None], axis=-1)
    le = jnp.take_along_axis(logits_e, g_idx[:, None, None], axis=1)[:, 0]
    top_v, top_i = lax.top_k(le, TOP_K)
    weight = p_grp * jax.nn.softmax(top_v, axis=-1)
    expert_id = g_idx[:, None].astype(jnp.int32) * EXPERTS_PER_GROUP + top_i.astype(jnp.int32)

    A = T * TOP_K
    flat_e = expert_id.reshape(-1)
    flat_w = weight.reshape(-1)
    onehot = (flat_e[:, None] == jnp.arange(N_EXPERTS, dtype=jnp.int32)[None, :]).astype(jnp.int32)
    csum = jnp.cumsum(onehot, axis=0)
    counts = csum[-1]
    rank = jnp.take_along_axis(csum, flat_e[:, None], axis=1)[:, 0] - 1
    padded = (counts + MOE_TM - 1) // MOE_TM * MOE_TM
    pends = jnp.cumsum(padded)
    pstarts = pends - padded
    pos = (pstarts[flat_e] + rank).astype(jnp.int32)
    nb = A // MOE_TM + N_EXPERTS
    rows = nb * MOE_TM
    row_tok = jnp.zeros((rows,), jnp.int32).at[pos].set(jnp.arange(A, dtype=jnp.int32) // TOP_K)
    row_w = jnp.zeros((rows,), F32).at[pos].set(flat_w)
    n_valid = (pends[-1] // MOE_TM).astype(jnp.int32)
    blk_start = jnp.minimum(jnp.arange(nb, dtype=jnp.int32), n_valid - 1) * MOE_TM
    block_e = jnp.minimum(jnp.searchsorted(pends, blk_start, side='right'), N_EXPERTS - 1).astype(jnp.int32)
    return row_tok, row_w, block_e, n_valid.reshape(1), pos


def kernel(x, positions, w_in, b_gate, conv_w, conv_b, w_rg_a, b_rg_a, w_rg_x, b_rg_x, lru_lambda, w_attn_proj, w_rec_proj, w_out, ln1_g, ln1_b, w_router_group, b_router_group, w_router_expert, b_router_expert, w_gate, w_up, w_down, ln2_g, ln2_b):
    B, S, D = x.shape
    T = B * S
    h = x.reshape(T, D)
    c, sa, sb = _rope_tables(positions)
    for layer in range(DEPTH):
        proj = _in_proj(h.astype(BF16), w_in[layer].astype(BF16), c, sa, sb)
        os_, lses = [], []
        for g, (_, dilation) in enumerate(ATTN_GROUPS):
            o, lse = _attention_group(proj, B, S, g, dilation)
            os_.append(o)
            lses.append(lse)
        rec = _rglru(proj, B, S, conv_w[layer], conv_b[layer], w_rg_a[layer].astype(BF16), b_rg_a[layer],
                     w_rg_x[layer].astype(BF16), b_rg_x[layer], lru_lambda[layer])
        merged = _merge(os_, lses, rec, proj, b_gate[layer], w_attn_proj[layer].astype(BF16),
                        w_rec_proj[layer].astype(BF16))
        w_router = jnp.concatenate([w_router_group[layer], w_router_expert[layer]], axis=1)
        w_router = jnp.pad(w_router, ((0, 0), (0, LANES - w_router.shape[1])))
        b_router = jnp.concatenate([b_router_group[layer], b_router_expert[layer]])
        b_router = jnp.pad(b_router, (0, LANES - b_router.shape[0])).reshape(1, LANES)
        wr_hi = w_router.astype(BF16)
        wr_lo = (w_router - wr_hi.astype(F32)).astype(BF16)
        h1, logits = _out_proj(merged, w_out[layer].astype(BF16), h, ln1_g[layer], ln1_b[layer], wr_hi, wr_lo, b_router)
        row_tok, row_w, block_e, n_valid, pos = _routing(logits)
        moe_rows = _moe_experts(h1, row_tok, row_w, block_e, n_valid, w_gate[layer].astype(BF16),
                                w_up[layer].astype(BF16), w_down[layer].astype(BF16))
        h = _moe_combine_ln2(h1, moe_rows, pos, ln2_g[layer], ln2_b[layer])
    return h.reshape(B, S, D)
```

```python
import functools

import jax
import jax.numpy as jnp
from jax import lax
from jax.experimental import pallas as pl
from jax.experimental.pallas import tpu as pltpu

D_MODEL = 4096
HEAD_DIM = 128
ATTN_GROUPS = ((128, 1), (512, 4), (2048, 16))
HEADS_PER_GROUP = 8
N_ATTN_HEADS = HEADS_PER_GROUP * len(ATTN_GROUPS)
ATTN_WIDTH = N_ATTN_HEADS * HEAD_DIM
ATTN_OUT_WIDTH = HEADS_PER_GROUP * HEAD_DIM
ROT_DIM = HEAD_DIM // 4
ROPE_THETA = 500000.0
ATTN_SPAN = 128
LRU_WIDTH = 2048
LRU_BLOCKS = 8
LRU_BLOCK_W = LRU_WIDTH // LRU_BLOCKS
CONV_WIDTH = 4
RG_C = 8.0
N_GROUPS = 8
EXPERTS_PER_GROUP = 8
N_EXPERTS = N_GROUPS * EXPERTS_PER_GROUP
TOP_K = 2
D_EXPERT = 512
LN_EPS = 1e-5
DEPTH = 1
DEEPNORM_ALPHA = (2 * DEPTH) ** 0.25
K_OFF = ATTN_WIDTH
V_OFF = 2 * ATTN_WIDTH
RX_OFF = 3 * ATTN_WIDTH
RG_OFF = RX_OFF + LRU_WIDTH
GATE_OFF = RG_OFF + LRU_WIDTH
IN_COLS = GATE_OFF + 2 * D_MODEL

LANES = 128
COL_TILE = 1024
PROJ_TM = 1024
PROJ_TN = 512
VMEM_LIMIT = 56 * 1024 * 1024

F32 = jnp.float32
BF16 = jnp.bfloat16


def _params(*sem, vmem=VMEM_LIMIT):
    return pltpu.CompilerParams(dimension_semantics=sem, vmem_limit_bytes=vmem)


def _rope_table_kernel(pos_ref, invf_ref, c_ref, sa_ref, sb_ref):
    half = ROT_DIM // 2
    ang = pos_ref[...].astype(F32) * invf_ref[...]
    lane = lax.broadcasted_iota(jnp.int32, ang.shape, 1)
    cos = jnp.cos(ang)
    sin = jnp.sin(ang)
    c_ref[...] = jnp.where(lane < ROT_DIM, cos, 1.0)
    sa_ref[...] = jnp.where(lane < half, -sin, 0.0)
    sb_ref[...] = jnp.where((lane >= half) & (lane < ROT_DIM), sin, 0.0)


def _rope_tables(positions):
    T = positions.size
    tm = 2048
    half = ROT_DIM // 2
    inv_freq = jnp.power(jnp.float32(ROPE_THETA), -jnp.arange(half, dtype=F32) * 2.0 / ROT_DIM)
    invf = jnp.zeros((1, LANES), F32).at[0, :ROT_DIM].set(jnp.concatenate([inv_freq, inv_freq]))
    tab = jax.ShapeDtypeStruct((T, LANES), F32)
    return pl.pallas_call(
        _rope_table_kernel,
        grid=(T // tm,),
        in_specs=[pl.BlockSpec((tm, 1), lambda i: (i, 0)), pl.BlockSpec((1, LANES), lambda i: (0, 0))],
        out_specs=[pl.BlockSpec((tm, LANES), lambda i: (i, 0))] * 3,
        out_shape=[tab, tab, tab],
        compiler_params=_params("arbitrary"),
        name="rope_tables",
    )(positions.reshape(T, 1), invf)


def _in_proj_kernel(x_ref, w_ref, c_ref, sa_ref, sb_ref, o_ref, wb_ref, slab_ref):
    n = pl.program_id(0)
    m = pl.program_id(1)
    tiles_per_sect = ATTN_WIDTH // PROJ_TN
    tiles_per_group = ATTN_OUT_WIDTH // PROJ_TN
    heads = PROJ_TN // HEAD_DIM

    @pl.when(m == 0)
    def _():
        wb_ref[...] = w_ref[...].astype(BF16)

    acc = jnp.dot(x_ref[...], wb_ref[...], preferred_element_type=F32)
    is_attn = n < 3 * tiles_per_sect

    @pl.when(jnp.logical_not(is_attn))
    def _():
        o_ref[...] = acc.astype(o_ref.dtype)

    @pl.when(is_attn)
    def _():
        rot = n < 2 * tiles_per_sect
        scale = jnp.where(n < tiles_per_sect, HEAD_DIM ** -0.5, 1.0).astype(F32)
        c = jnp.where(rot, c_ref[...], 1.0)
        sa = jnp.where(rot, sa_ref[...], 0.0)
        sb = jnp.where(rot, sb_ref[...], 0.0)
        for h in range(heads):
            t = acc[:, h * HEAD_DIM:(h + 1) * HEAD_DIM]
            r = t * c + pltpu.roll(t, HEAD_DIM - ROT_DIM // 2, 1) * sa + pltpu.roll(t, ROT_DIM // 2, 1) * sb
            slab_ref[h] = r * scale
        group = (n % tiles_per_sect) // tiles_per_group
        for g, (_, d) in enumerate(ATTN_GROUPS):
            @pl.when(group == g)
            def _(d=d):
                per = PROJ_TM // d
                for h in range(heads):
                    hs = slice(h * HEAD_DIM, (h + 1) * HEAD_DIM)
                    if d == 1:
                        o_ref[:, hs] = slab_ref[h].astype(o_ref.dtype)
                    else:
                        for res in range(d):
                            rows = slab_ref[h, pl.ds(res, per, stride=d), :]
                            o_ref[res * per:(res + 1) * per, hs] = rows.astype(o_ref.dtype)


def _in_proj(xb, w, c, sa, sb):
    T = xb.shape[0]
    tab_spec = pl.BlockSpec((PROJ_TM, LANES), lambda n, m: (m, 0))
    return pl.pallas_call(
        _in_proj_kernel,
        grid=(IN_COLS // PROJ_TN, T // PROJ_TM),
        in_specs=[pl.BlockSpec((PROJ_TM, D_MODEL), lambda n, m: (m, 0)),
                  pl.BlockSpec((D_MODEL, PROJ_TN), lambda n, m: (0, n)),
                  tab_spec, tab_spec, tab_spec],
        out_specs=pl.BlockSpec((PROJ_TM, PROJ_TN), lambda n, m: (m, n)),
        out_shape=jax.ShapeDtypeStruct((T, IN_COLS), BF16),
        scratch_shapes=[pltpu.VMEM((D_MODEL, PROJ_TN), BF16),
                        pltpu.VMEM((PROJ_TN // HEAD_DIM, PROJ_TM, LANES), F32)],
        compiler_params=_params("arbitrary", "arbitrary"),
        name="in_proj",
    )(xb, w, c, sa, sb)


def _attn_kernel(*refs, qr, nkb, nsub):
    q_ref = refs[0]
    k_refs = refs[1:1 + nkb]
    v_refs = refs[1 + nkb:1 + 2 * nkb]
    o_ref, lse_ref = refs[1 + 2 * nkb:]
    n = pl.program_id(2)
    row = lax.broadcasted_iota(jnp.int32, (qr, qr), 0)
    col = lax.broadcasted_iota(jnp.int32, (qr, qr), 1)
    lane = lax.broadcasted_iota(jnp.int32, (qr, LANES), 1)
    masks = []
    for j in range(nkb):
        rel = row - col + (nkb - 1 - j) * qr
        valid = jnp.logical_and(rel >= 0, rel <= ATTN_SPAN)
        masks.append(jnp.logical_and(valid, n >= nkb - 1 - j))
    dn = (((1,), (1,)), ((), ()))
    for s in range(nsub):
        rs = slice(s * qr, (s + 1) * qr)
        lse_all = jnp.zeros((qr, LANES), F32)
        for h in range(HEADS_PER_GROUP):
            hs = slice(h * HEAD_DIM, (h + 1) * HEAD_DIM)
            q = q_ref[rs, hs]
            sc = [jnp.where(masks[j], lax.dot_general(q, k_refs[j][rs, hs], dn, preferred_element_type=F32), -jnp.inf)
                  for j in range(nkb)]
            m = functools.reduce(jnp.maximum, [jnp.max(x, axis=1, keepdims=True) for x in sc])
            p = [jnp.exp(x - m) for x in sc]
            l = functools.reduce(jnp.add, [jnp.sum(x, axis=1, keepdims=True) for x in p])
            o = functools.reduce(jnp.add, [jnp.dot(p[j].astype(BF16), v_refs[j][rs, hs], preferred_element_type=F32)
                                           for j in range(nkb)])
            o_ref[0, :, s * ATTN_OUT_WIDTH + h * HEAD_DIM:s * ATTN_OUT_WIDTH + (h + 1) * HEAD_DIM] = (
                (o / l).astype(o_ref.dtype))
            lse_all = jnp.where(lane == h, m + jnp.log(l), lse_all)
        lse_ref[0, :, s * LANES:(s + 1) * LANES] = lse_all


def _attention_group(proj, batch, seq, g, dilation):
    L = seq // dilation
    per = PROJ_TM // dilation
    qr = min(per, ATTN_SPAN)
    nsub = ATTN_SPAN // qr
    nkb = ATTN_SPAN // qr + 1
    nblk = L // qr
    tiles_per_batch = seq // PROJ_TM
    blocks_per_tile = PROJ_TM // ATTN_SPAN
    rgrid = dilation // nsub
    kcol, vcol = K_OFF // COL_TILE, V_OFF // COL_TILE

    def row_block(b, r, n):
        tile = b * tiles_per_batch + (n * qr) // per
        return tile * blocks_per_tile + (r * nsub * per + (n * qr) % per) // ATTN_SPAN

    def spec(col, back):
        return pl.BlockSpec((ATTN_SPAN, COL_TILE),
                            lambda b, r, n: (row_block(b, r, jnp.maximum(n - back, 0)), col + g))

    in_specs = ([spec(0, 0)] + [spec(kcol, nkb - 1 - j) for j in range(nkb)]
                + [spec(vcol, nkb - 1 - j) for j in range(nkb)])
    o, lse = pl.pallas_call(
        functools.partial(_attn_kernel, qr=qr, nkb=nkb, nsub=nsub),
        grid=(batch, rgrid, nblk),
        in_specs=in_specs,
        out_specs=[pl.BlockSpec((1, qr, nsub * ATTN_OUT_WIDTH), lambda b, r, n: (b, n, r)),
                   pl.BlockSpec((1, qr, nsub * LANES), lambda b, r, n: (b, n, r))],
        out_shape=[jax.ShapeDtypeStruct((batch, L, dilation * ATTN_OUT_WIDTH), BF16),
                   jax.ShapeDtypeStruct((batch, L, dilation * LANES), F32)],
        compiler_params=_params("parallel", "parallel", "arbitrary"),
        name=f"attn_d{dilation}",
    )(*([proj] * (1 + 2 * nkb)))
    return o.reshape(batch * seq, ATTN_OUT_WIDTH), lse.reshape(batch * seq, LANES)


_TAIL = 8


def _rglru_kernel(rx_ref, rg_ref, cw_ref, cb_ref, wa_ref, ba_ref, wx_ref, bx_ref, lam_ref, o_ref,
                  xbuf, a_buf, h_buf, carry):
    tt = rx_ref.shape[0]
    t_idx = pl.program_id(2)

    @pl.when(t_idx == 0)
    def _():
        xbuf[0:_TAIL, :] = jnp.zeros((_TAIL, xbuf.shape[1]), F32)
        carry[...] = jnp.zeros(carry.shape, F32)

    x = rx_ref[...].astype(F32)
    xbuf[_TAIL:_TAIL + tt, :] = x
    xr = cb_ref[...] + cw_ref[0:1, :] * xbuf[_TAIL - 3:_TAIL - 3 + tt, :]
    for j in range(1, CONV_WIDTH):
        xr = xr + cw_ref[j:j + 1, :] * xbuf[_TAIL - 3 + j:_TAIL - 3 + j + tt, :]
    xbuf[0:_TAIL, :] = x[tt - _TAIL:tt, :]

    softplus_neg_lam = jnp.log1p(jnp.exp(-lam_ref[...]))
    for blk in range(xr.shape[1] // LRU_BLOCK_W):
        cs = slice(blk * LRU_BLOCK_W, (blk + 1) * LRU_BLOCK_W)
        xb = xr[:, cs]
        xb16 = xb.astype(BF16)
        r = jax.nn.sigmoid(jnp.dot(xb16, wa_ref[blk], preferred_element_type=F32) + ba_ref[:, cs])
        i = jax.nn.sigmoid(jnp.dot(xb16, wx_ref[blk], preferred_element_type=F32) + bx_ref[:, cs])
        log_a = (-RG_C) * r * softplus_neg_lam[:, cs]
        a = jnp.exp(log_a)
        a_buf[:, cs] = a
        h_buf[:, cs] = jnp.sqrt(1.0 - a * a) * (i * xb)

    def step(t, h):
        h = a_buf[pl.ds(t, 1), :] * h + h_buf[pl.ds(t, 1), :]
        h_buf[pl.ds(t, 1), :] = h
        return h

    carry[...] = lax.fori_loop(0, tt, step, carry[...], unroll=8)
    o_ref[...] = (h_buf[...] * jax.nn.gelu(rg_ref[...].astype(F32))).astype(o_ref.dtype)


def _rglru(proj, batch, seq, conv_w, conv_b, wa, ba, wx, bx, lam):
    tt = 512
    nt = seq // tt
    hw = COL_TILE
    assert hw % LRU_BLOCK_W == 0 and RX_OFF % hw == 0 and RG_OFF % hw == 0
    gb = hw // LRU_BLOCK_W
    row = lambda off: pl.BlockSpec((tt, hw), lambda b, c, t: (b * nt + t, off // hw + c))
    vec = pl.BlockSpec((1, hw), lambda b, c, t: (0, c))
    wspec = pl.BlockSpec((gb, LRU_BLOCK_W, LRU_BLOCK_W), lambda b, c, t: (c, 0, 0))
    return pl.pallas_call(
        _rglru_kernel,
        grid=(batch, LRU_WIDTH // hw, nt),
        in_specs=[row(RX_OFF), row(RG_OFF), pl.BlockSpec((CONV_WIDTH, hw), lambda b, c, t: (0, c)), vec,
                  wspec, vec, wspec, vec, vec],
        out_specs=pl.BlockSpec((tt, hw), lambda b, c, t: (b * nt + t, c)),
        out_shape=jax.ShapeDtypeStruct((batch * seq, LRU_WIDTH), BF16),
        scratch_shapes=[pltpu.VMEM((tt + _TAIL, hw), F32), pltpu.VMEM((tt, hw), F32),
                        pltpu.VMEM((tt, hw), F32), pltpu.VMEM((1, hw), F32)],
        compiler_params=_params("parallel", "parallel", "arbitrary"),
        name="rg_lru",
    )(proj, proj, conv_w, conv_b.reshape(1, -1), wa, ba.reshape(1, -1), wx, bx.reshape(1, -1), lam.reshape(1, -1))


def _merge_kernel(*refs):
    o0_ref, o1_ref, o2_ref, l0_ref, l1_ref, l2_ref, rec_ref = refs[:7]
    nct = D_MODEL // COL_TILE
    ga_refs = refs[7:7 + nct]
    gr_refs = refs[7 + nct:7 + 2 * nct]
    bg_ref, wa_ref, wr_ref, out_ref = refs[7 + 2 * nct:]
    l0, l1, l2 = l0_ref[...], l1_ref[...], l2_ref[...]
    parts = []
    for h in range(HEADS_PER_GROUP):
        hs = slice(h * HEAD_DIM, (h + 1) * HEAD_DIM)
        a0, a1, a2 = l0[:, h:h + 1], l1[:, h:h + 1], l2[:, h:h + 1]
        m = jnp.maximum(jnp.maximum(a0, a1), a2)
        e0, e1, e2 = jnp.exp(a0 - m), jnp.exp(a1 - m), jnp.exp(a2 - m)
        inv = 1.0 / (e0 + e1 + e2)
        mixed = ((e0 * inv) * o0_ref[:, hs].astype(F32) + (e1 * inv) * o1_ref[:, hs].astype(F32)
                 + (e2 * inv) * o2_ref[:, hs].astype(F32))
        parts.append(mixed.astype(BF16))
    attn = jnp.concatenate(parts, axis=1)
    rec = rec_ref[...]
    for j in range(nct):
        cs = slice(j * COL_TILE, (j + 1) * COL_TILE)
        ya = jnp.dot(attn, wa_ref[:, cs], preferred_element_type=F32)
        yr = jnp.dot(rec, wr_ref[:, cs], preferred_element_type=F32)
        gate_a = jax.nn.sigmoid(ga_refs[j][...].astype(F32) + bg_ref[0:1, cs])
        gate_r = jax.nn.sigmoid(gr_refs[j][...].astype(F32) + bg_ref[1:2, cs])
        out_ref[:, cs] = (gate_a * ya + gate_r * yr).astype(out_ref.dtype)


def _merge(os_, lses, rec, proj, b_gate, wa, wr):
    T = rec.shape[0]
    tm = 256
    nct = D_MODEL // COL_TILE
    row = lambda m: (m, 0)
    const = lambda m: (0, 0)
    gate_specs = [pl.BlockSpec((tm, COL_TILE), lambda m, c=(GATE_OFF + k * D_MODEL) // COL_TILE + j: (m, c))
                  for k in range(2) for j in range(nct)]
    o_spec = pl.BlockSpec((tm, ATTN_OUT_WIDTH), row)
    l_spec = pl.BlockSpec((tm, LANES), row)
    resident = lambda shape: pl.BlockSpec(shape, const, pipeline_mode=pl.Buffered(1))
    return pl.pallas_call(
        _merge_kernel,
        grid=(T // tm,),
        in_specs=[o_spec, o_spec, o_spec, l_spec, l_spec, l_spec, pl.BlockSpec((tm, LRU_WIDTH), row)]
                 + gate_specs
                 + [resident((2, D_MODEL)), resident((ATTN_OUT_WIDTH, D_MODEL)), resident((LRU_WIDTH, D_MODEL))],
        out_specs=pl.BlockSpec((tm, D_MODEL), row),
        out_shape=jax.ShapeDtypeStruct((T, D_MODEL), BF16),
        compiler_params=_params("parallel"),
        name="merge",
    )(*os_, *lses, rec, *([proj] * (2 * nct)), b_gate, wa, wr)


def _layer_norm(z, g, b):
    mu = jnp.mean(z, axis=-1, keepdims=True)
    zc = z - mu
    var = jnp.mean(zc * zc, axis=-1, keepdims=True)
    return zc * lax.rsqrt(var + LN_EPS) * g + b


def _out_proj_kernel(mg_ref, w_ref, x_ref, g_ref, b_ref, wrh_ref, wrl_ref, br_ref, h_ref, lg_ref):
    n = pl.program_id(1)
    nn = pl.num_programs(1)
    tn = w_ref.shape[1]
    mix = jnp.dot(mg_ref[...], w_ref[...], preferred_element_type=F32)
    for j in range(D_MODEL // tn):
        @pl.when(n == j)
        def _(j=j):
            h_ref[:, j * tn:(j + 1) * tn] = mix

    @pl.when(n == nn - 1)
    def _():
        rc = 128
        for i in range(h_ref.shape[0] // rc):
            rs = slice(i * rc, (i + 1) * rc)
            z = DEEPNORM_ALPHA * x_ref[rs, :] + h_ref[rs, :]
            y = _layer_norm(z, g_ref[...], b_ref[...])
            h_ref[rs, :] = y
            y_hi = y.astype(BF16)
            y_lo = (y - y_hi.astype(F32)).astype(BF16)
            lg = (jnp.dot(y_hi, wrh_ref[...], preferred_element_type=F32)
                  + jnp.dot(y_hi, wrl_ref[...], preferred_element_type=F32)
                  + jnp.dot(y_lo, wrh_ref[...], preferred_element_type=F32))
            lg_ref[rs, :] = lg + br_ref[...]


def _out_proj(merged, w_out, x2, g, b, wr_hi, wr_lo, br):
    T = merged.shape[0]
    tm, tn = 512, 512
    row = lambda m, n: (m, 0)
    const = lambda m, n: (0, 0)
    resident = lambda shape: pl.BlockSpec(shape, const, pipeline_mode=pl.Buffered(1))
    return pl.pallas_call(
        _out_proj_kernel,
        grid=(T // tm, D_MODEL // tn),
        in_specs=[pl.BlockSpec((tm, D_MODEL), row),
                  pl.BlockSpec((D_MODEL, tn), lambda m, n: (0, n)),
                  pl.BlockSpec((tm, D_MODEL), row),
                  resident((1, D_MODEL)), resident((1, D_MODEL)),
                  resident((D_MODEL, LANES)), resident((D_MODEL, LANES)), resident((1, LANES))],
        out_specs=[pl.BlockSpec((tm, D_MODEL), row), pl.BlockSpec((tm, LANES), row)],
        out_shape=[jax.ShapeDtypeStruct((T, D_MODEL), F32), jax.ShapeDtypeStruct((T, LANES), F32)],
        compiler_params=_params("parallel", "arbitrary"),
        name="out_proj_ln1",
    )(merged, w_out, x2, g.reshape(1, -1), b.reshape(1, -1), wr_hi, wr_lo, br)


MOE_TM = 256


def _moe_up_kernel(be_ref, first_ref, nv_ref, tok_ref, h_hbm, wg_ref, wu_ref, hid_ref, xbuf, wgb, wub, sem):
    b = pl.program_id(0)
    nv = nv_ref[0]

    def row_copy(tok, slot, r):
        return pltpu.make_async_copy(h_hbm.at[pl.ds(tok, 1), :], xbuf.at[slot, pl.ds(r, 1), :], sem.at[slot])

    def start_gather(blk, slot):
        def body(r, c):
            row_copy(tok_ref[blk * MOE_TM + r], slot, r).start()
            return c
        lax.fori_loop(0, MOE_TM, body, 0, unroll=8)

    def wait_gather(slot):
        def body(r, c):
            row_copy(0, slot, r).wait()
            return c
        lax.fori_loop(0, MOE_TM, body, 0, unroll=8)

    @pl.when(b == 0)
    def _():
        start_gather(0, 0)

    @pl.when(b + 1 < nv)
    def _():
        start_gather(b + 1, (b + 1) % 2)

    @pl.when(jnp.logical_and(b < nv, first_ref[b] == 1))
    def _():
        wgb[...] = wg_ref[0].astype(BF16)
        wub[...] = wu_ref[0].astype(BF16)

    @pl.when(b < nv)
    def _():
        slot = b % 2
        wait_gather(slot)
        x = xbuf[slot].astype(BF16)
        gate = jnp.dot(x, wgb[...], preferred_element_type=F32)
        up = jnp.dot(x, wub[...], preferred_element_type=F32)
        hid_ref[...] = (gate * jax.nn.sigmoid(gate) * up).astype(hid_ref.dtype)

    @pl.when(b >= nv)
    def _():
        hid_ref[...] = jnp.zeros(hid_ref.shape, hid_ref.dtype)


def _moe_down_kernel(be_ref, first_ref, nv_ref, hid_ref, rw_ref, wd_ref, out_ref, wdb):
    b = pl.program_id(0)
    nv = nv_ref[0]

    @pl.when(jnp.logical_and(b < nv, first_ref[b] == 1))
    def _():
        wdb[...] = wd_ref[0].astype(BF16)

    @pl.when(b < nv)
    def _():
        out_ref[...] = jnp.dot(hid_ref[...], wdb[...], preferred_element_type=F32) * rw_ref[...]

    @pl.when(b >= nv)
    def _():
        out_ref[...] = jnp.zeros(out_ref.shape, out_ref.dtype)


def _moe_experts(h1, row_tok, row_w, block_e, first, n_valid, wg, wu, wd):
    rows = row_tok.shape[0]
    nb = rows // MOE_TM
    hid = pl.pallas_call(
        _moe_up_kernel,
        grid_spec=pltpu.PrefetchScalarGridSpec(
            num_scalar_prefetch=4,
            grid=(nb,),
            in_specs=[pl.BlockSpec(memory_space=pl.ANY),
                      pl.BlockSpec((1, D_MODEL, D_EXPERT), lambda b, be, fi, nv, tok: (be[b], 0, 0)),
                      pl.BlockSpec((1, D_MODEL, D_EXPERT), lambda b, be, fi, nv, tok: (be[b], 0, 0))],
            out_specs=pl.BlockSpec((MOE_TM, D_EXPERT), lambda b, be, fi, nv, tok: (b, 0)),
            scratch_shapes=[pltpu.VMEM((2, MOE_TM, D_MODEL), F32), pltpu.VMEM((D_MODEL, D_EXPERT), BF16),
                            pltpu.VMEM((D_MODEL, D_EXPERT), BF16), pltpu.SemaphoreType.DMA((2,))]),
        out_shape=jax.ShapeDtypeStruct((rows, D_EXPERT), BF16),
        compiler_params=_params("arbitrary"),
        name="moe_up",
    )(block_e, first, n_valid, row_tok, h1, wg, wu)
    return pl.pallas_call(
        _moe_down_kernel,
        grid_spec=pltpu.PrefetchScalarGridSpec(
            num_scalar_prefetch=3,
            grid=(nb,),
            in_specs=[pl.BlockSpec((MOE_TM, D_EXPERT), lambda b, be, fi, nv: (b, 0)),
                      pl.BlockSpec((MOE_TM, 1), lambda b, be, fi, nv: (b, 0)),
                      pl.BlockSpec((1, D_EXPERT, D_MODEL), lambda b, be, fi, nv: (be[b], 0, 0))],
            out_specs=pl.BlockSpec((MOE_TM, D_MODEL), lambda b, be, fi, nv: (b, 0)),
            scratch_shapes=[pltpu.VMEM((D_EXPERT, D_MODEL), BF16)]),
        out_shape=jax.ShapeDtypeStruct((rows, D_MODEL), F32),
        compiler_params=_params("arbitrary"),
        name="moe_down",
    )(block_e, first, n_valid, hid, row_w.reshape(rows, 1), wd)


FIN_TM = 256


def _final_kernel(pos_ref, h_ref, y_hbm, g_ref, b_ref, out_ref, ybuf, sem):
    i = pl.program_id(0)
    n = pl.num_programs(0)

    def row_copy(src_row, slot, r):
        return pltpu.make_async_copy(y_hbm.at[pl.ds(src_row, 1), :], ybuf.at[slot, pl.ds(r, 1), :], sem.at[slot])

    def start_gather(tile, slot):
        def body(r, c):
            for k in range(TOP_K):
                row_copy(pos_ref[(tile * FIN_TM + r) * TOP_K + k], slot, k * FIN_TM + r).start()
            return c
        lax.fori_loop(0, FIN_TM, body, 0, unroll=4)

    def wait_gather(slot):
        def body(r, c):
            row_copy(0, slot, r).wait()
            return c
        lax.fori_loop(0, TOP_K * FIN_TM, body, 0, unroll=8)

    @pl.when(i == 0)
    def _():
        start_gather(0, 0)

    @pl.when(i + 1 < n)
    def _():
        start_gather(i + 1, (i + 1) % 2)

    slot = i % 2
    wait_gather(slot)
    ff = ybuf[slot, 0:FIN_TM, :]
    for k in range(1, TOP_K):
        ff = ff + ybuf[slot, k * FIN_TM:(k + 1) * FIN_TM, :]
    z = DEEPNORM_ALPHA * h_ref[...] + ff
    out_ref[...] = _layer_norm(z, g_ref[...], b_ref[...])


def _moe_combine_ln2(h1, moe_rows, pos, g, b):
    T = h1.shape[0]
    const = lambda i, pos: (0, 0)
    grid_spec = pltpu.PrefetchScalarGridSpec(
        num_scalar_prefetch=1,
        grid=(T // FIN_TM,),
        in_specs=[pl.BlockSpec((FIN_TM, D_MODEL), lambda i, pos: (i, 0)),
                  pl.BlockSpec(memory_space=pl.ANY),
                  pl.BlockSpec((1, D_MODEL), const), pl.BlockSpec((1, D_MODEL), const)],
        out_specs=pl.BlockSpec((FIN_TM, D_MODEL), lambda i, pos: (i, 0)),
        scratch_shapes=[pltpu.VMEM((2, TOP_K * FIN_TM, D_MODEL), F32), pltpu.SemaphoreType.DMA((2,))],
    )
    return pl.pallas_call(
        _final_kernel,
        grid_spec=grid_spec,
        out_shape=jax.ShapeDtypeStruct((T, D_MODEL), F32),
        compiler_params=_params("arbitrary"),
        name="moe_combine_ln2",
    )(pos, h1, moe_rows, g.reshape(1, -1), b.reshape(1, -1))


def _routing(logits):
    T = logits.shape[0]
    logits_g = logits[:, :N_GROUPS]
    logits_e = logits[:, N_GROUPS:N_GROUPS + N_EXPERTS].reshape(T, N_GROUPS, EXPERTS_PER_GROUP)
    p_g = jax.nn.softmax(logits_g, axis=-1)
    g_idx = jnp.argmax(logits_g, axis=-1)
    p_grp = jnp.take_along_axis(p_g, g_idx[:, None], axis=-1)
    le = jnp.take_along_axis(logits_e, g_idx[:, None, None], axis=1)[:, 0]
    top_v, top_i = lax.top_k(le, TOP_K)
    weight = p_grp * jax.nn.softmax(top_v, axis=-1)
    expert_id = g_idx[:, None].astype(jnp.int32) * EXPERTS_PER_GROUP + top_i.astype(jnp.int32)

    A = T * TOP_K
    flat_e = expert_id.reshape(-1)
    flat_w = weight.reshape(-1)
    onehot = (flat_e[:, None] == jnp.arange(N_EXPERTS, dtype=jnp.int32)[None, :]).astype(jnp.int32)
    csum = jnp.cumsum(onehot, axis=0)
    counts = csum[-1]
    rank = jnp.take_along_axis(csum, flat_e[:, None], axis=1)[:, 0] - 1
    starts = jnp.cumsum(counts) - counts
    padded = (counts + MOE_TM - 1) // MOE_TM * MOE_TM
    pends = jnp.cumsum(padded)
    pstarts = pends - padded
    pos = (pstarts[flat_e] + rank).astype(jnp.int32)
    order = jnp.argsort(flat_e, stable=True).astype(jnp.int32)
    nb = A // MOE_TM + N_EXPERTS
    rows = nb * MOE_TM
    r = jnp.arange(rows, dtype=jnp.int32)
    e_r = jnp.minimum(jnp.searchsorted(pends, r, side='right'), N_EXPERTS - 1).astype(jnp.int32)
    i_in = r - pstarts[e_r]
    valid = i_in < counts[e_r]
    a_r = order[jnp.clip(starts[e_r] + i_in, 0, A - 1)]
    row_tok = jnp.where(valid, a_r // TOP_K, 0).astype(jnp.int32)
    row_w = jnp.where(valid, flat_w[a_r], 0.0).astype(F32)
    n_valid = (pends[-1] // MOE_TM).astype(jnp.int32)
    blk = jnp.arange(nb, dtype=jnp.int32)
    block_e = e_r[jnp.minimum(blk, n_valid - 1) * MOE_TM]
    prev_e = jnp.concatenate([jnp.full((1,), -1, jnp.int32), block_e[:-1]])
    first = jnp.logical_and(block_e != prev_e, blk < n_valid).astype(jnp.int32)
    return row_tok, row_w, block_e, first, n_valid.reshape(1), pos


def kernel(x, positions, w_in, b_gate, conv_w, conv_b, w_rg_a, b_rg_a, w_rg_x, b_rg_x, lru_lambda, w_attn_proj, w_rec_proj, w_out, ln1_g, ln1_b, w_router_group, b_router_group, w_router_expert, b_router_expert, w_gate, w_up, w_down, ln2_g, ln2_b):
    B, S, D = x.shape
    T = B * S
    h = x.reshape(T, D)
    c, sa, sb = _rope_tables(positions)
    for layer in range(DEPTH):
        proj = _in_proj(h.astype(BF16), w_in[layer], c, sa, sb)
        os_, lses = [], []
        for g, (_, dilation) in enumerate(ATTN_GROUPS):
            o, lse = _attention_group(proj, B, S, g, dilation)
            os_.append(o)
            lses.append(lse)
        rec = _rglru(proj, B, S, conv_w[layer], conv_b[layer], w_rg_a[layer].astype(BF16), b_rg_a[layer],
                     w_rg_x[layer].astype(BF16), b_rg_x[layer], lru_lambda[layer])
        merged = _merge(os_, lses, rec, proj, b_gate[layer], w_attn_proj[layer].astype(BF16),
                        w_rec_proj[layer].astype(BF16))
        w_router = jnp.concatenate([w_router_group[layer], w_router_expert[layer]], axis=1)
        w_router = jnp.pad(w_router, ((0, 0), (0, LANES - w_router.shape[1])))
        b_router = jnp.concatenate([b_router_group[layer], b_router_expert[layer]])
        b_router = jnp.pad(b_router, (0, LANES - b_router.shape[0])).reshape(1, LANES)
        wr_hi = w_router.astype(BF16)
        wr_lo = (w_router - wr_hi.astype(F32)).astype(BF16)
        h1, logits = _out_proj(merged, w_out[layer].astype(BF16), h, ln1_g[layer], ln1_b[layer], wr_hi, wr_lo, b_router)
        row_tok, row_w, block_e, first, n_valid, pos = _routing(logits)
        moe_rows = _moe_experts(h1, row_tok, row_w, block_e, first, n_valid, w_gate[layer], w_up[layer], w_down[layer])
        h = _moe_combine_ln2(h1, moe_rows, pos, ln2_g[layer], ln2_b[layer])
    return h.reshape(B, S, D)
```

```python
import functools

import jax
import jax.numpy as jnp
from jax import lax
from jax.experimental import pallas as pl
from jax.experimental.pallas import tpu as pltpu

D_MODEL = 4096
HEAD_DIM = 128
ATTN_GROUPS = ((128, 1), (512, 4), (2048, 16))
HEADS_PER_GROUP = 8
N_ATTN_HEADS = HEADS_PER_GROUP * len(ATTN_GROUPS)
ATTN_WIDTH = N_ATTN_HEADS * HEAD_DIM
ATTN_OUT_WIDTH = HEADS_PER_GROUP * HEAD_DIM
ROT_DIM = HEAD_DIM // 4
ROPE_THETA = 500000.0
ATTN_SPAN = 128
LRU_WIDTH = 2048
LRU_BLOCKS = 8
LRU_BLOCK_W = LRU_WIDTH // LRU_BLOCKS
CONV_WIDTH = 4
RG_C = 8.0
N_GROUPS = 8
EXPERTS_PER_GROUP = 8
N_EXPERTS = N_GROUPS * EXPERTS_PER_GROUP
TOP_K = 2
D_EXPERT = 512
LN_EPS = 1e-5
DEPTH = 1
DEEPNORM_ALPHA = (2 * DEPTH) ** 0.25
K_OFF = ATTN_WIDTH
V_OFF = 2 * ATTN_WIDTH
RX_OFF = 3 * ATTN_WIDTH
RG_OFF = RX_OFF + LRU_WIDTH
GATE_OFF = RG_OFF + LRU_WIDTH
IN_COLS = GATE_OFF + 2 * D_MODEL

LANES = 128
COL_TILE = 1024
PROJ_TM = 1024
PROJ_TN = 512
VMEM_LIMIT = 56 * 1024 * 1024

F32 = jnp.float32
BF16 = jnp.bfloat16


def _params(*sem, vmem=VMEM_LIMIT):
    return pltpu.CompilerParams(dimension_semantics=sem, vmem_limit_bytes=vmem)


def _rope_table_kernel(pos_ref, invf_ref, c_ref, sa_ref, sb_ref):
    half = ROT_DIM // 2
    ang = pos_ref[...].astype(F32) * invf_ref[...]
    lane = lax.broadcasted_iota(jnp.int32, ang.shape, 1)
    cos = jnp.cos(ang)
    sin = jnp.sin(ang)
    c_ref[...] = jnp.where(lane < ROT_DIM, cos, 1.0)
    sa_ref[...] = jnp.where(lane < half, -sin, 0.0)
    sb_ref[...] = jnp.where((lane >= half) & (lane < ROT_DIM), sin, 0.0)


def _rope_tables(positions):
    T = positions.size
    tm = 2048
    half = ROT_DIM // 2
    inv_freq = jnp.power(jnp.float32(ROPE_THETA), -jnp.arange(half, dtype=F32) * 2.0 / ROT_DIM)
    invf = jnp.zeros((1, LANES), F32).at[0, :ROT_DIM].set(jnp.concatenate([inv_freq, inv_freq]))
    tab = jax.ShapeDtypeStruct((T, LANES), F32)
    return pl.pallas_call(
        _rope_table_kernel,
        grid=(T // tm,),
        in_specs=[pl.BlockSpec((tm, 1), lambda i: (i, 0)), pl.BlockSpec((1, LANES), lambda i: (0, 0))],
        out_specs=[pl.BlockSpec((tm, LANES), lambda i: (i, 0))] * 3,
        out_shape=[tab, tab, tab],
        compiler_params=_params("arbitrary"),
        name="rope_tables",
    )(positions.reshape(T, 1), invf)


def _in_proj_kernel(x_ref, w_ref, c_ref, sa_ref, sb_ref, o_ref, wb_ref, slab_ref):
    n = pl.program_id(0)
    m = pl.program_id(1)
    tiles_per_sect = ATTN_WIDTH // PROJ_TN
    tiles_per_group = ATTN_OUT_WIDTH // PROJ_TN
    heads = PROJ_TN // HEAD_DIM

    @pl.when(m == 0)
    def _():
        wb_ref[...] = w_ref[...].astype(BF16)

    acc = jnp.dot(x_ref[...], wb_ref[...], preferred_element_type=F32)
    is_attn = n < 3 * tiles_per_sect

    @pl.when(jnp.logical_not(is_attn))
    def _():
        o_ref[...] = acc.astype(o_ref.dtype)

    @pl.when(is_attn)
    def _():
        rot = n < 2 * tiles_per_sect

        @pl.when(rot)
        def _():
            scale = jnp.where(n < tiles_per_sect, HEAD_DIM ** -0.5, 1.0).astype(F32)
            c, sa, sb = c_ref[...], sa_ref[...], sb_ref[...]
            for h in range(heads):
                t = acc[:, h * HEAD_DIM:(h + 1) * HEAD_DIM]
                u = pltpu.bitcast(t.astype(BF16), jnp.uint32)
                ta = pltpu.bitcast(pltpu.roll(u, HEAD_DIM - ROT_DIM // 2, 1), BF16).astype(F32)
                tb = pltpu.bitcast(pltpu.roll(u, ROT_DIM // 2, 1), BF16).astype(F32)
                slab_ref[h] = (t * c + ta * sa + tb * sb) * scale

        @pl.when(jnp.logical_not(rot))
        def _():
            for h in range(heads):
                slab_ref[h] = acc[:, h * HEAD_DIM:(h + 1) * HEAD_DIM]

        group = (n % tiles_per_sect) // tiles_per_group
        for g, (_, d) in enumerate(ATTN_GROUPS):
            @pl.when(group == g)
            def _(d=d):
                per = PROJ_TM // d
                for h in range(heads):
                    hs = slice(h * HEAD_DIM, (h + 1) * HEAD_DIM)
                    if d == 1:
                        o_ref[:, hs] = slab_ref[h].astype(o_ref.dtype)
                    else:
                        for res in range(d):
                            rows = slab_ref[h, pl.ds(res, per, stride=d), :]
                            o_ref[res * per:(res + 1) * per, hs] = rows.astype(o_ref.dtype)


def _in_proj(xb, w, c, sa, sb):
    T = xb.shape[0]
    tab_spec = pl.BlockSpec((PROJ_TM, LANES), lambda n, m: (m, 0))
    return pl.pallas_call(
        _in_proj_kernel,
        grid=(IN_COLS // PROJ_TN, T // PROJ_TM),
        in_specs=[pl.BlockSpec((PROJ_TM, D_MODEL), lambda n, m: (m, 0)),
                  pl.BlockSpec((D_MODEL, PROJ_TN), lambda n, m: (0, n)),
                  tab_spec, tab_spec, tab_spec],
        out_specs=pl.BlockSpec((PROJ_TM, PROJ_TN), lambda n, m: (m, n)),
        out_shape=jax.ShapeDtypeStruct((T, IN_COLS), BF16),
        scratch_shapes=[pltpu.VMEM((D_MODEL, PROJ_TN), BF16),
                        pltpu.VMEM((PROJ_TN // HEAD_DIM, PROJ_TM, LANES), F32)],
        compiler_params=_params("arbitrary", "arbitrary"),
        name="in_proj",
    )(xb, w, c, sa, sb)


def _attn_kernel(*refs, npc):
    q_refs = refs[:npc]
    kp_refs, kc_refs = refs[npc:2 * npc], refs[2 * npc:3 * npc]
    vp_refs, vc_refs = refs[3 * npc:4 * npc], refs[4 * npc:5 * npc]
    o_ref, lse_ref = refs[5 * npc:]
    n = pl.program_id(2)
    nh, blk = HEADS_PER_GROUP, ATTN_SPAN

    def head(pieces, h):
        hs = slice(h * HEAD_DIM, (h + 1) * HEAD_DIM)
        parts = [r[:, hs] for r in pieces]
        return parts[0] if len(parts) == 1 else jnp.concatenate(parts, axis=0)

    dn = (((1,), (1,)), ((), ()))
    qs = [head(q_refs, h) for h in range(nh)]
    sp = jnp.concatenate([lax.dot_general(qs[h], head(kp_refs, h), dn, preferred_element_type=F32)
                          for h in range(nh)], axis=0)
    sc = jnp.concatenate([lax.dot_general(qs[h], head(kc_refs, h), dn, preferred_element_type=F32)
                          for h in range(nh)], axis=0)
    row = lax.broadcasted_iota(jnp.int32, sp.shape, 0) & (blk - 1)
    col = lax.broadcasted_iota(jnp.int32, sp.shape, 1)
    sp = jnp.where(jnp.logical_and(col >= row, n > 0), sp, -jnp.inf)
    sc = jnp.where(col <= row, sc, -jnp.inf)
    m = jnp.maximum(jnp.max(sp, axis=1, keepdims=True), jnp.max(sc, axis=1, keepdims=True))
    pp = jnp.exp(sp - m)
    pc = jnp.exp(sc - m)
    l = jnp.sum(pp, axis=1, keepdims=True) + jnp.sum(pc, axis=1, keepdims=True)
    inv = 1.0 / l
    lse = m + jnp.log(l)
    pp = pp.astype(BF16)
    pc = pc.astype(BF16)
    lane = lax.broadcasted_iota(jnp.int32, (blk, LANES), 1)
    lse_all = jnp.zeros((blk, LANES), F32)
    outs = []
    for h in range(nh):
        rs = slice(h * blk, (h + 1) * blk)
        o = (jnp.dot(pp[rs], head(vp_refs, h), preferred_element_type=F32)
             + jnp.dot(pc[rs], head(vc_refs, h), preferred_element_type=F32))
        outs.append((o * inv[rs]).astype(o_ref.dtype))
        lse_all = jnp.where(lane == h, lse[rs], lse_all)
    o_ref[0] = jnp.concatenate(outs, axis=1)
    lse_ref[0] = lse_all


def _attention_group(proj, batch, seq, g, dilation):
    L = seq // dilation
    per = PROJ_TM // dilation
    pr = min(per, ATTN_SPAN)
    npc = ATTN_SPAN // pr
    nblk = L // ATTN_SPAN
    tiles_per_batch = seq // PROJ_TM
    kcol, vcol = K_OFF // COL_TILE, V_OFF // COL_TILE

    def piece_block(b, res, n, p):
        pos = n * ATTN_SPAN + p * pr
        row = (b * tiles_per_batch + pos // per) * PROJ_TM + res * per + pos % per
        return row // pr

    def specs(col, back):
        return [pl.BlockSpec((pr, COL_TILE),
                             lambda b, r, n, p=p: (piece_block(b, r, jnp.maximum(n - back, 0), p), col + g))
                for p in range(npc)]

    in_specs = specs(0, 0) + specs(kcol, 1) + specs(kcol, 0) + specs(vcol, 1) + specs(vcol, 0)
    o, lse = pl.pallas_call(
        functools.partial(_attn_kernel, npc=npc),
        grid=(batch, dilation, nblk),
        in_specs=in_specs,
        out_specs=[pl.BlockSpec((1, ATTN_SPAN, ATTN_OUT_WIDTH), lambda b, r, n: (b, n, r)),
                   pl.BlockSpec((1, ATTN_SPAN, LANES), lambda b, r, n: (b, n, r))],
        out_shape=[jax.ShapeDtypeStruct((batch, L, dilation * ATTN_OUT_WIDTH), BF16),
                   jax.ShapeDtypeStruct((batch, L, dilation * LANES), F32)],
        compiler_params=_params("parallel", "parallel", "arbitrary"),
        name=f"attn_d{dilation}",
    )(*([proj] * (5 * npc)))
    return o.reshape(batch * seq, ATTN_OUT_WIDTH), lse.reshape(batch * seq, LANES)


_TAIL = 8


def _rglru_kernel(rx_ref, rg_ref, cw_ref, cb_ref, wa_ref, ba_ref, wx_ref, bx_ref, lam_ref, o_ref,
                  xbuf, a_buf, h_buf, carry):
    tt = rx_ref.shape[0]
    t_idx = pl.program_id(2)

    @pl.when(t_idx == 0)
    def _():
        xbuf[0:_TAIL, :] = jnp.zeros((_TAIL, xbuf.shape[1]), F32)
        carry[...] = jnp.zeros(carry.shape, F32)

    x = rx_ref[...].astype(F32)
    xbuf[_TAIL:_TAIL + tt, :] = x
    xr = cb_ref[...] + cw_ref[0:1, :] * xbuf[_TAIL - 3:_TAIL - 3 + tt, :]
    for j in range(1, CONV_WIDTH):
        xr = xr + cw_ref[j:j + 1, :] * xbuf[_TAIL - 3 + j:_TAIL - 3 + j + tt, :]
    xbuf[0:_TAIL, :] = x[tt - _TAIL:tt, :]

    softplus_neg_lam = jnp.log1p(jnp.exp(-lam_ref[...]))
    for blk in range(xr.shape[1] // LRU_BLOCK_W):
        cs = slice(blk * LRU_BLOCK_W, (blk + 1) * LRU_BLOCK_W)
        xb = xr[:, cs]
        xb16 = xb.astype(BF16)
        r = jax.nn.sigmoid(jnp.dot(xb16, wa_ref[blk], preferred_element_type=F32) + ba_ref[:, cs])
        i = jax.nn.sigmoid(jnp.dot(xb16, wx_ref[blk], preferred_element_type=F32) + bx_ref[:, cs])
        log_a = (-RG_C) * r * softplus_neg_lam[:, cs]
        a = jnp.exp(log_a)
        a_buf[:, cs] = a
        h_buf[:, cs] = jnp.sqrt(1.0 - a * a) * (i * xb)

    def step(t, h):
        h = a_buf[pl.ds(t, 1), :] * h + h_buf[pl.ds(t, 1), :]
        h_buf[pl.ds(t, 1), :] = h
        return h

    carry[...] = lax.fori_loop(0, tt, step, carry[...], unroll=8)
    o_ref[...] = (h_buf[...] * jax.nn.gelu(rg_ref[...].astype(F32))).astype(o_ref.dtype)


def _rglru(proj, batch, seq, conv_w, conv_b, wa, ba, wx, bx, lam):
    tt = 512
    nt = seq // tt
    hw = COL_TILE
    assert hw % LRU_BLOCK_W == 0 and RX_OFF % hw == 0 and RG_OFF % hw == 0
    gb = hw // LRU_BLOCK_W
    row = lambda off: pl.BlockSpec((tt, hw), lambda b, c, t: (b * nt + t, off // hw + c))
    vec = pl.BlockSpec((1, hw), lambda b, c, t: (0, c))
    wspec = pl.BlockSpec((gb, LRU_BLOCK_W, LRU_BLOCK_W), lambda b, c, t: (c, 0, 0))
    return pl.pallas_call(
        _rglru_kernel,
        grid=(batch, LRU_WIDTH // hw, nt),
        in_specs=[row(RX_OFF), row(RG_OFF), pl.BlockSpec((CONV_WIDTH, hw), lambda b, c, t: (0, c)), vec,
                  wspec, vec, wspec, vec, vec],
        out_specs=pl.BlockSpec((tt, hw), lambda b, c, t: (b * nt + t, c)),
        out_shape=jax.ShapeDtypeStruct((batch * seq, LRU_WIDTH), BF16),
        scratch_shapes=[pltpu.VMEM((tt + _TAIL, hw), F32), pltpu.VMEM((tt, hw), F32),
                        pltpu.VMEM((tt, hw), F32), pltpu.VMEM((1, hw), F32)],
        compiler_params=_params("parallel", "parallel", "arbitrary"),
        name="rg_lru",
    )(proj, proj, conv_w, conv_b.reshape(1, -1), wa, ba.reshape(1, -1), wx, bx.reshape(1, -1), lam.reshape(1, -1))


def _merge_kernel(*refs):
    o0_ref, o1_ref, o2_ref, l0_ref, l1_ref, l2_ref, rec_ref = refs[:7]
    nct = D_MODEL // COL_TILE
    ga_refs = refs[7:7 + nct]
    gr_refs = refs[7 + nct:7 + 2 * nct]
    bg_ref, wa_ref, wr_ref, out_ref = refs[7 + 2 * nct:]
    l0, l1, l2 = l0_ref[...], l1_ref[...], l2_ref[...]
    parts = []
    for h in range(HEADS_PER_GROUP):
        hs = slice(h * HEAD_DIM, (h + 1) * HEAD_DIM)
        a0, a1, a2 = l0[:, h:h + 1], l1[:, h:h + 1], l2[:, h:h + 1]
        m = jnp.maximum(jnp.maximum(a0, a1), a2)
        e0, e1, e2 = jnp.exp(a0 - m), jnp.exp(a1 - m), jnp.exp(a2 - m)
        inv = 1.0 / (e0 + e1 + e2)
        mixed = ((e0 * inv) * o0_ref[:, hs].astype(F32) + (e1 * inv) * o1_ref[:, hs].astype(F32)
                 + (e2 * inv) * o2_ref[:, hs].astype(F32))
        parts.append(mixed.astype(BF16))
    attn = jnp.concatenate(parts, axis=1)
    rec = rec_ref[...]
    for j in range(nct):
        cs = slice(j * COL_TILE, (j + 1) * COL_TILE)
        ya = jnp.dot(attn, wa_ref[:, cs], preferred_element_type=F32)
        yr = jnp.dot(rec, wr_ref[:, cs], preferred_element_type=F32)
        gate_a = jax.nn.sigmoid(ga_refs[j][...].astype(F32) + bg_ref[0:1, cs])
        gate_r = jax.nn.sigmoid(gr_refs[j][...].astype(F32) + bg_ref[1:2, cs])
        out_ref[:, cs] = (gate_a * ya + gate_r * yr).astype(out_ref.dtype)


def _merge(os_, lses, rec, proj, b_gate, wa, wr):
    T = rec.shape[0]
    tm = 256
    nct = D_MODEL // COL_TILE
    row = lambda m: (m, 0)
    const = lambda m: (0, 0)
    gate_specs = [pl.BlockSpec((tm, COL_TILE), lambda m, c=(GATE_OFF + k * D_MODEL) // COL_TILE + j: (m, c))
                  for k in range(2) for j in range(nct)]
    o_spec = pl.BlockSpec((tm, ATTN_OUT_WIDTH), row)
    l_spec = pl.BlockSpec((tm, LANES), row)
    resident = lambda shape: pl.BlockSpec(shape, const, pipeline_mode=pl.Buffered(1))
    return pl.pallas_call(
        _merge_kernel,
        grid=(T // tm,),
        in_specs=[o_spec, o_spec, o_spec, l_spec, l_spec, l_spec, pl.BlockSpec((tm, LRU_WIDTH), row)]
                 + gate_specs
                 + [resident((2, D_MODEL)), resident((ATTN_OUT_WIDTH, D_MODEL)), resident((LRU_WIDTH, D_MODEL))],
        out_specs=pl.BlockSpec((tm, D_MODEL), row),
        out_shape=jax.ShapeDtypeStruct((T, D_MODEL), BF16),
        compiler_params=_params("parallel"),
        name="merge",
    )(*os_, *lses, rec, *([proj] * (2 * nct)), b_gate, wa, wr)


def _layer_norm(z, g, b):
    mu = jnp.mean(z, axis=-1, keepdims=True)
    zc = z - mu
    var = jnp.mean(zc * zc, axis=-1, keepdims=True)
    return zc * lax.rsqrt(var + LN_EPS) * g + b


def _out_proj_kernel(mg_ref, w_ref, x_ref, g_ref, b_ref, wr_ref, br_ref, h_ref, lg_ref):
    n = pl.program_id(1)
    nn = pl.num_programs(1)
    tn = w_ref.shape[1]
    mix = jnp.dot(mg_ref[...], w_ref[...], preferred_element_type=F32)
    for j in range(D_MODEL // tn):
        @pl.when(n == j)
        def _(j=j):
            h_ref[:, j * tn:(j + 1) * tn] = mix

    @pl.when(n == nn - 1)
    def _():
        rc = 128
        for i in range(h_ref.shape[0] // rc):
            rs = slice(i * rc, (i + 1) * rc)
            z = DEEPNORM_ALPHA * x_ref[rs, :] + h_ref[rs, :]
            y = _layer_norm(z, g_ref[...], b_ref[...])
            h_ref[rs, :] = y
            lg = jnp.dot(y.astype(BF16), wr_ref[...], preferred_element_type=F32)
            lg_ref[rs, :] = lg + br_ref[...]


def _out_proj(merged, w_out, x2, g, b, wr, br):
    T = merged.shape[0]
    tm, tn = 512, 512
    row = lambda m, n: (m, 0)
    const = lambda m, n: (0, 0)
    resident = lambda shape: pl.BlockSpec(shape, const, pipeline_mode=pl.Buffered(1))
    return pl.pallas_call(
        _out_proj_kernel,
        grid=(T // tm, D_MODEL // tn),
        in_specs=[pl.BlockSpec((tm, D_MODEL), row),
                  pl.BlockSpec((D_MODEL, tn), lambda m, n: (0, n)),
                  pl.BlockSpec((tm, D_MODEL), row),
                  resident((1, D_MODEL)), resident((1, D_MODEL)),
                  resident((D_MODEL, LANES)), resident((1, LANES))],
        out_specs=[pl.BlockSpec((tm, D_MODEL), row), pl.BlockSpec((tm, LANES), row)],
        out_shape=[jax.ShapeDtypeStruct((T, D_MODEL), F32), jax.ShapeDtypeStruct((T, LANES), F32)],
        compiler_params=_params("parallel", "arbitrary"),
        name="out_proj_ln1",
    )(merged, w_out, x2, g.reshape(1, -1), b.reshape(1, -1), wr, br)


MOE_TM = 256


def _moe_up_kernel(be_ref, nv_ref, tok_ref, h_hbm, wg_ref, wu_ref, hid_ref, xbuf, sem):
    b = pl.program_id(0)
    nv = nv_ref[0]

    def row_copy(tok, slot, r):
        return pltpu.make_async_copy(h_hbm.at[pl.ds(tok, 1), :], xbuf.at[slot, pl.ds(r, 1), :], sem.at[slot])

    def start_gather_loop(blk, slot):
        def body(r, c):
            row_copy(tok_ref[blk * MOE_TM + r], slot, r).start()
            return c
        lax.fori_loop(0, MOE_TM, body, 0, unroll=8)

    def start_gather_inline(blk, slot):
        for r in range(MOE_TM):
            row_copy(tok_ref[blk * MOE_TM + r], slot, r).start()

    def wait_gather(slot):
        def body(r, c):
            row_copy(0, slot, r).wait()
            return c
        lax.fori_loop(0, MOE_TM, body, 0, unroll=8)

    def compute(slot):
        x = xbuf[slot].astype(BF16)
        gate = jnp.dot(x, wg_ref[0].astype(BF16), preferred_element_type=F32)
        up = jnp.dot(x, wu_ref[0].astype(BF16), preferred_element_type=F32)
        hid_ref[...] = (gate * jax.nn.sigmoid(gate) * up).astype(hid_ref.dtype)

    @pl.when(b == 0)
    def _():
        start_gather_loop(0, 0)

    @pl.when(b + 1 < nv)
    def _():
        wait_gather(b % 2)
        start_gather_inline(b + 1, (b + 1) % 2)
        compute(b % 2)

    @pl.when(b + 1 == nv)
    def _():
        wait_gather(b % 2)
        compute(b % 2)

    @pl.when(b >= nv)
    def _():
        hid_ref[...] = jnp.zeros(hid_ref.shape, hid_ref.dtype)


def _moe_down_kernel(be_ref, nv_ref, hid_ref, rw_ref, wd_ref, out_ref):
    b = pl.program_id(0)
    nv = nv_ref[0]

    @pl.when(b < nv)
    def _():
        y = jnp.dot(hid_ref[...], wd_ref[0].astype(BF16), preferred_element_type=F32)
        out_ref[...] = y * rw_ref[...]

    @pl.when(b >= nv)
    def _():
        out_ref[...] = jnp.zeros(out_ref.shape, out_ref.dtype)


def _moe_experts(h1, row_tok, row_w, block_e, n_valid, wg, wu, wd):
    rows = row_tok.shape[0]
    nb = rows // MOE_TM
    hid = pl.pallas_call(
        _moe_up_kernel,
        grid_spec=pltpu.PrefetchScalarGridSpec(
            num_scalar_prefetch=3,
            grid=(nb,),
            in_specs=[pl.BlockSpec(memory_space=pl.ANY),
                      pl.BlockSpec((1, D_MODEL, D_EXPERT), lambda b, be, nv, tok: (be[b], 0, 0)),
                      pl.BlockSpec((1, D_MODEL, D_EXPERT), lambda b, be, nv, tok: (be[b], 0, 0))],
            out_specs=pl.BlockSpec((MOE_TM, D_EXPERT), lambda b, be, nv, tok: (b, 0)),
            scratch_shapes=[pltpu.VMEM((2, MOE_TM, D_MODEL), F32), pltpu.SemaphoreType.DMA((2,))]),
        out_shape=jax.ShapeDtypeStruct((rows, D_EXPERT), BF16),
        compiler_params=_params("arbitrary"),
        name="moe_up",
    )(block_e, n_valid, row_tok, h1, wg, wu)
    return pl.pallas_call(
        _moe_down_kernel,
        grid_spec=pltpu.PrefetchScalarGridSpec(
            num_scalar_prefetch=2,
            grid=(nb,),
            in_specs=[pl.BlockSpec((MOE_TM, D_EXPERT), lambda b, be, nv: (b, 0)),
                      pl.BlockSpec((MOE_TM, 1), lambda b, be, nv: (b, 0)),
                      pl.BlockSpec((1, D_EXPERT, D_MODEL), lambda b, be, nv: (be[b], 0, 0))],
            out_specs=pl.BlockSpec((MOE_TM, D_MODEL), lambda b, be, nv: (b, 0))),
        out_shape=jax.ShapeDtypeStruct((rows, D_MODEL), F32),
        compiler_params=_params("arbitrary"),
        name="moe_down",
    )(block_e, n_valid, hid, row_w.reshape(rows, 1), wd)


FIN_TM = 256


def _final_kernel(pos_ref, h_ref, y_hbm, g_ref, b_ref, out_ref, ybuf, sem):
    i = pl.program_id(0)
    n = pl.num_programs(0)

    def row_copy(src_row, slot, r):
        return pltpu.make_async_copy(y_hbm.at[pl.ds(src_row, 1), :], ybuf.at[slot, pl.ds(r, 1), :], sem.at[slot])

    def start_gather(tile, slot):
        def body(r, c):
            for k in range(TOP_K):
                row_copy(pos_ref[(tile * FIN_TM + r) * TOP_K + k], slot, k * FIN_TM + r).start()
            return c
        lax.fori_loop(0, FIN_TM, body, 0, unroll=4)

    def wait_gather(slot):
        def body(r, c):
            row_copy(0, slot, r).wait()
            return c
        lax.fori_loop(0, TOP_K * FIN_TM, body, 0, unroll=8)

    @pl.when(i == 0)
    def _():
        start_gather(0, 0)

    @pl.when(i + 1 < n)
    def _():
        start_gather(i + 1, (i + 1) % 2)

    slot = i % 2
    wait_gather(slot)
    ff = ybuf[slot, 0:FIN_TM, :]
    for k in range(1, TOP_K):
        ff = ff + ybuf[slot, k * FIN_TM:(k + 1) * FIN_TM, :]
    z = DEEPNORM_ALPHA * h_ref[...] + ff
    out_ref[...] = _layer_norm(z, g_ref[...], b_ref[...])


def _moe_combine_ln2(h1, moe_rows, pos, g, b):
    T = h1.shape[0]
    const = lambda i, pos: (0, 0)
    grid_spec = pltpu.PrefetchScalarGridSpec(
        num_scalar_prefetch=1,
        grid=(T // FIN_TM,),
        in_specs=[pl.BlockSpec((FIN_TM, D_MODEL), lambda i, pos: (i, 0)),
                  pl.BlockSpec(memory_space=pl.ANY),
                  pl.BlockSpec((1, D_MODEL), const), pl.BlockSpec((1, D_MODEL), const)],
        out_specs=pl.BlockSpec((FIN_TM, D_MODEL), lambda i, pos: (i, 0)),
        scratch_shapes=[pltpu.VMEM((2, TOP_K * FIN_TM, D_MODEL), F32), pltpu.SemaphoreType.DMA((2,))],
    )
    return pl.pallas_call(
        _final_kernel,
        grid_spec=grid_spec,
        out_shape=jax.ShapeDtypeStruct((T, D_MODEL), F32),
        compiler_params=_params("arbitrary"),
        name="moe_combine_ln2",
    )(pos, h1, moe_rows, g.reshape(1, -1), b.reshape(1, -1))


def _routing(logits):
    T = logits.shape[0]
    logits_g = logits[:, :N_GROUPS]
    logits_e = logits[:, N_GROUPS:N_GROUPS + N_EXPERTS].reshape(T, N_GROUPS, EXPERTS_PER_GROUP)
    p_g = jax.nn.softmax(logits_g, axis=-1)
    g_idx = jnp.argmax(logits_g, axis=-1)
    p_grp = jnp.take_along_axis(p_g, g_idx[:, None], axis=-1)
    le = jnp.take_along_axis(logits_e, g_idx[:, None, None], axis=1)[:, 0]
    top_v, top_i = lax.top_k(le, TOP_K)
    weight = p_grp * jax.nn.softmax(top_v, axis=-1)
    expert_id = g_idx[:, None].astype(jnp.int32) * EXPERTS_PER_GROUP + top_i.astype(jnp.int32)

    A = T * TOP_K
    flat_e = expert_id.reshape(-1)
    flat_w = weight.reshape(-1)
    onehot = (flat_e[:, None] == jnp.arange(N_EXPERTS, dtype=jnp.int32)[None, :]).astype(jnp.int32)
    csum = jnp.cumsum(onehot, axis=0)
    counts = csum[-1]
    rank = jnp.take_along_axis(csum, flat_e[:, None], axis=1)[:, 0] - 1
    starts = jnp.cumsum(counts) - counts
    padded = (counts + MOE_TM - 1) // MOE_TM * MOE_TM
    pends = jnp.cumsum(padded)
    pstarts = pends - padded
    pos = (pstarts[flat_e] + rank).astype(jnp.int32)
    order = jnp.argsort(flat_e, stable=True).astype(jnp.int32)
    nb = A // MOE_TM + N_EXPERTS
    rows = nb * MOE_TM
    r = jnp.arange(rows, dtype=jnp.int32)
    e_r = jnp.minimum(jnp.sum((r[:, None] >= pends[None, :]).astype(jnp.int32), axis=1), N_EXPERTS - 1)
    i_in = r - pstarts[e_r]
    valid = i_in < counts[e_r]
    a_r = order[jnp.clip(starts[e_r] + i_in, 0, A - 1)]
    row_tok = jnp.where(valid, a_r // TOP_K, 0).astype(jnp.int32)
    row_w = jnp.where(valid, flat_w[a_r], 0.0).astype(F32)
    n_valid = (pends[-1] // MOE_TM).astype(jnp.int32)
    blk = jnp.arange(nb, dtype=jnp.int32)
    block_e = e_r[jnp.minimum(blk, n_valid - 1) * MOE_TM]
    return row_tok, row_w, block_e, n_valid.reshape(1), pos


def kernel(x, positions, w_in, b_gate, conv_w, conv_b, w_rg_a, b_rg_a, w_rg_x, b_rg_x, lru_lambda, w_attn_proj, w_rec_proj, w_out, ln1_g, ln1_b, w_router_group, b_router_group, w_router_expert, b_router_expert, w_gate, w_up, w_down, ln2_g, ln2_b):
    B, S, D = x.shape
    T = B * S
    h = x.reshape(T, D)
    c, sa, sb = _rope_tables(positions)
    for layer in range(DEPTH):
        proj = _in_proj(h.astype(BF16), w_in[layer], c, sa, sb)
        os_, lses = [], []
        for g, (_, dilation) in enumerate(ATTN_GROUPS):
            o, lse = _attention_group(proj, B, S, g, dilation)
            os_.append(o)
            lses.append(lse)
        rec = _rglru(proj, B, S, conv_w[layer], conv_b[layer], w_rg_a[layer].astype(BF16), b_rg_a[layer],
                     w_rg_x[layer].astype(BF16), b_rg_x[layer], lru_lambda[layer])
        merged = _merge(os_, lses, rec, proj, b_gate[layer], w_attn_proj[layer].astype(BF16),
                        w_rec_proj[layer].astype(BF16))
        w_router = jnp.concatenate([w_router_group[layer], w_router_expert[layer]], axis=1)
        w_router = jnp.pad(w_router, ((0, 0), (0, LANES - w_router.shape[1])))
        b_router = jnp.concatenate([b_router_group[layer], b_router_expert[layer]])
        b_router = jnp.pad(b_router, (0, LANES - b_router.shape[0])).reshape(1, LANES)
        h1, logits = _out_proj(merged, w_out[layer].astype(BF16), h, ln1_g[layer], ln1_b[layer],
                               w_router.astype(BF16), b_router)
        row_tok, row_w, block_e, n_valid, pos = _routing(logits)
        moe_rows = _moe_experts(h1, row_tok, row_w, block_e, n_valid, w_gate[layer], w_up[layer], w_down[layer])
        h = _moe_combine_ln2(h1, moe_rows, pos, ln2_g[layer], ln2_b[layer])
    return h.reshape(B, S, D)
```

```python
import functools

import jax
import jax.numpy as jnp
from jax import lax
from jax.experimental import pallas as pl
from jax.experimental.pallas import tpu as pltpu

D_MODEL = 4096
HEAD_DIM = 128
ATTN_GROUPS = ((128, 1), (512, 4), (2048, 16))
HEADS_PER_GROUP = 8
N_ATTN_HEADS = HEADS_PER_GROUP * len(ATTN_GROUPS)
ATTN_WIDTH = N_ATTN_HEADS * HEAD_DIM
ATTN_OUT_WIDTH = HEADS_PER_GROUP * HEAD_DIM
ROT_DIM = HEAD_DIM // 4
ROPE_THETA = 500000.0
ATTN_SPAN = 128
LRU_WIDTH = 2048
LRU_BLOCKS = 8
LRU_BLOCK_W = LRU_WIDTH // LRU_BLOCKS
CONV_WIDTH = 4
RG_C = 8.0
N_GROUPS = 8
EXPERTS_PER_GROUP = 8
N_EXPERTS = N_GROUPS * EXPERTS_PER_GROUP
TOP_K = 2
D_EXPERT = 512
LN_EPS = 1e-5
DEPTH = 1
DEEPNORM_ALPHA = (2 * DEPTH) ** 0.25
K_OFF = ATTN_WIDTH
V_OFF = 2 * ATTN_WIDTH
RX_OFF = 3 * ATTN_WIDTH
RG_OFF = RX_OFF + LRU_WIDTH
GATE_OFF = RG_OFF + LRU_WIDTH
IN_COLS = GATE_OFF + 2 * D_MODEL

LANES = 128
COL_TILE = 1024
PROJ_TM = 1024
PROJ_TN = 512
VMEM_LIMIT = 56 * 1024 * 1024

F32 = jnp.float32
BF16 = jnp.bfloat16


def _params(*sem, vmem=VMEM_LIMIT):
    return pltpu.CompilerParams(dimension_semantics=sem, vmem_limit_bytes=vmem)


def _rope_table_kernel(pos_ref, invf_ref, c_ref, sa_ref, sb_ref):
    half = ROT_DIM // 2
    ang = pos_ref[...].astype(F32) * invf_ref[...]
    lane = lax.broadcasted_iota(jnp.int32, ang.shape, 1)
    cos = jnp.cos(ang)
    sin = jnp.sin(ang)
    c_ref[...] = jnp.where(lane < ROT_DIM, cos, 1.0)
    sa_ref[...] = jnp.where(lane < half, -sin, 0.0)
    sb_ref[...] = jnp.where((lane >= half) & (lane < ROT_DIM), sin, 0.0)


def _rope_tables(positions):
    T = positions.size
    tm = 2048
    half = ROT_DIM // 2
    inv_freq = jnp.power(jnp.float32(ROPE_THETA), -jnp.arange(half, dtype=F32) * 2.0 / ROT_DIM)
    invf = jnp.zeros((1, LANES), F32).at[0, :ROT_DIM].set(jnp.concatenate([inv_freq, inv_freq]))
    tab = jax.ShapeDtypeStruct((T, LANES), F32)
    return pl.pallas_call(
        _rope_table_kernel,
        grid=(T // tm,),
        in_specs=[pl.BlockSpec((tm, 1), lambda i: (i, 0)), pl.BlockSpec((1, LANES), lambda i: (0, 0))],
        out_specs=[pl.BlockSpec((tm, LANES), lambda i: (i, 0))] * 3,
        out_shape=[tab, tab, tab],
        compiler_params=_params("arbitrary"),
        name="rope_tables",
    )(positions.reshape(T, 1), invf)


def _in_proj_kernel(x_ref, w_ref, c_ref, sa_ref, sb_ref, o_ref, wb_ref, slab_ref):
    n = pl.program_id(0)
    m = pl.program_id(1)
    tiles_per_sect = ATTN_WIDTH // PROJ_TN
    tiles_per_group = ATTN_OUT_WIDTH // PROJ_TN
    heads = PROJ_TN // HEAD_DIM

    @pl.when(m == 0)
    def _():
        wb_ref[...] = w_ref[...].astype(BF16)

    acc = jnp.dot(x_ref[...], wb_ref[...], preferred_element_type=F32)
    is_attn = n < 3 * tiles_per_sect

    @pl.when(jnp.logical_not(is_attn))
    def _():
        o_ref[...] = acc.astype(o_ref.dtype)

    @pl.when(is_attn)
    def _():
        rot = n < 2 * tiles_per_sect

        @pl.when(rot)
        def _():
            scale = jnp.where(n < tiles_per_sect, HEAD_DIM ** -0.5, 1.0).astype(F32)
            c, sa, sb = c_ref[...], sa_ref[...], sb_ref[...]
            for h in range(heads):
                t = acc[:, h * HEAD_DIM:(h + 1) * HEAD_DIM]
                u = pltpu.bitcast(t.astype(BF16), jnp.uint32)
                ta = pltpu.bitcast(pltpu.roll(u, HEAD_DIM - ROT_DIM // 2, 1), BF16).astype(F32)
                tb = pltpu.bitcast(pltpu.roll(u, ROT_DIM // 2, 1), BF16).astype(F32)
                slab_ref[h] = (t * c + ta * sa + tb * sb) * scale

        @pl.when(jnp.logical_not(rot))
        def _():
            for h in range(heads):
                slab_ref[h] = acc[:, h * HEAD_DIM:(h + 1) * HEAD_DIM]

        group = (n % tiles_per_sect) // tiles_per_group
        for g, (_, d) in enumerate(ATTN_GROUPS):
            @pl.when(group == g)
            def _(d=d):
                per = PROJ_TM // d
                for h in range(heads):
                    hs = slice(h * HEAD_DIM, (h + 1) * HEAD_DIM)
                    if d == 1:
                        o_ref[:, hs] = slab_ref[h].astype(o_ref.dtype)
                    else:
                        for res in range(d):
                            rows = slab_ref[h, pl.ds(res, per, stride=d), :]
                            o_ref[res * per:(res + 1) * per, hs] = rows.astype(o_ref.dtype)


def _in_proj(xb, w, c, sa, sb):
    T = xb.shape[0]
    tab_spec = pl.BlockSpec((PROJ_TM, LANES), lambda n, m: (m, 0))
    return pl.pallas_call(
        _in_proj_kernel,
        grid=(IN_COLS // PROJ_TN, T // PROJ_TM),
        in_specs=[pl.BlockSpec((PROJ_TM, D_MODEL), lambda n, m: (m, 0)),
                  pl.BlockSpec((D_MODEL, PROJ_TN), lambda n, m: (0, n)),
                  tab_spec, tab_spec, tab_spec],
        out_specs=pl.BlockSpec((PROJ_TM, PROJ_TN), lambda n, m: (m, n)),
        out_shape=jax.ShapeDtypeStruct((T, IN_COLS), BF16),
        scratch_shapes=[pltpu.VMEM((D_MODEL, PROJ_TN), BF16),
                        pltpu.VMEM((PROJ_TN // HEAD_DIM, PROJ_TM, LANES), F32)],
        compiler_params=_params("arbitrary", "arbitrary"),
        name="in_proj",
    )(xb, w, c, sa, sb)


def _attn_kernel(*refs, npc):
    q_refs = refs[:npc]
    kp_refs, kc_refs = refs[npc:2 * npc], refs[2 * npc:3 * npc]
    vp_refs, vc_refs = refs[3 * npc:4 * npc], refs[4 * npc:5 * npc]
    o_ref, lse_ref = refs[5 * npc:]
    n = pl.program_id(2)
    nh, blk = HEADS_PER_GROUP, ATTN_SPAN

    def head(pieces, h):
        hs = slice(h * HEAD_DIM, (h + 1) * HEAD_DIM)
        parts = [r[:, hs] for r in pieces]
        return parts[0] if len(parts) == 1 else jnp.concatenate(parts, axis=0)

    dn = (((1,), (1,)), ((), ()))
    qs = [head(q_refs, h) for h in range(nh)]
    sp = jnp.concatenate([lax.dot_general(qs[h], head(kp_refs, h), dn, preferred_element_type=F32)
                          for h in range(nh)], axis=0)
    sc = jnp.concatenate([lax.dot_general(qs[h], head(kc_refs, h), dn, preferred_element_type=F32)
                          for h in range(nh)], axis=0)
    row = lax.broadcasted_iota(jnp.int32, sp.shape, 0) & (blk - 1)
    col = lax.broadcasted_iota(jnp.int32, sp.shape, 1)
    sp = jnp.where(jnp.logical_and(col >= row, n > 0), sp, -jnp.inf)
    sc = jnp.where(col <= row, sc, -jnp.inf)
    m = jnp.maximum(jnp.max(sp, axis=1, keepdims=True), jnp.max(sc, axis=1, keepdims=True))
    pp = jnp.exp(sp - m)
    pc = jnp.exp(sc - m)
    l = jnp.sum(pp, axis=1, keepdims=True) + jnp.sum(pc, axis=1, keepdims=True)
    inv = 1.0 / l
    lse = m + jnp.log(l)
    pp = pp.astype(BF16)
    pc = pc.astype(BF16)
    lane = lax.broadcasted_iota(jnp.int32, (blk, LANES), 1)
    lse_all = jnp.zeros((blk, LANES), F32)
    outs = []
    for h in range(nh):
        rs = slice(h * blk, (h + 1) * blk)
        o = (jnp.dot(pp[rs], head(vp_refs, h), preferred_element_type=F32)
             + jnp.dot(pc[rs], head(vc_refs, h), preferred_element_type=F32))
        outs.append((o * inv[rs]).astype(o_ref.dtype))
        lse_all = jnp.where(lane == h, lse[rs], lse_all)
    o_ref[0] = jnp.concatenate(outs, axis=1)
    lse_ref[0] = lse_all


def _attention_group(proj, batch, seq, g, dilation):
    L = seq // dilation
    per = PROJ_TM // dilation
    pr = min(per, ATTN_SPAN)
    npc = ATTN_SPAN // pr
    nblk = L // ATTN_SPAN
    tiles_per_batch = seq // PROJ_TM
    kcol, vcol = K_OFF // COL_TILE, V_OFF // COL_TILE

    def piece_block(b, res, n, p):
        pos = n * ATTN_SPAN + p * pr
        row = (b * tiles_per_batch + pos // per) * PROJ_TM + res * per + pos % per
        return row // pr

    def specs(col, back):
        return [pl.BlockSpec((pr, COL_TILE),
                             lambda b, r, n, p=p: (piece_block(b, r, jnp.maximum(n - back, 0), p), col + g))
                for p in range(npc)]

    in_specs = specs(0, 0) + specs(kcol, 1) + specs(kcol, 0) + specs(vcol, 1) + specs(vcol, 0)
    o, lse = pl.pallas_call(
        functools.partial(_attn_kernel, npc=npc),
        grid=(batch, dilation, nblk),
        in_specs=in_specs,
        out_specs=[pl.BlockSpec((1, ATTN_SPAN, ATTN_OUT_WIDTH), lambda b, r, n: (b, n, r)),
                   pl.BlockSpec((1, ATTN_SPAN, LANES), lambda b, r, n: (b, n, r))],
        out_shape=[jax.ShapeDtypeStruct((batch, L, dilation * ATTN_OUT_WIDTH), BF16),
                   jax.ShapeDtypeStruct((batch, L, dilation * LANES), F32)],
        compiler_params=_params("parallel", "parallel", "arbitrary"),
        name=f"attn_d{dilation}",
    )(*([proj] * (5 * npc)))
    return o.reshape(batch * seq, ATTN_OUT_WIDTH), lse.reshape(batch * seq, LANES)


_TAIL = 8


def _rglru_kernel(rx_ref, rg_ref, cw_ref, cb_ref, wa_ref, ba_ref, wx_ref, bx_ref, lam_ref, o_ref,
                  xbuf, a_buf, h_buf, carry):
    tt = rx_ref.shape[0]
    t_idx = pl.program_id(2)

    @pl.when(t_idx == 0)
    def _():
        xbuf[0:_TAIL, :] = jnp.zeros((_TAIL, xbuf.shape[1]), F32)
        carry[...] = jnp.zeros(carry.shape, F32)

    x = rx_ref[...].astype(F32)
    xbuf[_TAIL:_TAIL + tt, :] = x
    xr = cb_ref[...] + cw_ref[0:1, :] * xbuf[_TAIL - 3:_TAIL - 3 + tt, :]
    for j in range(1, CONV_WIDTH):
        xr = xr + cw_ref[j:j + 1, :] * xbuf[_TAIL - 3 + j:_TAIL - 3 + j + tt, :]
    xbuf[0:_TAIL, :] = x[tt - _TAIL:tt, :]

    softplus_neg_lam = jnp.log1p(jnp.exp(-lam_ref[...]))
    for blk in range(xr.shape[1] // LRU_BLOCK_W):
        cs = slice(blk * LRU_BLOCK_W, (blk + 1) * LRU_BLOCK_W)
        xb = xr[:, cs]
        xb16 = xb.astype(BF16)
        r = jax.nn.sigmoid(jnp.dot(xb16, wa_ref[blk], preferred_element_type=F32) + ba_ref[:, cs])
        i = jax.nn.sigmoid(jnp.dot(xb16, wx_ref[blk], preferred_element_type=F32) + bx_ref[:, cs])
        log_a = (-RG_C) * r * softplus_neg_lam[:, cs]
        a = jnp.exp(log_a)
        a_buf[:, cs] = a
        h_buf[:, cs] = jnp.sqrt(1.0 - a * a) * (i * xb)

    def step(t, h):
        h = a_buf[pl.ds(t, 1), :] * h + h_buf[pl.ds(t, 1), :]
        h_buf[pl.ds(t, 1), :] = h
        return h

    carry[...] = lax.fori_loop(0, tt, step, carry[...], unroll=8)
    o_ref[...] = (h_buf[...] * jax.nn.gelu(rg_ref[...].astype(F32))).astype(o_ref.dtype)


def _rglru(proj, batch, seq, conv_w, conv_b, wa, ba, wx, bx, lam):
    tt = 512
    nt = seq // tt
    hw = COL_TILE
    assert hw % LRU_BLOCK_W == 0 and RX_OFF % hw == 0 and RG_OFF % hw == 0
    gb = hw // LRU_BLOCK_W
    row = lambda off: pl.BlockSpec((tt, hw), lambda b, c, t: (b * nt + t, off // hw + c))
    vec = pl.BlockSpec((1, hw), lambda b, c, t: (0, c))
    wspec = pl.BlockSpec((gb, LRU_BLOCK_W, LRU_BLOCK_W), lambda b, c, t: (c, 0, 0))
    return pl.pallas_call(
        _rglru_kernel,
        grid=(batch, LRU_WIDTH // hw, nt),
        in_specs=[row(RX_OFF), row(RG_OFF), pl.BlockSpec((CONV_WIDTH, hw), lambda b, c, t: (0, c)), vec,
                  wspec, vec, wspec, vec, vec],
        out_specs=pl.BlockSpec((tt, hw), lambda b, c, t: (b * nt + t, c)),
        out_shape=jax.ShapeDtypeStruct((batch * seq, LRU_WIDTH), BF16),
        scratch_shapes=[pltpu.VMEM((tt + _TAIL, hw), F32), pltpu.VMEM((tt, hw), F32),
                        pltpu.VMEM((tt, hw), F32), pltpu.VMEM((1, hw), F32)],
        compiler_params=_params("parallel", "parallel", "arbitrary"),
        name="rg_lru",
    )(proj, proj, conv_w, conv_b.reshape(1, -1), wa, ba.reshape(1, -1), wx, bx.reshape(1, -1), lam.reshape(1, -1))


def _merge_kernel(*refs):
    o0_ref, o1_ref, o2_ref, l0_ref, l1_ref, l2_ref, rec_ref = refs[:7]
    nct = D_MODEL // COL_TILE
    ga_refs = refs[7:7 + nct]
    gr_refs = refs[7 + nct:7 + 2 * nct]
    bg_ref, wa_ref, wr_ref, out_ref = refs[7 + 2 * nct:]
    l0, l1, l2 = l0_ref[...], l1_ref[...], l2_ref[...]
    parts = []
    for h in range(HEADS_PER_GROUP):
        hs = slice(h * HEAD_DIM, (h + 1) * HEAD_DIM)
        a0, a1, a2 = l0[:, h:h + 1], l1[:, h:h + 1], l2[:, h:h + 1]
        m = jnp.maximum(jnp.maximum(a0, a1), a2)
        e0, e1, e2 = jnp.exp(a0 - m), jnp.exp(a1 - m), jnp.exp(a2 - m)
        inv = 1.0 / (e0 + e1 + e2)
        mixed = ((e0 * inv) * o0_ref[:, hs].astype(F32) + (e1 * inv) * o1_ref[:, hs].astype(F32)
                 + (e2 * inv) * o2_ref[:, hs].astype(F32))
        parts.append(mixed.astype(BF16))
    attn = jnp.concatenate(parts, axis=1)
    rec = rec_ref[...]
    for j in range(nct):
        cs = slice(j * COL_TILE, (j + 1) * COL_TILE)
        ya = jnp.dot(attn, wa_ref[:, cs], preferred_element_type=F32)
        yr = jnp.dot(rec, wr_ref[:, cs], preferred_element_type=F32)
        gate_a = jax.nn.sigmoid(ga_refs[j][...].astype(F32) + bg_ref[0:1, cs])
        gate_r = jax.nn.sigmoid(gr_refs[j][...].astype(F32) + bg_ref[1:2, cs])
        out_ref[:, cs] = (gate_a * ya + gate_r * yr).astype(out_ref.dtype)


def _merge(os_, lses, rec, proj, b_gate, wa, wr):
    T = rec.shape[0]
    tm = 256
    nct = D_MODEL // COL_TILE
    row = lambda m: (m, 0)
    const = lambda m: (0, 0)
    gate_specs = [pl.BlockSpec((tm, COL_TILE), lambda m, c=(GATE_OFF + k * D_MODEL) // COL_TILE + j: (m, c))
                  for k in range(2) for j in range(nct)]
    o_spec = pl.BlockSpec((tm, ATTN_OUT_WIDTH), row)
    l_spec = pl.BlockSpec((tm, LANES), row)
    resident = lambda shape: pl.BlockSpec(shape, const, pipeline_mode=pl.Buffered(1))
    return pl.pallas_call(
        _merge_kernel,
        grid=(T // tm,),
        in_specs=[o_spec, o_spec, o_spec, l_spec, l_spec, l_spec, pl.BlockSpec((tm, LRU_WIDTH), row)]
                 + gate_specs
                 + [resident((2, D_MODEL)), resident((ATTN_OUT_WIDTH, D_MODEL)), resident((LRU_WIDTH, D_MODEL))],
        out_specs=pl.BlockSpec((tm, D_MODEL), row),
        out_shape=jax.ShapeDtypeStruct((T, D_MODEL), BF16),
        compiler_params=_params("parallel"),
        name="merge",
    )(*os_, *lses, rec, *([proj] * (2 * nct)), b_gate, wa, wr)


def _layer_norm(z, g, b):
    mu = jnp.mean(z, axis=-1, keepdims=True)
    zc = z - mu
    var = jnp.mean(zc * zc, axis=-1, keepdims=True)
    return zc * lax.rsqrt(var + LN_EPS) * g + b


_HI_MASK = 0xFFFF0000


def _pack_bf16_pairs(y):
    bits = lax.bitcast_convert_type(y.astype(BF16).astype(F32), jnp.uint32)
    c = y.shape[1] // 2
    return (bits[:, :c] >> 16) | (bits[:, c:] & jnp.uint32(_HI_MASK))


def _unpack_bf16_pairs(p):
    lo = lax.bitcast_convert_type(p << 16, F32)
    hi = lax.bitcast_convert_type(p & jnp.uint32(_HI_MASK), F32)
    return lo, hi


def _out_matmul_kernel(mg_ref, w_ref, o_ref):
    o_ref[...] = jnp.dot(mg_ref[...], w_ref[...], preferred_element_type=F32).astype(o_ref.dtype)


def _out_matmul(merged, w_out):
    T = merged.shape[0]
    tm, tn = 1024, 1024
    return pl.pallas_call(
        _out_matmul_kernel,
        grid=(T // tm, D_MODEL // tn),
        in_specs=[pl.BlockSpec((tm, D_MODEL), lambda m, n: (m, 0)),
                  pl.BlockSpec((D_MODEL, tn), lambda m, n: (0, n))],
        out_specs=pl.BlockSpec((tm, tn), lambda m, n: (m, n)),
        out_shape=jax.ShapeDtypeStruct((T, D_MODEL), BF16),
        compiler_params=_params("parallel", "arbitrary"),
        name="out_matmul",
    )(merged, w_out)


def _ln1_kernel(mix_ref, x_ref, g_ref, b_ref, wr_ref, br_ref, h_ref, hp_ref, lg_ref):
    z = DEEPNORM_ALPHA * x_ref[...] + mix_ref[...].astype(F32)
    y = _layer_norm(z, g_ref[...], b_ref[...])
    h_ref[...] = y
    hp_ref[...] = _pack_bf16_pairs(y)
    lg_ref[...] = jnp.dot(y.astype(BF16), wr_ref[...], preferred_element_type=F32) + br_ref[...]


def _ln1_router(mix, x2, g, b, wr, br):
    T = mix.shape[0]
    tm = 256
    row = lambda m: (m, 0)
    const = lambda m: (0, 0)
    return pl.pallas_call(
        _ln1_kernel,
        grid=(T // tm,),
        in_specs=[pl.BlockSpec((tm, D_MODEL), row), pl.BlockSpec((tm, D_MODEL), row),
                  pl.BlockSpec((1, D_MODEL), const), pl.BlockSpec((1, D_MODEL), const),
                  pl.BlockSpec((D_MODEL, LANES), const), pl.BlockSpec((1, LANES), const)],
        out_specs=[pl.BlockSpec((tm, D_MODEL), row), pl.BlockSpec((tm, D_MODEL // 2), row),
                   pl.BlockSpec((tm, LANES), row)],
        out_shape=[jax.ShapeDtypeStruct((T, D_MODEL), F32), jax.ShapeDtypeStruct((T, D_MODEL // 2), jnp.uint32),
                   jax.ShapeDtypeStruct((T, LANES), F32)],
        compiler_params=_params("parallel"),
        name="ln1_router",
    )(mix, x2, g.reshape(1, -1), b.reshape(1, -1), wr, br)


MOE_TM = 256


def _moe_up_kernel(be_ref, nv_ref, tok_ref, h_hbm, wg_ref, wu_ref, hid_ref, xbuf, sem):
    b = pl.program_id(0)
    nv = nv_ref[0]

    def row_copy(tok, slot, r):
        return pltpu.make_async_copy(h_hbm.at[pl.ds(tok, 1), :], xbuf.at[slot, pl.ds(r, 1), :], sem.at[slot])

    def start_gather_loop(blk, slot):
        def body(r, c):
            row_copy(tok_ref[blk * MOE_TM + r], slot, r).start()
            return c
        lax.fori_loop(0, MOE_TM, body, 0, unroll=8)

    def start_gather_inline(blk, slot):
        for r in range(MOE_TM):
            row_copy(tok_ref[blk * MOE_TM + r], slot, r).start()

    def wait_gather(slot):
        def body(r, c):
            row_copy(0, slot, r).wait()
            return c
        lax.fori_loop(0, MOE_TM, body, 0, unroll=8)

    def compute(slot):
        lo, hi = _unpack_bf16_pairs(xbuf[slot])
        x = jnp.concatenate([lo.astype(BF16), hi.astype(BF16)], axis=1)
        gate = jnp.dot(x, wg_ref[0].astype(BF16), preferred_element_type=F32)
        up = jnp.dot(x, wu_ref[0].astype(BF16), preferred_element_type=F32)
        hid_ref[...] = (gate * jax.nn.sigmoid(gate) * up).astype(hid_ref.dtype)

    @pl.when(b == 0)
    def _():
        start_gather_loop(0, 0)

    @pl.when(b + 1 < nv)
    def _():
        wait_gather(b % 2)
        start_gather_inline(b + 1, (b + 1) % 2)
        compute(b % 2)

    @pl.when(b + 1 == nv)
    def _():
        wait_gather(b % 2)
        compute(b % 2)

    @pl.when(b >= nv)
    def _():
        hid_ref[...] = jnp.zeros(hid_ref.shape, hid_ref.dtype)


def _moe_down_kernel(be_ref, nv_ref, hid_ref, rw_ref, wd_ref, out_ref):
    b = pl.program_id(0)
    nv = nv_ref[0]

    @pl.when(b < nv)
    def _():
        y = jnp.dot(hid_ref[...], wd_ref[0].astype(BF16), preferred_element_type=F32)
        out_ref[...] = _pack_bf16_pairs(y * rw_ref[...])

    @pl.when(b >= nv)
    def _():
        out_ref[...] = jnp.zeros(out_ref.shape, out_ref.dtype)


def _moe_experts(h1p, row_tok, row_w, block_e, n_valid, wg, wu, wd):
    rows = row_tok.shape[0]
    nb = rows // MOE_TM
    hid = pl.pallas_call(
        _moe_up_kernel,
        grid_spec=pltpu.PrefetchScalarGridSpec(
            num_scalar_prefetch=3,
            grid=(nb,),
            in_specs=[pl.BlockSpec(memory_space=pl.ANY),
                      pl.BlockSpec((1, D_MODEL, D_EXPERT), lambda b, be, nv, tok: (be[b], 0, 0)),
                      pl.BlockSpec((1, D_MODEL, D_EXPERT), lambda b, be, nv, tok: (be[b], 0, 0))],
            out_specs=pl.BlockSpec((MOE_TM, D_EXPERT), lambda b, be, nv, tok: (b, 0)),
            scratch_shapes=[pltpu.VMEM((2, MOE_TM, D_MODEL // 2), jnp.uint32), pltpu.SemaphoreType.DMA((2,))]),
        out_shape=jax.ShapeDtypeStruct((rows, D_EXPERT), BF16),
        compiler_params=_params("arbitrary"),
        name="moe_up",
    )(block_e, n_valid, row_tok, h1p, wg, wu)
    return pl.pallas_call(
        _moe_down_kernel,
        grid_spec=pltpu.PrefetchScalarGridSpec(
            num_scalar_prefetch=2,
            grid=(nb,),
            in_specs=[pl.BlockSpec((MOE_TM, D_EXPERT), lambda b, be, nv: (b, 0)),
                      pl.BlockSpec((MOE_TM, 1), lambda b, be, nv: (b, 0)),
                      pl.BlockSpec((1, D_EXPERT, D_MODEL), lambda b, be, nv: (be[b], 0, 0))],
            out_specs=pl.BlockSpec((MOE_TM, D_MODEL // 2), lambda b, be, nv: (b, 0))),
        out_shape=jax.ShapeDtypeStruct((rows, D_MODEL // 2), jnp.uint32),
        compiler_params=_params("arbitrary"),
        name="moe_down",
    )(block_e, n_valid, hid, row_w.reshape(rows, 1), wd)


FIN_TM = 256


def _final_kernel(pos_ref, h_ref, y_hbm, g_ref, b_ref, out_ref, ybuf, sem):
    i = pl.program_id(0)
    n = pl.num_programs(0)

    def row_copy(src_row, slot, r):
        return pltpu.make_async_copy(y_hbm.at[pl.ds(src_row, 1), :], ybuf.at[slot, pl.ds(r, 1), :], sem.at[slot])

    def start_gather(tile, slot):
        def body(r, c):
            for k in range(TOP_K):
                row_copy(pos_ref[(tile * FIN_TM + r) * TOP_K + k], slot, k * FIN_TM + r).start()
            return c
        lax.fori_loop(0, FIN_TM, body, 0, unroll=4)

    def wait_gather(slot):
        def body(r, c):
            row_copy(0, slot, r).wait()
            return c
        lax.fori_loop(0, TOP_K * FIN_TM, body, 0, unroll=8)

    @pl.when(i == 0)
    def _():
        start_gather(0, 0)

    @pl.when(i + 1 < n)
    def _():
        start_gather(i + 1, (i + 1) % 2)

    slot = i % 2
    wait_gather(slot)
    ff_lo, ff_hi = _unpack_bf16_pairs(ybuf[slot, 0:FIN_TM, :])
    for k in range(1, TOP_K):
        lo, hi = _unpack_bf16_pairs(ybuf[slot, k * FIN_TM:(k + 1) * FIN_TM, :])
        ff_lo, ff_hi = ff_lo + lo, ff_hi + hi
    z = DEEPNORM_ALPHA * h_ref[...] + jnp.concatenate([ff_lo, ff_hi], axis=1)
    out_ref[...] = _layer_norm(z, g_ref[...], b_ref[...])


def _moe_combine_ln2(h1, moe_rows, pos, g, b):
    T = h1.shape[0]
    const = lambda i, pos: (0, 0)
    grid_spec = pltpu.PrefetchScalarGridSpec(
        num_scalar_prefetch=1,
        grid=(T // FIN_TM,),
        in_specs=[pl.BlockSpec((FIN_TM, D_MODEL), lambda i, pos: (i, 0)),
                  pl.BlockSpec(memory_space=pl.ANY),
                  pl.BlockSpec((1, D_MODEL), const), pl.BlockSpec((1, D_MODEL), const)],
        out_specs=pl.BlockSpec((FIN_TM, D_MODEL), lambda i, pos: (i, 0)),
        scratch_shapes=[pltpu.VMEM((2, TOP_K * FIN_TM, D_MODEL // 2), jnp.uint32), pltpu.SemaphoreType.DMA((2,))],
    )
    return pl.pallas_call(
        _final_kernel,
        grid_spec=grid_spec,
        out_shape=jax.ShapeDtypeStruct((T, D_MODEL), F32),
        compiler_params=_params("arbitrary"),
        name="moe_combine_ln2",
    )(pos, h1, moe_rows, g.reshape(1, -1), b.reshape(1, -1))


def _routing(logits):
    T = logits.shape[0]
    logits_g = logits[:, :N_GROUPS]
    logits_e = logits[:, N_GROUPS:N_GROUPS + N_EXPERTS].reshape(T, N_GROUPS, EXPERTS_PER_GROUP)
    p_g = jax.nn.softmax(logits_g, axis=-1)
    g_idx = jnp.argmax(logits_g, axis=-1)
    p_grp = jnp.take_along_axis(p_g, g_idx[:, None], axis=-1)
    le = jnp.take_along_axis(logits_e, g_idx[:, None, None], axis=1)[:, 0]
    top_v, top_i = lax.top_k(le, TOP_K)
    weight = p_grp * jax.nn.softmax(top_v, axis=-1)
    expert_id = g_idx[:, None].astype(jnp.int32) * EXPERTS_PER_GROUP + top_i.astype(jnp.int32)

    A = T * TOP_K
    flat_e = expert_id.reshape(-1)
    flat_w = weight.reshape(-1)
    experts = jnp.arange(N_EXPERTS, dtype=jnp.int32)
    counts = jnp.sum((flat_e[:, None] == experts[None, :]).astype(jnp.int32), axis=0)
    starts = jnp.cumsum(counts) - counts
    padded = (counts + MOE_TM - 1) // MOE_TM * MOE_TM
    pends = jnp.cumsum(padded)
    pstarts = pends - padded
    order = jnp.argsort(flat_e, stable=True).astype(jnp.int32)
    inv = jnp.argsort(order).astype(jnp.int32)
    pos = (inv + (pstarts - starts)[flat_e]).astype(jnp.int32)
    nb = A // MOE_TM + N_EXPERTS
    n_valid = (pends[-1] // MOE_TM).astype(jnp.int32)
    blk = jnp.arange(nb, dtype=jnp.int32)
    blk_start = jnp.minimum(blk, n_valid - 1) * MOE_TM
    block_e = jnp.minimum(jnp.sum((blk_start[:, None] >= pends[None, :]).astype(jnp.int32), axis=1), N_EXPERTS - 1)
    i_in = (blk * MOE_TM - pstarts[block_e])[:, None] + jnp.arange(MOE_TM, dtype=jnp.int32)[None, :]
    valid = i_in < counts[block_e][:, None]
    a_r = order[jnp.clip(starts[block_e][:, None] + i_in, 0, A - 1)]
    row_tok = jnp.where(valid, a_r // TOP_K, 0).astype(jnp.int32).reshape(-1)
    row_w = jnp.where(valid, flat_w[a_r], 0.0).astype(F32).reshape(-1)
    return row_tok, row_w, block_e, n_valid.reshape(1), pos


def kernel(x, positions, w_in, b_gate, conv_w, conv_b, w_rg_a, b_rg_a, w_rg_x, b_rg_x, lru_lambda, w_attn_proj, w_rec_proj, w_out, ln1_g, ln1_b, w_router_group, b_router_group, w_router_expert, b_router_expert, w_gate, w_up, w_down, ln2_g, ln2_b):
    B, S, D = x.shape
    T = B * S
    h = x.reshape(T, D)
    c, sa, sb = _rope_tables(positions)
    for layer in range(DEPTH):
        proj = _in_proj(h.astype(BF16), w_in[layer], c, sa, sb)
        os_, lses = [], []
        for g, (_, dilation) in enumerate(ATTN_GROUPS):
            o, lse = _attention_group(proj, B, S, g, dilation)
            os_.append(o)
            lses.append(lse)
        rec = _rglru(proj, B, S, conv_w[layer], conv_b[layer], w_rg_a[layer].astype(BF16), b_rg_a[layer],
                     w_rg_x[layer].astype(BF16), b_rg_x[layer], lru_lambda[layer])
        merged = _merge(os_, lses, rec, proj, b_gate[layer], w_attn_proj[layer].astype(BF16),
                        w_rec_proj[layer].astype(BF16))
        w_router = jnp.concatenate([w_router_group[layer], w_router_expert[layer]], axis=1)
        w_router = jnp.pad(w_router, ((0, 0), (0, LANES - w_router.shape[1])))
        b_router = jnp.concatenate([b_router_group[layer], b_router_expert[layer]])
        b_router = jnp.pad(b_router, (0, LANES - b_router.shape[0])).reshape(1, LANES)
        mix = _out_matmul(merged, w_out[layer].astype(BF16))
        h1, h1p, logits = _ln1_router(mix, h, ln1_g[layer], ln1_b[layer], w_router.astype(BF16), b_router)
        row_tok, row_w, block_e, n_valid, pos = _routing(logits)
        moe_rows = _moe_experts(h1p, row_tok, row_w, block_e, n_valid, w_gate[layer], w_up[layer], w_down[layer])
        h = _moe_combine_ln2(h1, moe_rows, pos, ln2_g[layer], ln2_b[layer])
    return h.reshape(B, S, D)
```

```python
import functools

import jax
import jax.numpy as jnp
from jax import lax
from jax.experimental import pallas as pl
from jax.experimental.pallas import tpu as pltpu

D_MODEL = 4096
HEAD_DIM = 128
ATTN_GROUPS = ((128, 1), (512, 4), (2048, 16))
HEADS_PER_GROUP = 8
N_ATTN_HEADS = HEADS_PER_GROUP * len(ATTN_GROUPS)
ATTN_WIDTH = N_ATTN_HEADS * HEAD_DIM
ATTN_OUT_WIDTH = HEADS_PER_GROUP * HEAD_DIM
ROT_DIM = HEAD_DIM // 4
ROPE_THETA = 500000.0
ATTN_SPAN = 128
LRU_WIDTH = 2048
LRU_BLOCKS = 8
LRU_BLOCK_W = LRU_WIDTH // LRU_BLOCKS
CONV_WIDTH = 4
RG_C = 8.0
N_GROUPS = 8
EXPERTS_PER_GROUP = 8
N_EXPERTS = N_GROUPS * EXPERTS_PER_GROUP
TOP_K = 2
D_EXPERT = 512
LN_EPS = 1e-5
DEPTH = 1
DEEPNORM_ALPHA = (2 * DEPTH) ** 0.25
K_OFF = ATTN_WIDTH
V_OFF = 2 * ATTN_WIDTH
RX_OFF = 3 * ATTN_WIDTH
RG_OFF = RX_OFF + LRU_WIDTH
GATE_OFF = RG_OFF + LRU_WIDTH
IN_COLS = GATE_OFF + 2 * D_MODEL

LANES = 128
COL_TILE = 1024
PROJ_TM = 1024
PROJ_TN = 512
VMEM_LIMIT = 56 * 1024 * 1024
VMEM_LIMIT_MAX = 60 * 1024 * 1024

F32 = jnp.float32
BF16 = jnp.bfloat16


def _params(*sem, vmem=VMEM_LIMIT):
    return pltpu.CompilerParams(dimension_semantics=sem, vmem_limit_bytes=vmem)


def _rope_table_kernel(pos_ref, invf_ref, c_ref, sa_ref, sb_ref):
    half = ROT_DIM // 2
    ang = pos_ref[...].astype(F32) * invf_ref[...]
    lane = lax.broadcasted_iota(jnp.int32, ang.shape, 1)
    cos = jnp.cos(ang)
    sin = jnp.sin(ang)
    c_ref[...] = jnp.where(lane < ROT_DIM, cos, 1.0)
    sa_ref[...] = jnp.where(lane < half, -sin, 0.0)
    sb_ref[...] = jnp.where((lane >= half) & (lane < ROT_DIM), sin, 0.0)


def _rope_tables(positions):
    T = positions.size
    tm = 2048
    half = ROT_DIM // 2
    inv_freq = jnp.power(jnp.float32(ROPE_THETA), -jnp.arange(half, dtype=F32) * 2.0 / ROT_DIM)
    invf = jnp.zeros((1, LANES), F32).at[0, :ROT_DIM].set(jnp.concatenate([inv_freq, inv_freq]))
    tab = jax.ShapeDtypeStruct((T, LANES), F32)
    return pl.pallas_call(
        _rope_table_kernel,
        grid=(T // tm,),
        in_specs=[pl.BlockSpec((tm, 1), lambda i: (i, 0)), pl.BlockSpec((1, LANES), lambda i: (0, 0))],
        out_specs=[pl.BlockSpec((tm, LANES), lambda i: (i, 0))] * 3,
        out_shape=[tab, tab, tab],
        compiler_params=_params("arbitrary"),
        name="rope_tables",
    )(positions.reshape(T, 1), invf)


def _in_proj_kernel(x_ref, w_ref, c_ref, sa_ref, sb_ref, o_ref, wb_ref, slab_ref):
    n = pl.program_id(0)
    m = pl.program_id(1)
    tiles_per_sect = ATTN_WIDTH // PROJ_TN
    tiles_per_group = ATTN_OUT_WIDTH // PROJ_TN
    heads = PROJ_TN // HEAD_DIM

    @pl.when(m == 0)
    def _():
        wb_ref[...] = w_ref[...].astype(BF16)

    acc = jnp.dot(x_ref[...], wb_ref[...], preferred_element_type=F32)
    is_attn = n < 3 * tiles_per_sect

    @pl.when(jnp.logical_not(is_attn))
    def _():
        o_ref[...] = acc.astype(o_ref.dtype)

    @pl.when(is_attn)
    def _():
        rot = n < 2 * tiles_per_sect

        @pl.when(rot)
        def _():
            scale = jnp.where(n < tiles_per_sect, HEAD_DIM ** -0.5, 1.0).astype(F32)
            c, sa, sb = c_ref[...], sa_ref[...], sb_ref[...]
            for h in range(heads):
                t = acc[:, h * HEAD_DIM:(h + 1) * HEAD_DIM]
                u = pltpu.bitcast(t.astype(BF16), jnp.uint32)
                ta = pltpu.bitcast(pltpu.roll(u, HEAD_DIM - ROT_DIM // 2, 1), BF16).astype(F32)
                tb = pltpu.bitcast(pltpu.roll(u, ROT_DIM // 2, 1), BF16).astype(F32)
                slab_ref[h] = (t * c + ta * sa + tb * sb) * scale

        @pl.when(jnp.logical_not(rot))
        def _():
            for h in range(heads):
                slab_ref[h] = acc[:, h * HEAD_DIM:(h + 1) * HEAD_DIM]

        group = (n % tiles_per_sect) // tiles_per_group
        for g, (_, d) in enumerate(ATTN_GROUPS):
            @pl.when(group == g)
            def _(d=d):
                per = PROJ_TM // d
                for h in range(heads):
                    hs = slice(h * HEAD_DIM, (h + 1) * HEAD_DIM)
                    if d == 1:
                        o_ref[:, hs] = slab_ref[h].astype(o_ref.dtype)
                    else:
                        for res in range(d):
                            rows = slab_ref[h, pl.ds(res, per, stride=d), :]
                            o_ref[res * per:(res + 1) * per, hs] = rows.astype(o_ref.dtype)


def _in_proj(xb, w, c, sa, sb):
    T = xb.shape[0]
    tab_spec = pl.BlockSpec((PROJ_TM, LANES), lambda n, m: (m, 0))
    return pl.pallas_call(
        _in_proj_kernel,
        grid=(IN_COLS // PROJ_TN, T // PROJ_TM),
        in_specs=[pl.BlockSpec((PROJ_TM, D_MODEL), lambda n, m: (m, 0)),
                  pl.BlockSpec((D_MODEL, PROJ_TN), lambda n, m: (0, n)),
                  tab_spec, tab_spec, tab_spec],
        out_specs=pl.BlockSpec((PROJ_TM, PROJ_TN), lambda n, m: (m, n)),
        out_shape=jax.ShapeDtypeStruct((T, IN_COLS), BF16),
        scratch_shapes=[pltpu.VMEM((D_MODEL, PROJ_TN), BF16),
                        pltpu.VMEM((PROJ_TN // HEAD_DIM, PROJ_TM, LANES), F32)],
        compiler_params=_params("arbitrary", "arbitrary"),
        name="in_proj",
    )(xb, w, c, sa, sb)


def _attn_kernel(*refs, npc):
    q_refs = refs[:npc]
    kp_refs, kc_refs = refs[npc:2 * npc], refs[2 * npc:3 * npc]
    vp_refs, vc_refs = refs[3 * npc:4 * npc], refs[4 * npc:5 * npc]
    o_ref, lse_ref = refs[5 * npc:]
    n = pl.program_id(2)
    nh, blk = HEADS_PER_GROUP, ATTN_SPAN

    def head(pieces, h):
        hs = slice(h * HEAD_DIM, (h + 1) * HEAD_DIM)
        parts = [r[:, hs] for r in pieces]
        return parts[0] if len(parts) == 1 else jnp.concatenate(parts, axis=0)

    dn = (((1,), (1,)), ((), ()))
    qs = [head(q_refs, h) for h in range(nh)]
    sp = jnp.concatenate([lax.dot_general(qs[h], head(kp_refs, h), dn, preferred_element_type=F32)
                          for h in range(nh)], axis=0)
    sc = jnp.concatenate([lax.dot_general(qs[h], head(kc_refs, h), dn, preferred_element_type=F32)
                          for h in range(nh)], axis=0)
    row = lax.broadcasted_iota(jnp.int32, sp.shape, 0) & (blk - 1)
    col = lax.broadcasted_iota(jnp.int32, sp.shape, 1)
    sp = jnp.where(jnp.logical_and(col >= row, n > 0), sp, -jnp.inf)
    sc = jnp.where(col <= row, sc, -jnp.inf)
    m = jnp.maximum(jnp.max(sp, axis=1, keepdims=True), jnp.max(sc, axis=1, keepdims=True))
    pp = jnp.exp(sp - m)
    pc = jnp.exp(sc - m)
    l = jnp.sum(pp, axis=1, keepdims=True) + jnp.sum(pc, axis=1, keepdims=True)
    inv = 1.0 / l
    lse = m + jnp.log(l)
    pp = pp.astype(BF16)
    pc = pc.astype(BF16)
    lane = lax.broadcasted_iota(jnp.int32, (blk, LANES), 1)
    lse_all = jnp.zeros((blk, LANES), F32)
    outs = []
    for h in range(nh):
        rs = slice(h * blk, (h + 1) * blk)
        o = (jnp.dot(pp[rs], head(vp_refs, h), preferred_element_type=F32)
             + jnp.dot(pc[rs], head(vc_refs, h), preferred_element_type=F32))
        outs.append((o * inv[rs]).astype(o_ref.dtype))
        lse_all = jnp.where(lane == h, lse[rs], lse_all)
    o_ref[0] = jnp.concatenate(outs, axis=1)
    lse_ref[0] = lse_all


def _attention_group(proj, batch, seq, g, dilation):
    L = seq // dilation
    per = PROJ_TM // dilation
    pr = min(per, ATTN_SPAN)
    npc = ATTN_SPAN // pr
    nblk = L // ATTN_SPAN
    tiles_per_batch = seq // PROJ_TM
    kcol, vcol = K_OFF // COL_TILE, V_OFF // COL_TILE

    def piece_block(b, res, n, p):
        pos = n * ATTN_SPAN + p * pr
        row = (b * tiles_per_batch + pos // per) * PROJ_TM + res * per + pos % per
        return row // pr

    def specs(col, back):
        return [pl.BlockSpec((pr, COL_TILE),
                             lambda b, r, n, p=p: (piece_block(b, r, jnp.maximum(n - back, 0), p), col + g))
                for p in range(npc)]

    in_specs = specs(0, 0) + specs(kcol, 1) + specs(kcol, 0) + specs(vcol, 1) + specs(vcol, 0)
    o, lse = pl.pallas_call(
        functools.partial(_attn_kernel, npc=npc),
        grid=(batch, dilation, nblk),
        in_specs=in_specs,
        out_specs=[pl.BlockSpec((1, ATTN_SPAN, ATTN_OUT_WIDTH), lambda b, r, n: (b, n, r)),
                   pl.BlockSpec((1, ATTN_SPAN, LANES), lambda b, r, n: (b, n, r))],
        out_shape=[jax.ShapeDtypeStruct((batch, L, dilation * ATTN_OUT_WIDTH), BF16),
                   jax.ShapeDtypeStruct((batch, L, dilation * LANES), F32)],
        compiler_params=_params("parallel", "parallel", "arbitrary"),
        name=f"attn_d{dilation}",
    )(*([proj] * (5 * npc)))
    return o.reshape(batch * seq, ATTN_OUT_WIDTH), lse.reshape(batch * seq, LANES)


_TAIL = 8


def _rglru_kernel(rx_ref, rg_ref, cw_ref, cb_ref, wa_ref, ba_ref, wx_ref, bx_ref, lam_ref, o_ref,
                  xbuf, a_buf, h_buf, carry):
    tt = rx_ref.shape[0]
    t_idx = pl.program_id(2)

    @pl.when(t_idx == 0)
    def _():
        xbuf[0:_TAIL, :] = jnp.zeros((_TAIL, xbuf.shape[1]), F32)
        carry[...] = jnp.zeros(carry.shape, F32)

    x = rx_ref[...].astype(F32)
    xbuf[_TAIL:_TAIL + tt, :] = x
    xr = cb_ref[...] + cw_ref[0:1, :] * xbuf[_TAIL - 3:_TAIL - 3 + tt, :]
    for j in range(1, CONV_WIDTH):
        xr = xr + cw_ref[j:j + 1, :] * xbuf[_TAIL - 3 + j:_TAIL - 3 + j + tt, :]
    xbuf[0:_TAIL, :] = x[tt - _TAIL:tt, :]

    softplus_neg_lam = jnp.log1p(jnp.exp(-lam_ref[...]))
    for blk in range(xr.shape[1] // LRU_BLOCK_W):
        cs = slice(blk * LRU_BLOCK_W, (blk + 1) * LRU_BLOCK_W)
        xb = xr[:, cs]
        xb16 = xb.astype(BF16)
        r = jax.nn.sigmoid(jnp.dot(xb16, wa_ref[blk], preferred_element_type=F32) + ba_ref[:, cs])
        i = jax.nn.sigmoid(jnp.dot(xb16, wx_ref[blk], preferred_element_type=F32) + bx_ref[:, cs])
        log_a = (-RG_C) * r * softplus_neg_lam[:, cs]
        a = jnp.exp(log_a)
        a_buf[:, cs] = a
        h_buf[:, cs] = jnp.sqrt(1.0 - a * a) * (i * xb)

    def step(t, h):
        h = a_buf[pl.ds(t, 1), :] * h + h_buf[pl.ds(t, 1), :]
        h_buf[pl.ds(t, 1), :] = h
        return h

    carry[...] = lax.fori_loop(0, tt, step, carry[...], unroll=8)
    o_ref[...] = (h_buf[...] * jax.nn.gelu(rg_ref[...].astype(F32))).astype(o_ref.dtype)


def _rglru(proj, batch, seq, conv_w, conv_b, wa, ba, wx, bx, lam):
    tt = 512
    nt = seq // tt
    hw = COL_TILE
    assert hw % LRU_BLOCK_W == 0 and RX_OFF % hw == 0 and RG_OFF % hw == 0
    gb = hw // LRU_BLOCK_W
    row = lambda off: pl.BlockSpec((tt, hw), lambda b, c, t: (b * nt + t, off // hw + c))
    vec = pl.BlockSpec((1, hw), lambda b, c, t: (0, c))
    wspec = pl.BlockSpec((gb, LRU_BLOCK_W, LRU_BLOCK_W), lambda b, c, t: (c, 0, 0))
    return pl.pallas_call(
        _rglru_kernel,
        grid=(batch, LRU_WIDTH // hw, nt),
        in_specs=[row(RX_OFF), row(RG_OFF), pl.BlockSpec((CONV_WIDTH, hw), lambda b, c, t: (0, c)), vec,
                  wspec, vec, wspec, vec, vec],
        out_specs=pl.BlockSpec((tt, hw), lambda b, c, t: (b * nt + t, c)),
        out_shape=jax.ShapeDtypeStruct((batch * seq, LRU_WIDTH), BF16),
        scratch_shapes=[pltpu.VMEM((tt + _TAIL, hw), F32), pltpu.VMEM((tt, hw), F32),
                        pltpu.VMEM((tt, hw), F32), pltpu.VMEM((1, hw), F32)],
        compiler_params=_params("parallel", "parallel", "arbitrary"),
        name="rg_lru",
    )(proj, proj, conv_w, conv_b.reshape(1, -1), wa, ba.reshape(1, -1), wx, bx.reshape(1, -1), lam.reshape(1, -1))


def _merge_kernel(*refs):
    o0_ref, o1_ref, o2_ref, l0_ref, l1_ref, l2_ref, rec_ref = refs[:7]
    nct = D_MODEL // COL_TILE
    ga_refs = refs[7:7 + nct]
    gr_refs = refs[7 + nct:7 + 2 * nct]
    bg_ref, wa_ref, wr_ref, out_ref = refs[7 + 2 * nct:]
    l0, l1, l2 = l0_ref[...], l1_ref[...], l2_ref[...]
    parts = []
    for h in range(HEADS_PER_GROUP):
        hs = slice(h * HEAD_DIM, (h + 1) * HEAD_DIM)
        a0, a1, a2 = l0[:, h:h + 1], l1[:, h:h + 1], l2[:, h:h + 1]
        m = jnp.maximum(jnp.maximum(a0, a1), a2)
        e0, e1, e2 = jnp.exp(a0 - m), jnp.exp(a1 - m), jnp.exp(a2 - m)
        inv = 1.0 / (e0 + e1 + e2)
        mixed = ((e0 * inv) * o0_ref[:, hs].astype(F32) + (e1 * inv) * o1_ref[:, hs].astype(F32)
                 + (e2 * inv) * o2_ref[:, hs].astype(F32))
        parts.append(mixed.astype(BF16))
    attn = jnp.concatenate(parts, axis=1)
    rec = rec_ref[...]
    for j in range(nct):
        cs = slice(j * COL_TILE, (j + 1) * COL_TILE)
        ya = jnp.dot(attn, wa_ref[:, cs], preferred_element_type=F32)
        yr = jnp.dot(rec, wr_ref[:, cs], preferred_element_type=F32)
        gate_a = jax.nn.sigmoid(ga_refs[j][...].astype(F32) + bg_ref[0:1, cs])
        gate_r = jax.nn.sigmoid(gr_refs[j][...].astype(F32) + bg_ref[1:2, cs])
        out_ref[:, cs] = (gate_a * ya + gate_r * yr).astype(out_ref.dtype)


def _merge(os_, lses, rec, proj, b_gate, wa, wr):
    T = rec.shape[0]
    tm = 256
    nct = D_MODEL // COL_TILE
    row = lambda m: (m, 0)
    const = lambda m: (0, 0)
    gate_specs = [pl.BlockSpec((tm, COL_TILE), lambda m, c=(GATE_OFF + k * D_MODEL) // COL_TILE + j: (m, c))
                  for k in range(2) for j in range(nct)]
    o_spec = pl.BlockSpec((tm, ATTN_OUT_WIDTH), row)
    l_spec = pl.BlockSpec((tm, LANES), row)
    resident = lambda shape: pl.BlockSpec(shape, const, pipeline_mode=pl.Buffered(1))
    return pl.pallas_call(
        _merge_kernel,
        grid=(T // tm,),
        in_specs=[o_spec, o_spec, o_spec, l_spec, l_spec, l_spec, pl.BlockSpec((tm, LRU_WIDTH), row)]
                 + gate_specs
                 + [resident((2, D_MODEL)), resident((ATTN_OUT_WIDTH, D_MODEL)), resident((LRU_WIDTH, D_MODEL))],
        out_specs=pl.BlockSpec((tm, D_MODEL), row),
        out_shape=jax.ShapeDtypeStruct((T, D_MODEL), BF16),
        compiler_params=_params("parallel"),
        name="merge",
    )(*os_, *lses, rec, *([proj] * (2 * nct)), b_gate, wa, wr)


def _layer_norm(z, g, b):
    mu = jnp.mean(z, axis=-1, keepdims=True)
    zc = z - mu
    var = jnp.mean(zc * zc, axis=-1, keepdims=True)
    return zc * lax.rsqrt(var + LN_EPS) * g + b


_HI_MASK = 0xFFFF0000


def _pack_bf16_pairs(y):
    bits = lax.bitcast_convert_type(y.astype(BF16).astype(F32), jnp.uint32)
    c = y.shape[1] // 2
    return (bits[:, :c] >> 16) | (bits[:, c:] & jnp.uint32(_HI_MASK))


def _unpack_bf16_pairs(p):
    lo = lax.bitcast_convert_type(p << 16, F32)
    hi = lax.bitcast_convert_type(p & jnp.uint32(_HI_MASK), F32)
    return lo, hi


def _out_matmul_kernel(mg_ref, w_ref, o_ref):
    o_ref[...] = jnp.dot(mg_ref[...], w_ref[...], preferred_element_type=F32).astype(o_ref.dtype)


def _out_matmul(merged, w_out):
    T = merged.shape[0]
    tm, tn = 1024, 1024
    return pl.pallas_call(
        _out_matmul_kernel,
        grid=(T // tm, D_MODEL // tn),
        in_specs=[pl.BlockSpec((tm, D_MODEL), lambda m, n: (m, 0)),
                  pl.BlockSpec((D_MODEL, tn), lambda m, n: (0, n))],
        out_specs=pl.BlockSpec((tm, tn), lambda m, n: (m, n)),
        out_shape=jax.ShapeDtypeStruct((T, D_MODEL), BF16),
        compiler_params=_params("parallel", "arbitrary"),
        name="out_matmul",
    )(merged, w_out)


def _ln1_kernel(mix_ref, x_ref, g_ref, b_ref, wr_ref, br_ref, h_ref, hp_ref, lg_ref):
    z = DEEPNORM_ALPHA * x_ref[...] + mix_ref[...].astype(F32)
    y = _layer_norm(z, g_ref[...], b_ref[...])
    h_ref[...] = y
    hp_ref[...] = _pack_bf16_pairs(y)
    lg_ref[...] = jnp.dot(y.astype(BF16), wr_ref[...], preferred_element_type=F32) + br_ref[...]


def _ln1_router(mix, x2, g, b, wr, br):
    T = mix.shape[0]
    tm = 256
    row = lambda m: (m, 0)
    const = lambda m: (0, 0)
    return pl.pallas_call(
        _ln1_kernel,
        grid=(T // tm,),
        in_specs=[pl.BlockSpec((tm, D_MODEL), row), pl.BlockSpec((tm, D_MODEL), row),
                  pl.BlockSpec((1, D_MODEL), const), pl.BlockSpec((1, D_MODEL), const),
                  pl.BlockSpec((D_MODEL, LANES), const), pl.BlockSpec((1, LANES), const)],
        out_specs=[pl.BlockSpec((tm, D_MODEL), row), pl.BlockSpec((tm, D_MODEL // 2), row),
                   pl.BlockSpec((tm, LANES), row)],
        out_shape=[jax.ShapeDtypeStruct((T, D_MODEL), F32), jax.ShapeDtypeStruct((T, D_MODEL // 2), jnp.uint32),
                   jax.ShapeDtypeStruct((T, LANES), F32)],
        compiler_params=_params("parallel"),
        name="ln1_router",
    )(mix, x2, g.reshape(1, -1), b.reshape(1, -1), wr, br)


MOE_TM = 256
W_PARTS = 4
W_STAGES = 2


def _expert_weight_copies(srcs, e, stage_ref, st, sem):
    copies = []
    for which, src in enumerate(srcs):
        rows = src.shape[1] // W_PARTS
        for p in range(W_PARTS):
            rs = pl.ds(p * rows, rows)
            copies.append(pltpu.make_async_copy(src.at[e, rs, :], stage_ref.at[st, which, rs, :], sem.at[st]))
    return copies


def _expert_weight_pipeline(b, first_ref, run_ref, rune_ref, srcs, stage_ref, bf16_refs, sem):
    @pl.when(b == 0)
    def _():
        for st in range(W_STAGES):
            @pl.when(rune_ref[st] < N_EXPERTS)
            def _(st=st):
                for c in _expert_weight_copies(srcs, rune_ref[st], stage_ref, st, sem):
                    c.start()

    @pl.when(first_ref[b] == 1)
    def _():
        k = run_ref[b]
        st = k % W_STAGES
        for c in _expert_weight_copies(srcs, 0, stage_ref, st, sem):
            c.wait()
        for which, dst in enumerate(bf16_refs):
            rows = dst.shape[0] // W_PARTS
            for p in range(W_PARTS):
                rs = slice(p * rows, (p + 1) * rows)
                dst[rs, :] = stage_ref[st, which, rs, :].astype(BF16)
        e_next = rune_ref[k + W_STAGES]

        @pl.when(e_next < N_EXPERTS)
        def _():
            for c in _expert_weight_copies(srcs, e_next, stage_ref, st, sem):
                c.start()


def _moe_up_kernel(first_ref, run_ref, rune_ref, nv_ref, tok_ref, h_hbm, wg_hbm, wu_hbm, hid_ref,
                   xbuf, wstage, wgb, wub, gsem, wsem):
    b = pl.program_id(0)
    nv = nv_ref[0]

    def row_copy(tok, slot, r):
        return pltpu.make_async_copy(h_hbm.at[pl.ds(tok, 1), :], xbuf.at[slot, pl.ds(r, 1), :], gsem.at[slot])

    def start_gather_loop(blk, slot):
        def body(r, c):
            row_copy(tok_ref[blk * MOE_TM + r], slot, r).start()
            return c
        lax.fori_loop(0, MOE_TM, body, 0, unroll=8)

    def start_gather_inline(blk, slot):
        for r in range(MOE_TM):
            row_copy(tok_ref[blk * MOE_TM + r], slot, r).start()

    def wait_gather(slot):
        def body(r, c):
            row_copy(0, slot, r).wait()
            return c
        lax.fori_loop(0, MOE_TM, body, 0, unroll=8)

    def compute(slot):
        lo, hi = _unpack_bf16_pairs(xbuf[slot])
        x = jnp.concatenate([lo.astype(BF16), hi.astype(BF16)], axis=1)
        gate = jnp.dot(x, wgb[...], preferred_element_type=F32)
        up = jnp.dot(x, wub[...], preferred_element_type=F32)
        hid_ref[...] = (gate * jax.nn.sigmoid(gate) * up).astype(hid_ref.dtype)

    @pl.when(b == 0)
    def _():
        start_gather_loop(0, 0)

    _expert_weight_pipeline(b, first_ref, run_ref, rune_ref, (wg_hbm, wu_hbm), wstage, (wgb, wub), wsem)

    @pl.when(b + 1 < nv)
    def _():
        wait_gather(b % 2)
        start_gather_inline(b + 1, (b + 1) % 2)
        compute(b % 2)

    @pl.when(b + 1 == nv)
    def _():
        wait_gather(b % 2)
        compute(b % 2)

    @pl.when(b >= nv)
    def _():
        hid_ref[...] = jnp.zeros(hid_ref.shape, hid_ref.dtype)


def _moe_down_kernel(first_ref, run_ref, rune_ref, nv_ref, hid_ref, rw_ref, wd_hbm, out_ref, wstage, wdb, wsem):
    b = pl.program_id(0)
    nv = nv_ref[0]
    _expert_weight_pipeline(b, first_ref, run_ref, rune_ref, (wd_hbm,), wstage, (wdb,), wsem)

    @pl.when(b < nv)
    def _():
        y = jnp.dot(hid_ref[...], wdb[...], preferred_element_type=F32)
        out_ref[...] = _pack_bf16_pairs(y * rw_ref[...])

    @pl.when(b >= nv)
    def _():
        out_ref[...] = jnp.zeros(out_ref.shape, out_ref.dtype)


def _moe_experts(h1p, row_tok, row_w, first, run_idx, run_e, n_valid, wg, wu, wd):
    rows = row_tok.shape[0]
    nb = rows // MOE_TM
    hbm = pl.BlockSpec(memory_space=pl.ANY)
    hid = pl.pallas_call(
        _moe_up_kernel,
        grid_spec=pltpu.PrefetchScalarGridSpec(
            num_scalar_prefetch=5,
            grid=(nb,),
            in_specs=[hbm, hbm, hbm],
            out_specs=pl.BlockSpec((MOE_TM, D_EXPERT), lambda b, *_: (b, 0)),
            scratch_shapes=[pltpu.VMEM((2, MOE_TM, D_MODEL // 2), jnp.uint32),
                            pltpu.VMEM((W_STAGES, 2, D_MODEL, D_EXPERT), F32),
                            pltpu.VMEM((D_MODEL, D_EXPERT), BF16), pltpu.VMEM((D_MODEL, D_EXPERT), BF16),
                            pltpu.SemaphoreType.DMA((2,)), pltpu.SemaphoreType.DMA((W_STAGES,))]),
        out_shape=jax.ShapeDtypeStruct((rows, D_EXPERT), BF16),
        compiler_params=_params("arbitrary", vmem=VMEM_LIMIT_MAX),
        name="moe_up",
    )(first, run_idx, run_e, n_valid, row_tok, h1p, wg, wu)
    return pl.pallas_call(
        _moe_down_kernel,
        grid_spec=pltpu.PrefetchScalarGridSpec(
            num_scalar_prefetch=4,
            grid=(nb,),
            in_specs=[pl.BlockSpec((MOE_TM, D_EXPERT), lambda b, *_: (b, 0)),
                      pl.BlockSpec((MOE_TM, 1), lambda b, *_: (b, 0)),
                      hbm],
            out_specs=pl.BlockSpec((MOE_TM, D_MODEL // 2), lambda b, *_: (b, 0)),
            scratch_shapes=[pltpu.VMEM((W_STAGES, 1, D_EXPERT, D_MODEL), F32), pltpu.VMEM((D_EXPERT, D_MODEL), BF16),
                            pltpu.SemaphoreType.DMA((W_STAGES,))]),
        out_shape=jax.ShapeDtypeStruct((rows, D_MODEL // 2), jnp.uint32),
        compiler_params=_params("arbitrary"),
        name="moe_down",
    )(first, run_idx, run_e, n_valid, hid, row_w.reshape(rows, 1), wd)


FIN_TM = 256


def _final_kernel(pos_ref, h_ref, y_hbm, g_ref, b_ref, out_ref, ybuf, sem):
    i = pl.program_id(0)
    n = pl.num_programs(0)

    def row_copy(src_row, slot, r):
        return pltpu.make_async_copy(y_hbm.at[pl.ds(src_row, 1), :], ybuf.at[slot, pl.ds(r, 1), :], sem.at[slot])

    def start_gather(tile, slot):
        def body(r, c):
            for k in range(TOP_K):
                row_copy(pos_ref[(tile * FIN_TM + r) * TOP_K + k], slot, k * FIN_TM + r).start()
            return c
        lax.fori_loop(0, FIN_TM, body, 0, unroll=4)

    def wait_gather(slot):
        def body(r, c):
            row_copy(0, slot, r).wait()
            return c
        lax.fori_loop(0, TOP_K * FIN_TM, body, 0, unroll=8)

    @pl.when(i == 0)
    def _():
        start_gather(0, 0)

    @pl.when(i + 1 < n)
    def _():
        start_gather(i + 1, (i + 1) % 2)

    slot = i % 2
    wait_gather(slot)
    ff_lo, ff_hi = _unpack_bf16_pairs(ybuf[slot, 0:FIN_TM, :])
    for k in range(1, TOP_K):
        lo, hi = _unpack_bf16_pairs(ybuf[slot, k * FIN_TM:(k + 1) * FIN_TM, :])
        ff_lo, ff_hi = ff_lo + lo, ff_hi + hi
    z = DEEPNORM_ALPHA * h_ref[...] + jnp.concatenate([ff_lo, ff_hi], axis=1)
    out_ref[...] = _layer_norm(z, g_ref[...], b_ref[...])


def _moe_combine_ln2(h1, moe_rows, pos, g, b):
    T = h1.shape[0]
    const = lambda i, pos: (0, 0)
    grid_spec = pltpu.PrefetchScalarGridSpec(
        num_scalar_prefetch=1,
        grid=(T // FIN_TM,),
        in_specs=[pl.BlockSpec((FIN_TM, D_MODEL), lambda i, pos: (i, 0)),
                  pl.BlockSpec(memory_space=pl.ANY),
                  pl.BlockSpec((1, D_MODEL), const), pl.BlockSpec((1, D_MODEL), const)],
        out_specs=pl.BlockSpec((FIN_TM, D_MODEL), lambda i, pos: (i, 0)),
        scratch_shapes=[pltpu.VMEM((2, TOP_K * FIN_TM, D_MODEL // 2), jnp.uint32), pltpu.SemaphoreType.DMA((2,))],
    )
    return pl.pallas_call(
        _final_kernel,
        grid_spec=grid_spec,
        out_shape=jax.ShapeDtypeStruct((T, D_MODEL), F32),
        compiler_params=_params("arbitrary"),
        name="moe_combine_ln2",
    )(pos, h1, moe_rows, g.reshape(1, -1), b.reshape(1, -1))


def _routing(logits):
    T = logits.shape[0]
    logits_g = logits[:, :N_GROUPS]
    logits_e = logits[:, N_GROUPS:N_GROUPS + N_EXPERTS].reshape(T, N_GROUPS, EXPERTS_PER_GROUP)
    p_g = jax.nn.softmax(logits_g, axis=-1)
    g_idx = jnp.argmax(logits_g, axis=-1)
    p_grp = jnp.take_along_axis(p_g, g_idx[:, None], axis=-1)
    le = jnp.take_along_axis(logits_e, g_idx[:, None, None], axis=1)[:, 0]
    top_v, top_i = lax.top_k(le, TOP_K)
    weight = p_grp * jax.nn.softmax(top_v, axis=-1)
    expert_id = g_idx[:, None].astype(jnp.int32) * EXPERTS_PER_GROUP + top_i.astype(jnp.int32)

    A = T * TOP_K
    flat_e = expert_id.reshape(-1)
    flat_w = weight.reshape(-1)
    experts = jnp.arange(N_EXPERTS, dtype=jnp.int32)
    counts = jnp.sum((flat_e[:, None] == experts[None, :]).astype(jnp.int32), axis=0)
    starts = jnp.cumsum(counts) - counts
    padded = (counts + MOE_TM - 1) // MOE_TM * MOE_TM
    pends = jnp.cumsum(padded)
    pstarts = pends - padded
    order = jnp.argsort(flat_e, stable=True).astype(jnp.int32)
    inv = jnp.argsort(order).astype(jnp.int32)
    pos = (inv + (pstarts - starts)[flat_e]).astype(jnp.int32)
    nb = A // MOE_TM + N_EXPERTS
    n_valid = (pends[-1] // MOE_TM).astype(jnp.int32)
    blk = jnp.arange(nb, dtype=jnp.int32)
    blk_start = jnp.minimum(blk, n_valid - 1) * MOE_TM
    block_e = jnp.minimum(jnp.sum((blk_start[:, None] >= pends[None, :]).astype(jnp.int32), axis=1), N_EXPERTS - 1)
    i_in = (blk * MOE_TM - pstarts[block_e])[:, None] + jnp.arange(MOE_TM, dtype=jnp.int32)[None, :]
    valid = i_in < counts[block_e][:, None]
    a_r = order[jnp.clip(starts[block_e][:, None] + i_in, 0, A - 1)]
    row_tok = jnp.where(valid, a_r // TOP_K, 0).astype(jnp.int32).reshape(-1)
    row_w = jnp.where(valid, flat_w[a_r], 0.0).astype(F32).reshape(-1)
    prev_e = jnp.concatenate([jnp.full((1,), -1, jnp.int32), block_e[:-1]])
    first = jnp.logical_and(block_e != prev_e, blk < n_valid).astype(jnp.int32)
    run_idx = jnp.maximum(jnp.cumsum(first) - 1, 0).astype(jnp.int32)
    run_e = jnp.concatenate([jnp.sort(jnp.where(counts > 0, experts, N_EXPERTS)),
                             jnp.full((W_STAGES,), N_EXPERTS, jnp.int32)]).astype(jnp.int32)
    return row_tok, row_w, first, run_idx, run_e, n_valid.reshape(1), pos


def kernel(x, positions, w_in, b_gate, conv_w, conv_b, w_rg_a, b_rg_a, w_rg_x, b_rg_x, lru_lambda, w_attn_proj, w_rec_proj, w_out, ln1_g, ln1_b, w_router_group, b_router_group, w_router_expert, b_router_expert, w_gate, w_up, w_down, ln2_g, ln2_b):
    B, S, D = x.shape
    T = B * S
    h = x.reshape(T, D)
    c, sa, sb = _rope_tables(positions)
    for layer in range(DEPTH):
        proj = _in_proj(h.astype(BF16), w_in[layer], c, sa, sb)
        os_, lses = [], []
        for g, (_, dilation) in enumerate(ATTN_GROUPS):
            o, lse = _attention_group(proj, B, S, g, dilation)
            os_.append(o)
            lses.append(lse)
        rec = _rglru(proj, B, S, conv_w[layer], conv_b[layer], w_rg_a[layer].astype(BF16), b_rg_a[layer],
                     w_rg_x[layer].astype(BF16), b_rg_x[layer], lru_lambda[layer])
        merged = _merge(os_, lses, rec, proj, b_gate[layer], w_attn_proj[layer].astype(BF16),
                        w_rec_proj[layer].astype(BF16))
        w_router = jnp.concatenate([w_router_group[layer], w_router_expert[layer]], axis=1)
        w_router = jnp.pad(w_router, ((0, 0), (0, LANES - w_router.shape[1])))
        b_router = jnp.concatenate([b_router_group[layer], b_router_expert[layer]])
        b_router = jnp.pad(b_router, (0, LANES - b_router.shape[0])).reshape(1, LANES)
        mix = _out_matmul(merged, w_out[layer].astype(BF16))
        h1, h1p, logits = _ln1_router(mix, h, ln1_g[layer], ln1_b[layer], w_router.astype(BF16), b_router)
        row_tok, row_w, first, run_idx, run_e, n_valid, pos = _routing(logits)
        moe_rows = _moe_experts(h1p, row_tok, row_w, first, run_idx, run_e, n_valid,
                                w_gate[layer], w_up[layer], w_down[layer])
        h = _moe_combine_ln2(h1, moe_rows, pos, ln2_g[layer], ln2_b[layer])
    return h.reshape(B, S, D)
```

```python
import functools

import jax
import jax.numpy as jnp
from jax import lax
from jax.experimental import pallas as pl
from jax.experimental.pallas import tpu as pltpu

D_MODEL = 4096
HEAD_DIM = 128
ATTN_GROUPS = ((128, 1), (512, 4), (2048, 16))
HEADS_PER_GROUP = 8
N_ATTN_HEADS = HEADS_PER_GROUP * len(ATTN_GROUPS)
ATTN_WIDTH = N_ATTN_HEADS * HEAD_DIM
ATTN_OUT_WIDTH = HEADS_PER_GROUP * HEAD_DIM
ROT_DIM = HEAD_DIM // 4
ROPE_THETA = 500000.0
ATTN_SPAN = 128
LRU_WIDTH = 2048
LRU_BLOCKS = 8
LRU_BLOCK_W = LRU_WIDTH // LRU_BLOCKS
CONV_WIDTH = 4
RG_C = 8.0
N_GROUPS = 8
EXPERTS_PER_GROUP = 8
N_EXPERTS = N_GROUPS * EXPERTS_PER_GROUP
TOP_K = 2
D_EXPERT = 512
LN_EPS = 1e-5
DEPTH = 1
DEEPNORM_ALPHA = (2 * DEPTH) ** 0.25
REST_OFF = 3 * ATTN_WIDTH
REST_COLS = 2 * LRU_WIDTH + 2 * D_MODEL
RX_COL = 0
RG_COL = LRU_WIDTH
GATE_COL = 2 * LRU_WIDTH

LANES = 128
COL_TILE = 1024
PROJ_TM = 1024
PROJ_TN = 512
VMEM_LIMIT = 56 * 1024 * 1024

F32 = jnp.float32
BF16 = jnp.bfloat16


def _params(*sem, vmem=VMEM_LIMIT):
    return pltpu.CompilerParams(dimension_semantics=sem, vmem_limit_bytes=vmem)


def _rope_table_kernel(pos_ref, invf_ref, c_ref, sa_ref, sb_ref):
    half = ROT_DIM // 2
    ang = pos_ref[...].astype(F32) * invf_ref[...]
    lane = lax.broadcasted_iota(jnp.int32, ang.shape, 1)
    cos = jnp.cos(ang)
    sin = jnp.sin(ang)
    c_ref[...] = jnp.where(lane < ROT_DIM, cos, 1.0)
    sa_ref[...] = jnp.where(lane < half, -sin, 0.0)
    sb_ref[...] = jnp.where((lane >= half) & (lane < ROT_DIM), sin, 0.0)


def _rope_tables(positions):
    T = positions.size
    tm = 2048
    half = ROT_DIM // 2
    inv_freq = jnp.power(jnp.float32(ROPE_THETA), -jnp.arange(half, dtype=F32) * 2.0 / ROT_DIM)
    invf = jnp.zeros((1, LANES), F32).at[0, :ROT_DIM].set(jnp.concatenate([inv_freq, inv_freq]))
    tab = jax.ShapeDtypeStruct((T, LANES), F32)
    return pl.pallas_call(
        _rope_table_kernel,
        grid=(T // tm,),
        in_specs=[pl.BlockSpec((tm, 1), lambda i: (i, 0)), pl.BlockSpec((1, LANES), lambda i: (0, 0))],
        out_specs=[pl.BlockSpec((tm, LANES), lambda i: (i, 0))] * 3,
        out_shape=[tab, tab, tab],
        compiler_params=_params("arbitrary"),
        name="rope_tables",
    )(positions.reshape(T, 1), invf)


def _qkv_proj_kernel(x_ref, w_ref, c_ref, sa_ref, sb_ref, o_ref, wb_ref, slab_ref, *, d):
    n = pl.program_id(0)
    m = pl.program_id(1)
    tiles_per_sect = ATTN_OUT_WIDTH // PROJ_TN

    @pl.when(m == 0)
    def _():
        wb_ref[...] = w_ref[...].astype(BF16)

    sect = n // tiles_per_sect
    rot = sect < 2
    scale = jnp.where(sect == 0, HEAD_DIM ** -0.5, 1.0).astype(F32)
    c = jnp.where(rot, c_ref[...], 1.0)
    sa = jnp.where(rot, sa_ref[...], 0.0)
    sb = jnp.where(rot, sb_ref[...], 0.0)
    x = x_ref[...]
    per = PROJ_TM // d
    half_w = PROJ_TN // 2
    heads_per_half = half_w // HEAD_DIM
    for half in range(2):
        acc = jnp.dot(x, wb_ref[:, half * half_w:(half + 1) * half_w], preferred_element_type=F32)
        for hh in range(heads_per_half):
            h = half * heads_per_half + hh
            hs = slice(h * HEAD_DIM, (h + 1) * HEAD_DIM)
            t = acc[:, hh * HEAD_DIM:(hh + 1) * HEAD_DIM]
            u = pltpu.bitcast(t.astype(BF16), jnp.uint32)
            ta = pltpu.bitcast(pltpu.roll(u, HEAD_DIM - ROT_DIM // 2, 1), BF16).astype(F32)
            tb = pltpu.bitcast(pltpu.roll(u, ROT_DIM // 2, 1), BF16).astype(F32)
            r = (t * c + ta * sa + tb * sb) * scale
            if d == 1:
                o_ref[:, hs] = r.astype(o_ref.dtype)
            else:
                slab_ref[h] = r
                for res in range(d):
                    rows = slab_ref[h, pl.ds(res, per, stride=d), :]
                    o_ref[res * per:(res + 1) * per, hs] = rows.astype(o_ref.dtype)


def _qkv_proj(xb, w, c, sa, sb, g, d):
    T = xb.shape[0]
    tiles_per_sect = ATTN_OUT_WIDTH // PROJ_TN
    sect_tiles = ATTN_WIDTH // PROJ_TN
    tab_spec = pl.BlockSpec((PROJ_TM, LANES), lambda n, m: (m, 0))
    wcol = lambda n: (n // tiles_per_sect) * sect_tiles + g * tiles_per_sect + n % tiles_per_sect
    return pl.pallas_call(
        functools.partial(_qkv_proj_kernel, d=d),
        grid=(3 * tiles_per_sect, T // PROJ_TM),
        in_specs=[pl.BlockSpec((PROJ_TM, D_MODEL), lambda n, m: (m, 0)),
                  pl.BlockSpec((D_MODEL, PROJ_TN), lambda n, m: (0, wcol(n))),
                  tab_spec, tab_spec, tab_spec],
        out_specs=pl.BlockSpec((PROJ_TM, PROJ_TN), lambda n, m: (m, n)),
        out_shape=jax.ShapeDtypeStruct((T, 3 * ATTN_OUT_WIDTH), BF16),
        scratch_shapes=[pltpu.VMEM((D_MODEL, PROJ_TN), BF16),
                        pltpu.VMEM((PROJ_TN // HEAD_DIM, PROJ_TM, LANES), F32)],
        compiler_params=_params("arbitrary", "arbitrary"),
        name=f"qkv_proj_d{d}",
    )(xb, w, c, sa, sb)


def _rest_proj_kernel(x_ref, w_ref, o_ref, wb_ref):
    @pl.when(pl.program_id(1) == 0)
    def _():
        wb_ref[...] = w_ref[...].astype(BF16)

    o_ref[...] = jnp.dot(x_ref[...], wb_ref[...], preferred_element_type=F32).astype(o_ref.dtype)


def _rest_proj(xb, w):
    T = xb.shape[0]
    first = REST_OFF // PROJ_TN
    return pl.pallas_call(
        _rest_proj_kernel,
        grid=(REST_COLS // PROJ_TN, T // PROJ_TM),
        in_specs=[pl.BlockSpec((PROJ_TM, D_MODEL), lambda n, m: (m, 0)),
                  pl.BlockSpec((D_MODEL, PROJ_TN), lambda n, m: (0, first + n))],
        out_specs=pl.BlockSpec((PROJ_TM, PROJ_TN), lambda n, m: (m, n)),
        out_shape=jax.ShapeDtypeStruct((T, REST_COLS), BF16),
        scratch_shapes=[pltpu.VMEM((D_MODEL, PROJ_TN), BF16)],
        compiler_params=_params("arbitrary", "arbitrary"),
        name="rest_proj",
    )(xb, w)


def _attn_kernel(*refs, npc):
    q_refs = refs[:npc]
    kp_refs, kc_refs = refs[npc:2 * npc], refs[2 * npc:3 * npc]
    vp_refs, vc_refs = refs[3 * npc:4 * npc], refs[4 * npc:5 * npc]
    o_ref, lse_ref = refs[5 * npc:]
    n = pl.program_id(2)
    nh, blk = HEADS_PER_GROUP, ATTN_SPAN

    def head(pieces, h):
        hs = slice(h * HEAD_DIM, (h + 1) * HEAD_DIM)
        parts = [r[:, hs] for r in pieces]
        return parts[0] if len(parts) == 1 else jnp.concatenate(parts, axis=0)

    dn = (((1,), (1,)), ((), ()))
    qs = [head(q_refs, h) for h in range(nh)]
    sp = jnp.concatenate([lax.dot_general(qs[h], head(kp_refs, h), dn, preferred_element_type=F32)
                          for h in range(nh)], axis=0)
    sc = jnp.concatenate([lax.dot_general(qs[h], head(kc_refs, h), dn, preferred_element_type=F32)
                          for h in range(nh)], axis=0)
    row = lax.broadcasted_iota(jnp.int32, sp.shape, 0) & (blk - 1)
    col = lax.broadcasted_iota(jnp.int32, sp.shape, 1)
    sp = jnp.where(jnp.logical_and(col >= row, n > 0), sp, -jnp.inf)
    sc = jnp.where(col <= row, sc, -jnp.inf)
    m = jnp.maximum(jnp.max(sp, axis=1, keepdims=True), jnp.max(sc, axis=1, keepdims=True))
    pp = jnp.exp(sp - m)
    pc = jnp.exp(sc - m)
    l = jnp.sum(pp, axis=1, keepdims=True) + jnp.sum(pc, axis=1, keepdims=True)
    inv = 1.0 / l
    lse = m + jnp.log(l)
    pp = pp.astype(BF16)
    pc = pc.astype(BF16)
    lane = lax.broadcasted_iota(jnp.int32, (blk, LANES), 1)
    lse_all = jnp.zeros((blk, LANES), F32)
    outs = []
    for h in range(nh):
        rs = slice(h * blk, (h + 1) * blk)
        o = (jnp.dot(pp[rs], head(vp_refs, h), preferred_element_type=F32)
             + jnp.dot(pc[rs], head(vc_refs, h), preferred_element_type=F32))
        outs.append((o * inv[rs]).astype(o_ref.dtype))
        lse_all = jnp.where(lane == h, lse[rs], lse_all)
    o_ref[0] = jnp.concatenate(outs, axis=1)
    lse_ref[0] = lse_all


def _attention_group(qkv, batch, seq, dilation):
    L = seq // dilation
    per = PROJ_TM // dilation
    pr = min(per, ATTN_SPAN)
    npc = ATTN_SPAN // pr
    nblk = L // ATTN_SPAN
    tiles_per_batch = seq // PROJ_TM
    qcol, kcol, vcol = 0, 1, 2

    def piece_block(b, res, n, p):
        pos = n * ATTN_SPAN + p * pr
        row = (b * tiles_per_batch + pos // per) * PROJ_TM + res * per + pos % per
        return row // pr

    def specs(col, back):
        return [pl.BlockSpec((pr, COL_TILE),
                             lambda b, r, n, p=p: (piece_block(b, r, jnp.maximum(n - back, 0), p), col))
                for p in range(npc)]

    in_specs = specs(qcol, 0) + specs(kcol, 1) + specs(kcol, 0) + specs(vcol, 1) + specs(vcol, 0)
    o, lse = pl.pallas_call(
        functools.partial(_attn_kernel, npc=npc),
        grid=(batch, dilation, nblk),
        in_specs=in_specs,
        out_specs=[pl.BlockSpec((1, ATTN_SPAN, ATTN_OUT_WIDTH), lambda b, r, n: (b, n, r)),
                   pl.BlockSpec((1, ATTN_SPAN, LANES), lambda b, r, n: (b, n, r))],
        out_shape=[jax.ShapeDtypeStruct((batch, L, dilation * ATTN_OUT_WIDTH), BF16),
                   jax.ShapeDtypeStruct((batch, L, dilation * LANES), F32)],
        compiler_params=_params("parallel", "parallel", "arbitrary"),
        name=f"attn_d{dilation}",
    )(*([qkv] * (5 * npc)))
    return o.reshape(batch * seq, ATTN_OUT_WIDTH), lse.reshape(batch * seq, LANES)


_TAIL = 8


def _rglru_kernel(rx_ref, rg_ref, cw_ref, cb_ref, wa_ref, ba_ref, wx_ref, bx_ref, lam_ref, o_ref,
                  xbuf, a_buf, h_buf, carry):
    tt = rx_ref.shape[0]
    t_idx = pl.program_id(2)

    @pl.when(t_idx == 0)
    def _():
        xbuf[0:_TAIL, :] = jnp.zeros((_TAIL, xbuf.shape[1]), F32)
        carry[...] = jnp.zeros(carry.shape, F32)

    x = rx_ref[...].astype(F32)
    xbuf[_TAIL:_TAIL + tt, :] = x
    xr = cb_ref[...] + cw_ref[0:1, :] * xbuf[_TAIL - 3:_TAIL - 3 + tt, :]
    for j in range(1, CONV_WIDTH):
        xr = xr + cw_ref[j:j + 1, :] * xbuf[_TAIL - 3 + j:_TAIL - 3 + j + tt, :]
    xbuf[0:_TAIL, :] = x[tt - _TAIL:tt, :]

    softplus_neg_lam = jnp.log1p(jnp.exp(-lam_ref[...]))
    for blk in range(xr.shape[1] // LRU_BLOCK_W):
        cs = slice(blk * LRU_BLOCK_W, (blk + 1) * LRU_BLOCK_W)
        xb = xr[:, cs]
        xb16 = xb.astype(BF16)
        r = jax.nn.sigmoid(jnp.dot(xb16, wa_ref[blk], preferred_element_type=F32) + ba_ref[:, cs])
        i = jax.nn.sigmoid(jnp.dot(xb16, wx_ref[blk], preferred_element_type=F32) + bx_ref[:, cs])
        log_a = (-RG_C) * r * softplus_neg_lam[:, cs]
        a = jnp.exp(log_a)
        a_buf[:, cs] = a
        h_buf[:, cs] = jnp.sqrt(1.0 - a * a) * (i * xb)

    def step(t, h):
        h = a_buf[pl.ds(t, 1), :] * h + h_buf[pl.ds(t, 1), :]
        h_buf[pl.ds(t, 1), :] = h
        return h

    carry[...] = lax.fori_loop(0, tt, step, carry[...], unroll=8)
    o_ref[...] = (h_buf[...] * jax.nn.gelu(rg_ref[...].astype(F32))).astype(o_ref.dtype)


def _rglru(rest, batch, seq, conv_w, conv_b, wa, ba, wx, bx, lam):
    tt = 512
    nt = seq // tt
    hw = COL_TILE
    assert hw % LRU_BLOCK_W == 0 and RX_COL % hw == 0 and RG_COL % hw == 0
    gb = hw // LRU_BLOCK_W
    row = lambda off: pl.BlockSpec((tt, hw), lambda b, c, t: (b * nt + t, off // hw + c))
    vec = pl.BlockSpec((1, hw), lambda b, c, t: (0, c))
    wspec = pl.BlockSpec((gb, LRU_BLOCK_W, LRU_BLOCK_W), lambda b, c, t: (c, 0, 0))
    return pl.pallas_call(
        _rglru_kernel,
        grid=(batch, LRU_WIDTH // hw, nt),
        in_specs=[row(RX_COL), row(RG_COL), pl.BlockSpec((CONV_WIDTH, hw), lambda b, c, t: (0, c)), vec,
                  wspec, vec, wspec, vec, vec],
        out_specs=pl.BlockSpec((tt, hw), lambda b, c, t: (b * nt + t, c)),
        out_shape=jax.ShapeDtypeStruct((batch * seq, LRU_WIDTH), BF16),
        scratch_shapes=[pltpu.VMEM((tt + _TAIL, hw), F32), pltpu.VMEM((tt, hw), F32),
                        pltpu.VMEM((tt, hw), F32), pltpu.VMEM((1, hw), F32)],
        compiler_params=_params("parallel", "parallel", "arbitrary"),
        name="rg_lru",
    )(rest, rest, conv_w, conv_b.reshape(1, -1), wa, ba.reshape(1, -1), wx, bx.reshape(1, -1), lam.reshape(1, -1))


def _merge_kernel(*refs):
    o0_ref, o1_ref, o2_ref, l0_ref, l1_ref, l2_ref, rec_ref = refs[:7]
    nct = D_MODEL // COL_TILE
    ga_refs = refs[7:7 + nct]
    gr_refs = refs[7 + nct:7 + 2 * nct]
    bg_ref, wa_ref, wr_ref, out_ref = refs[7 + 2 * nct:]
    l0, l1, l2 = l0_ref[...], l1_ref[...], l2_ref[...]
    parts = []
    for h in range(HEADS_PER_GROUP):
        hs = slice(h * HEAD_DIM, (h + 1) * HEAD_DIM)
        a0, a1, a2 = l0[:, h:h + 1], l1[:, h:h + 1], l2[:, h:h + 1]
        m = jnp.maximum(jnp.maximum(a0, a1), a2)
        e0, e1, e2 = jnp.exp(a0 - m), jnp.exp(a1 - m), jnp.exp(a2 - m)
        inv = 1.0 / (e0 + e1 + e2)
        mixed = ((e0 * inv) * o0_ref[:, hs].astype(F32) + (e1 * inv) * o1_ref[:, hs].astype(F32)
                 + (e2 * inv) * o2_ref[:, hs].astype(F32))
        parts.append(mixed.astype(BF16))
    attn = jnp.concatenate(parts, axis=1)
    rec = rec_ref[...]
    for j in range(nct):
        cs = slice(j * COL_TILE, (j + 1) * COL_TILE)
        ya = jnp.dot(attn, wa_ref[:, cs], preferred_element_type=F32)
        yr = jnp.dot(rec, wr_ref[:, cs], preferred_element_type=F32)
        gate_a = jax.nn.sigmoid(ga_refs[j][...].astype(F32) + bg_ref[0:1, cs])
        gate_r = jax.nn.sigmoid(gr_refs[j][...].astype(F32) + bg_ref[1:2, cs])
        out_ref[:, cs] = (gate_a * ya + gate_r * yr).astype(out_ref.dtype)


def _merge(os_, lses, rec, rest, b_gate, wa, wr):
    T = rec.shape[0]
    tm = 256
    nct = D_MODEL // COL_TILE
    row = lambda m: (m, 0)
    const = lambda m: (0, 0)
    gate_specs = [pl.BlockSpec((tm, COL_TILE), lambda m, c=(GATE_COL + k * D_MODEL) // COL_TILE + j: (m, c))
                  for k in range(2) for j in range(nct)]
    o_spec = pl.BlockSpec((tm, ATTN_OUT_WIDTH), row)
    l_spec = pl.BlockSpec((tm, LANES), row)
    resident = lambda shape: pl.BlockSpec(shape, const, pipeline_mode=pl.Buffered(1))
    return pl.pallas_call(
        _merge_kernel,
        grid=(T // tm,),
        in_specs=[o_spec, o_spec, o_spec, l_spec, l_spec, l_spec, pl.BlockSpec((tm, LRU_WIDTH), row)]
                 + gate_specs
                 + [resident((2, D_MODEL)), resident((ATTN_OUT_WIDTH, D_MODEL)), resident((LRU_WIDTH, D_MODEL))],
        out_specs=pl.BlockSpec((tm, D_MODEL), row),
        out_shape=jax.ShapeDtypeStruct((T, D_MODEL), BF16),
        compiler_params=_params("parallel"),
        name="merge",
    )(*os_, *lses, rec, *([rest] * (2 * nct)), b_gate, wa, wr)


def _layer_norm(z, g, b):
    mu = jnp.mean(z, axis=-1, keepdims=True)
    zc = z - mu
    var = jnp.mean(zc * zc, axis=-1, keepdims=True)
    return zc * lax.rsqrt(var + LN_EPS) * g + b


_HI_MASK = 0xFFFF0000


def _pack_bf16_pairs(y):
    bits = lax.bitcast_convert_type(y.astype(BF16).astype(F32), jnp.uint32)
    c = y.shape[1] // 2
    return (bits[:, :c] >> 16) | (bits[:, c:] & jnp.uint32(_HI_MASK))


def _unpack_bf16_pairs(p):
    lo = lax.bitcast_convert_type(p << 16, F32)
    hi = lax.bitcast_convert_type(p & jnp.uint32(_HI_MASK), F32)
    return lo, hi


def _out_matmul_kernel(mg_ref, w_ref, o_ref):
    o_ref[...] = jnp.dot(mg_ref[...], w_ref[...], preferred_element_type=F32).astype(o_ref.dtype)


def _out_matmul(merged, w_out):
    T = merged.shape[0]
    tm, tn = 1024, 1024
    return pl.pallas_call(
        _out_matmul_kernel,
        grid=(T // tm, D_MODEL // tn),
        in_specs=[pl.BlockSpec((tm, D_MODEL), lambda m, n: (m, 0)),
                  pl.BlockSpec((D_MODEL, tn), lambda m, n: (0, n))],
        out_specs=pl.BlockSpec((tm, tn), lambda m, n: (m, n)),
        out_shape=jax.ShapeDtypeStruct((T, D_MODEL), BF16),
        compiler_params=_params("parallel", "arbitrary"),
        name="out_matmul",
    )(merged, w_out)


def _ln1_kernel(mix_ref, x_ref, g_ref, b_ref, wr_ref, br_ref, h_ref, hp_ref, lg_ref):
    z = DEEPNORM_ALPHA * x_ref[...] + mix_ref[...].astype(F32)
    y = _layer_norm(z, g_ref[...], b_ref[...])
    h_ref[...] = y
    hp_ref[...] = _pack_bf16_pairs(y)
    lg_ref[...] = jnp.dot(y.astype(BF16), wr_ref[...], preferred_element_type=F32) + br_ref[...]


def _ln1_router(mix, x2, g, b, wr, br):
    T = mix.shape[0]
    tm = 256
    row = lambda m: (m, 0)
    const = lambda m: (0, 0)
    return pl.pallas_call(
        _ln1_kernel,
        grid=(T // tm,),
        in_specs=[pl.BlockSpec((tm, D_MODEL), row), pl.BlockSpec((tm, D_MODEL), row),
                  pl.BlockSpec((1, D_MODEL), const), pl.BlockSpec((1, D_MODEL), const),
                  pl.BlockSpec((D_MODEL, LANES), const), pl.BlockSpec((1, LANES), const)],
        out_specs=[pl.BlockSpec((tm, D_MODEL), row), pl.BlockSpec((tm, D_MODEL // 2), row),
                   pl.BlockSpec((tm, LANES), row)],
        out_shape=[jax.ShapeDtypeStruct((T, D_MODEL), F32), jax.ShapeDtypeStruct((T, D_MODEL // 2), jnp.uint32),
                   jax.ShapeDtypeStruct((T, LANES), F32)],
        compiler_params=_params("parallel"),
        name="ln1_router",
    )(mix, x2, g.reshape(1, -1), b.reshape(1, -1), wr, br)


MOE_TM = 256
W_PARTS = 4
W_STAGES = 2


def _moe_up_kernel(be_ref, nv_ref, tok_ref, h_hbm, wg_ref, wu_ref, hid_ref, xbuf, sem):
    b = pl.program_id(0)
    nv = nv_ref[0]
    words = MOE_TM // 2

    def row_copy(tok, slot, r):
        return pltpu.make_async_copy(h_hbm.at[pl.ds(tok, 1), :], xbuf.at[slot, pl.ds(r, 1), :], sem.at[slot])

    def start_pair(blk, slot, w):
        word = tok_ref[blk * words + w]
        row_copy(word & 0xFFFF, slot, 2 * w).start()
        row_copy((word >> 16) & 0xFFFF, slot, 2 * w + 1).start()

    def start_gather_loop(blk, slot):
        def body(w, c):
            start_pair(blk, slot, w)
            return c
        lax.fori_loop(0, words, body, 0, unroll=4)

    def start_gather_inline(blk, slot):
        for w in range(words):
            start_pair(blk, slot, w)

    def wait_gather(slot):
        def body(r, c):
            row_copy(0, slot, r).wait()
            return c
        lax.fori_loop(0, MOE_TM, body, 0, unroll=8)

    def compute(slot):
        lo, hi = _unpack_bf16_pairs(xbuf[slot])
        x = jnp.concatenate([lo.astype(BF16), hi.astype(BF16)], axis=1)
        gate = jnp.dot(x, wg_ref[0].astype(BF16), preferred_element_type=F32)
        up = jnp.dot(x, wu_ref[0].astype(BF16), preferred_element_type=F32)
        hid_ref[...] = (gate * jax.nn.sigmoid(gate) * up).astype(hid_ref.dtype)

    @pl.when(b == 0)
    def _():
        start_gather_loop(0, 0)

    @pl.when(b + 1 < nv)
    def _():
        wait_gather(b % 2)
        start_gather_inline(b + 1, (b + 1) % 2)
        compute(b % 2)

    @pl.when(b + 1 == nv)
    def _():
        wait_gather(b % 2)
        compute(b % 2)

    @pl.when(b >= nv)
    def _():
        hid_ref[...] = jnp.zeros(hid_ref.shape, hid_ref.dtype)


def _expert_weight_copies(srcs, e, stage_ref, st, sem):
    copies = []
    for which, src in enumerate(srcs):
        rows = src.shape[1] // W_PARTS
        for p in range(W_PARTS):
            rs = pl.ds(p * rows, rows)
            copies.append(pltpu.make_async_copy(src.at[e, rs, :], stage_ref.at[st, which, rs, :], sem.at[st]))
    return copies


def _expert_weight_pipeline(b, first_ref, run_ref, rune_ref, srcs, stage_ref, bf16_refs, sem):
    @pl.when(b == 0)
    def _():
        for st in range(W_STAGES):
            @pl.when(rune_ref[st] < N_EXPERTS)
            def _(st=st):
                for c in _expert_weight_copies(srcs, rune_ref[st], stage_ref, st, sem):
                    c.start()

    @pl.when(first_ref[b] == 1)
    def _():
        k = run_ref[b]
        st = k % W_STAGES
        for c in _expert_weight_copies(srcs, 0, stage_ref, st, sem):
            c.wait()
        for which, dst in enumerate(bf16_refs):
            rows = dst.shape[0] // W_PARTS
            for p in range(W_PARTS):
                rs = slice(p * rows, (p + 1) * rows)
                dst[rs, :] = stage_ref[st, which, rs, :].astype(BF16)
        e_next = rune_ref[k + W_STAGES]

        @pl.when(e_next < N_EXPERTS)
        def _():
            for c in _expert_weight_copies(srcs, e_next, stage_ref, st, sem):
                c.start()


def _moe_down_kernel(first_ref, run_ref, rune_ref, nv_ref, hid_ref, rw_ref, wd_hbm, out_ref, wstage, wdb, wsem):
    b = pl.program_id(0)
    nv = nv_ref[0]
    _expert_weight_pipeline(b, first_ref, run_ref, rune_ref, (wd_hbm,), wstage, (wdb,), wsem)

    @pl.when(b < nv)
    def _():
        y = jnp.dot(hid_ref[...], wdb[...], preferred_element_type=F32)
        out_ref[...] = _pack_bf16_pairs(y * rw_ref[...])

    @pl.when(b >= nv)
    def _():
        out_ref[...] = jnp.zeros(out_ref.shape, out_ref.dtype)


def _moe_experts(h1p, tok_words, row_w, block_e, first, run_idx, run_e, n_valid, wg, wu, wd):
    rows = row_w.shape[0]
    nb = rows // MOE_TM
    hid = pl.pallas_call(
        _moe_up_kernel,
        grid_spec=pltpu.PrefetchScalarGridSpec(
            num_scalar_prefetch=3,
            grid=(nb,),
            in_specs=[pl.BlockSpec(memory_space=pl.ANY),
                      pl.BlockSpec((1, D_MODEL, D_EXPERT), lambda b, be, nv, tok: (be[b], 0, 0)),
                      pl.BlockSpec((1, D_MODEL, D_EXPERT), lambda b, be, nv, tok: (be[b], 0, 0))],
            out_specs=pl.BlockSpec((MOE_TM, D_EXPERT), lambda b, be, nv, tok: (b, 0)),
            scratch_shapes=[pltpu.VMEM((2, MOE_TM, D_MODEL // 2), jnp.uint32), pltpu.SemaphoreType.DMA((2,))]),
        out_shape=jax.ShapeDtypeStruct((rows, D_EXPERT), BF16),
        compiler_params=_params("arbitrary"),
        name="moe_up",
    )(block_e, n_valid, tok_words, h1p, wg, wu)
    return pl.pallas_call(
        _moe_down_kernel,
        grid_spec=pltpu.PrefetchScalarGridSpec(
            num_scalar_prefetch=4,
            grid=(nb,),
            in_specs=[pl.BlockSpec((MOE_TM, D_EXPERT), lambda b, *_: (b, 0)),
                      pl.BlockSpec((MOE_TM, 1), lambda b, *_: (b, 0)),
                      pl.BlockSpec(memory_space=pl.ANY)],
            out_specs=pl.BlockSpec((MOE_TM, D_MODEL // 2), lambda b, *_: (b, 0)),
            scratch_shapes=[pltpu.VMEM((W_STAGES, 1, D_EXPERT, D_MODEL), F32), pltpu.VMEM((D_EXPERT, D_MODEL), BF16),
                            pltpu.SemaphoreType.DMA((W_STAGES,))]),
        out_shape=jax.ShapeDtypeStruct((rows, D_MODEL // 2), jnp.uint32),
        compiler_params=_params("arbitrary"),
        name="moe_down",
    )(first, run_idx, run_e, n_valid, hid, row_w.reshape(rows, 1), wd)


FIN_TM = 256


def _final_kernel(pos_ref, h_ref, y_hbm, g_ref, b_ref, out_ref, ybuf, sem):
    i = pl.program_id(0)
    n = pl.num_programs(0)

    def row_copy(src_row, slot, r):
        return pltpu.make_async_copy(y_hbm.at[pl.ds(src_row, 1), :], ybuf.at[slot, pl.ds(r, 1), :], sem.at[slot])

    def start_token(tile, slot, r):
        word = pos_ref[tile * FIN_TM + r]
        row_copy(word & 0xFFFF, slot, r).start()
        row_copy((word >> 16) & 0xFFFF, slot, FIN_TM + r).start()

    def start_gather_loop(tile, slot):
        def body(r, c):
            start_token(tile, slot, r)
            return c
        lax.fori_loop(0, FIN_TM, body, 0, unroll=4)

    def start_gather_inline(tile, slot):
        for r in range(FIN_TM):
            start_token(tile, slot, r)

    def wait_gather(slot):
        def body(r, c):
            row_copy(0, slot, r).wait()
            return c
        lax.fori_loop(0, TOP_K * FIN_TM, body, 0, unroll=8)

    def compute(slot):
        ff_lo, ff_hi = _unpack_bf16_pairs(ybuf[slot, 0:FIN_TM, :])
        for k in range(1, TOP_K):
            lo, hi = _unpack_bf16_pairs(ybuf[slot, k * FIN_TM:(k + 1) * FIN_TM, :])
            ff_lo, ff_hi = ff_lo + lo, ff_hi + hi
        z = DEEPNORM_ALPHA * h_ref[...] + jnp.concatenate([ff_lo, ff_hi], axis=1)
        out_ref[...] = _layer_norm(z, g_ref[...], b_ref[...])

    @pl.when(i == 0)
    def _():
        start_gather_loop(0, 0)

    @pl.when(i + 1 < n)
    def _():
        wait_gather(i % 2)
        start_gather_inline(i + 1, (i + 1) % 2)
        compute(i % 2)

    @pl.when(i + 1 == n)
    def _():
        wait_gather(i % 2)
        compute(i % 2)


def _moe_combine_ln2(h1, moe_rows, pos_words, g, b):
    T = h1.shape[0]
    assert TOP_K == 2
    const = lambda i, pos: (0, 0)
    grid_spec = pltpu.PrefetchScalarGridSpec(
        num_scalar_prefetch=1,
        grid=(T // FIN_TM,),
        in_specs=[pl.BlockSpec((FIN_TM, D_MODEL), lambda i, pos: (i, 0)),
                  pl.BlockSpec(memory_space=pl.ANY),
                  pl.BlockSpec((1, D_MODEL), const), pl.BlockSpec((1, D_MODEL), const)],
        out_specs=pl.BlockSpec((FIN_TM, D_MODEL), lambda i, pos: (i, 0)),
        scratch_shapes=[pltpu.VMEM((2, TOP_K * FIN_TM, D_MODEL // 2), jnp.uint32), pltpu.SemaphoreType.DMA((2,))],
    )
    return pl.pallas_call(
        _final_kernel,
        grid_spec=grid_spec,
        out_shape=jax.ShapeDtypeStruct((T, D_MODEL), F32),
        compiler_params=_params("arbitrary"),
        name="moe_combine_ln2",
    )(pos_words, h1, moe_rows, g.reshape(1, -1), b.reshape(1, -1))


def _pack_u16_pairs(lo, hi):
    return (lo.astype(jnp.int32) | (hi.astype(jnp.int32) << 16)).astype(jnp.int32)


def _routing(logits):
    T = logits.shape[0]
    logits_g = logits[:, :N_GROUPS]
    logits_e = logits[:, N_GROUPS:N_GROUPS + N_EXPERTS].reshape(T, N_GROUPS, EXPERTS_PER_GROUP)
    p_g = jax.nn.softmax(logits_g, axis=-1)
    g_idx = jnp.argmax(logits_g, axis=-1)
    p_grp = jnp.take_along_axis(p_g, g_idx[:, None], axis=-1)
    le = jnp.take_along_axis(logits_e, g_idx[:, None, None], axis=1)[:, 0]
    top_v, top_i = lax.top_k(le, TOP_K)
    weight = p_grp * jax.nn.softmax(top_v, axis=-1)
    expert_id = g_idx[:, None].astype(jnp.int32) * EXPERTS_PER_GROUP + top_i.astype(jnp.int32)

    A = T * TOP_K
    flat_e = expert_id.reshape(-1)
    flat_w = weight.reshape(-1)
    experts = jnp.arange(N_EXPERTS, dtype=jnp.int32)
    counts = jnp.sum((flat_e[:, None] == experts[None, :]).astype(jnp.int32), axis=0)
    starts = jnp.cumsum(counts) - counts
    padded = (counts + MOE_TM - 1) // MOE_TM * MOE_TM
    pends = jnp.cumsum(padded)
    pstarts = pends - padded
    order = jnp.argsort(flat_e, stable=True).astype(jnp.int32)
    inv = jnp.argsort(order).astype(jnp.int32)
    pos = (inv + (pstarts - starts)[flat_e]).astype(jnp.int32)
    nb = A // MOE_TM + N_EXPERTS
    assert nb * MOE_TM < 2 ** 16 and T < 2 ** 16
    n_valid = (pends[-1] // MOE_TM).astype(jnp.int32)
    blk = jnp.arange(nb, dtype=jnp.int32)
    blk_start = jnp.minimum(blk, n_valid - 1) * MOE_TM
    block_e = jnp.minimum(jnp.sum((blk_start[:, None] >= pends[None, :]).astype(jnp.int32), axis=1), N_EXPERTS - 1)
    i_in = (blk * MOE_TM - pstarts[block_e])[:, None] + jnp.arange(MOE_TM, dtype=jnp.int32)[None, :]
    valid = i_in < counts[block_e][:, None]
    a_r = order[jnp.clip(starts[block_e][:, None] + i_in, 0, A - 1)]
    row_tok = jnp.where(valid, a_r // TOP_K, 0).astype(jnp.int32)
    row_w = jnp.where(valid, flat_w[a_r], 0.0).astype(F32).reshape(-1)
    tok_words = _pack_u16_pairs(row_tok[:, 0::2], row_tok[:, 1::2]).reshape(-1)
    pos2 = pos.reshape(T, TOP_K)
    pos_words = _pack_u16_pairs(pos2[:, 0], pos2[:, 1])
    prev_e = jnp.concatenate([jnp.full((1,), -1, jnp.int32), block_e[:-1]])
    first = jnp.logical_and(block_e != prev_e, blk < n_valid).astype(jnp.int32)
    run_idx = jnp.maximum(jnp.cumsum(first) - 1, 0).astype(jnp.int32)
    run_e = jnp.concatenate([jnp.sort(jnp.where(counts > 0, experts, N_EXPERTS)),
                             jnp.full((W_STAGES,), N_EXPERTS, jnp.int32)]).astype(jnp.int32)
    return tok_words, row_w, block_e, first, run_idx, run_e, n_valid.reshape(1), pos_words


def kernel(x, positions, w_in, b_gate, conv_w, conv_b, w_rg_a, b_rg_a, w_rg_x, b_rg_x, lru_lambda, w_attn_proj, w_rec_proj, w_out, ln1_g, ln1_b, w_router_group, b_router_group, w_router_expert, b_router_expert, w_gate, w_up, w_down, ln2_g, ln2_b):
    B, S, D = x.shape
    T = B * S
    h = x.reshape(T, D)
    c, sa, sb = _rope_tables(positions)
    for layer in range(DEPTH):
        xb = h.astype(BF16)
        os_, lses = [], []
        for g, (_, dilation) in enumerate(ATTN_GROUPS):
            qkv = _qkv_proj(xb, w_in[layer], c, sa, sb, g, dilation)
            o, lse = _attention_group(qkv, B, S, dilation)
            os_.append(o)
            lses.append(lse)
        rest = _rest_proj(xb, w_in[layer])
        rec = _rglru(rest, B, S, conv_w[layer], conv_b[layer], w_rg_a[layer].astype(BF16), b_rg_a[layer],
                     w_rg_x[layer].astype(BF16), b_rg_x[layer], lru_lambda[layer])
        merged = _merge(os_, lses, rec, rest, b_gate[layer], w_attn_proj[layer].astype(BF16),
                        w_rec_proj[layer].astype(BF16))
        w_router = jnp.concatenate([w_router_group[layer], w_router_expert[layer]], axis=1)
        w_router = jnp.pad(w_router, ((0, 0), (0, LANES - w_router.shape[1])))
        b_router = jnp.concatenate([b_router_group[layer], b_router_expert[layer]])
        b_router = jnp.pad(b_router, (0, LANES - b_router.shape[0])).reshape(1, LANES)
        mix = _out_matmul(merged, w_out[layer].astype(BF16))
        h1, h1p, logits = _ln1_router(mix, h, ln1_g[layer], ln1_b[layer], w_router.astype(BF16), b_router)
        tok_words, row_w, block_e, first, run_idx, run_e, n_valid, pos_words = _routing(logits)
        moe_rows = _moe_experts(h1p, tok_words, row_w, block_e, first, run_idx, run_e, n_valid,
                                w_gate[layer], w_up[layer], w_down[layer])
        h = _moe_combine_ln2(h1, moe_rows, pos_words, ln2_g[layer], ln2_b[layer])
    return h.reshape(B, S, D)
```

```python
import functools

import jax
import jax.numpy as jnp
from jax import lax
from jax.experimental import pallas as pl
from jax.experimental.pallas import tpu as pltpu

D_MODEL = 4096
HEAD_DIM = 128
ATTN_GROUPS = ((128, 1), (512, 4), (2048, 16))
HEADS_PER_GROUP = 8
N_ATTN_HEADS = HEADS_PER_GROUP * len(ATTN_GROUPS)
ATTN_WIDTH = N_ATTN_HEADS * HEAD_DIM
ATTN_OUT_WIDTH = HEADS_PER_GROUP * HEAD_DIM
ROT_DIM = HEAD_DIM // 4
ROPE_THETA = 500000.0
ATTN_SPAN = 128
LRU_WIDTH = 2048
LRU_BLOCKS = 8
LRU_BLOCK_W = LRU_WIDTH // LRU_BLOCKS
CONV_WIDTH = 4
RG_C = 8.0
N_GROUPS = 8
EXPERTS_PER_GROUP = 8
N_EXPERTS = N_GROUPS * EXPERTS_PER_GROUP
TOP_K = 2
D_EXPERT = 512
LN_EPS = 1e-5
DEPTH = 1
DEEPNORM_ALPHA = (2 * DEPTH) ** 0.25
REST_OFF = 3 * ATTN_WIDTH
REST_COLS = 2 * LRU_WIDTH + 2 * D_MODEL
RX_COL = 0
RG_COL = LRU_WIDTH
GATE_COL = 2 * LRU_WIDTH

LANES = 128
COL_TILE = 1024
PROJ_TM = 1024
PROJ_TN = 512
VMEM_LIMIT = 56 * 1024 * 1024

F32 = jnp.float32
BF16 = jnp.bfloat16


def _params(*sem, vmem=VMEM_LIMIT):
    return pltpu.CompilerParams(dimension_semantics=sem, vmem_limit_bytes=vmem)


def _rope_table_kernel(pos_ref, invf_ref, c_ref, sa_ref, sb_ref):
    half = ROT_DIM // 2
    ang = pos_ref[...].astype(F32) * invf_ref[...]
    lane = lax.broadcasted_iota(jnp.int32, ang.shape, 1)
    cos = jnp.cos(ang)
    sin = jnp.sin(ang)
    c_ref[...] = jnp.where(lane < ROT_DIM, cos, 1.0)
    sa_ref[...] = jnp.where(lane < half, -sin, 0.0)
    sb_ref[...] = jnp.where((lane >= half) & (lane < ROT_DIM), sin, 0.0)


def _rope_tables(positions):
    T = positions.size
    tm = 2048
    half = ROT_DIM // 2
    inv_freq = jnp.power(jnp.float32(ROPE_THETA), -jnp.arange(half, dtype=F32) * 2.0 / ROT_DIM)
    invf = jnp.zeros((1, LANES), F32).at[0, :ROT_DIM].set(jnp.concatenate([inv_freq, inv_freq]))
    tab = jax.ShapeDtypeStruct((T, LANES), F32)
    return pl.pallas_call(
        _rope_table_kernel,
        grid=(T // tm,),
        in_specs=[pl.BlockSpec((tm, 1), lambda i: (i, 0)), pl.BlockSpec((1, LANES), lambda i: (0, 0))],
        out_specs=[pl.BlockSpec((tm, LANES), lambda i: (i, 0))] * 3,
        out_shape=[tab, tab, tab],
        compiler_params=_params("arbitrary"),
        name="rope_tables",
    )(positions.reshape(T, 1), invf)


def _qkv_proj_kernel(x_ref, w_ref, c_ref, sa_ref, sb_ref, o_ref, wb_ref, slab_ref, *, d):
    n = pl.program_id(0)
    m = pl.program_id(1)
    tiles_per_sect = ATTN_OUT_WIDTH // PROJ_TN

    @pl.when(m == 0)
    def _():
        wb_ref[...] = w_ref[...].astype(BF16)

    sect = n // tiles_per_sect
    rot = sect < 2
    scale = jnp.where(sect == 0, HEAD_DIM ** -0.5, 1.0).astype(F32)
    c = jnp.where(rot, c_ref[...], 1.0)
    sa = jnp.where(rot, sa_ref[...], 0.0)
    sb = jnp.where(rot, sb_ref[...], 0.0)
    x = x_ref[...]
    per = PROJ_TM // d
    half_w = PROJ_TN // 2
    heads_per_half = half_w // HEAD_DIM
    for half in range(2):
        acc = jnp.dot(x, wb_ref[:, half * half_w:(half + 1) * half_w], preferred_element_type=F32)
        for hh in range(heads_per_half):
            h = half * heads_per_half + hh
            hs = slice(h * HEAD_DIM, (h + 1) * HEAD_DIM)
            t = acc[:, hh * HEAD_DIM:(hh + 1) * HEAD_DIM]
            u = pltpu.bitcast(t.astype(BF16), jnp.uint32)
            ta = pltpu.bitcast(pltpu.roll(u, HEAD_DIM - ROT_DIM // 2, 1), BF16).astype(F32)
            tb = pltpu.bitcast(pltpu.roll(u, ROT_DIM // 2, 1), BF16).astype(F32)
            r = (t * c + ta * sa + tb * sb) * scale
            if d == 1:
                o_ref[:, hs] = r.astype(o_ref.dtype)
            else:
                slab_ref[h] = r
                for res in range(d):
                    rows = slab_ref[h, pl.ds(res, per, stride=d), :]
                    o_ref[res * per:(res + 1) * per, hs] = rows.astype(o_ref.dtype)


def _qkv_proj(xb, w, c, sa, sb, g, d):
    T = xb.shape[0]
    tiles_per_sect = ATTN_OUT_WIDTH // PROJ_TN
    sect_tiles = ATTN_WIDTH // PROJ_TN
    tab_spec = pl.BlockSpec((PROJ_TM, LANES), lambda n, m: (m, 0))
    wcol = lambda n: (n // tiles_per_sect) * sect_tiles + g * tiles_per_sect + n % tiles_per_sect
    return pl.pallas_call(
        functools.partial(_qkv_proj_kernel, d=d),
        grid=(3 * tiles_per_sect, T // PROJ_TM),
        in_specs=[pl.BlockSpec((PROJ_TM, D_MODEL), lambda n, m: (m, 0)),
                  pl.BlockSpec((D_MODEL, PROJ_TN), lambda n, m: (0, wcol(n))),
                  tab_spec, tab_spec, tab_spec],
        out_specs=pl.BlockSpec((PROJ_TM, PROJ_TN), lambda n, m: (m, n)),
        out_shape=jax.ShapeDtypeStruct((T, 3 * ATTN_OUT_WIDTH), BF16),
        scratch_shapes=[pltpu.VMEM((D_MODEL, PROJ_TN), BF16),
                        pltpu.VMEM((PROJ_TN // HEAD_DIM, PROJ_TM, LANES), F32)],
        compiler_params=_params("arbitrary", "arbitrary"),
        name=f"qkv_proj_d{d}",
    )(xb, w, c, sa, sb)


def _rest_proj_kernel(xa_ref, xb_ref, w_ref, o_ref, wb_ref):
    @pl.when(pl.program_id(1) == 0)
    def _():
        wb_ref[...] = w_ref[...].astype(BF16)

    half = xa_ref.shape[0]
    o_ref[0:half, :] = jnp.dot(xa_ref[...], wb_ref[...], preferred_element_type=F32).astype(o_ref.dtype)
    o_ref[half:, :] = jnp.dot(xb_ref[...], wb_ref[...], preferred_element_type=F32).astype(o_ref.dtype)


def _rest_proj(xb, w):
    T = xb.shape[0]
    first = REST_OFF // PROJ_TN
    half = PROJ_TM // 2
    return pl.pallas_call(
        _rest_proj_kernel,
        grid=(REST_COLS // PROJ_TN, T // PROJ_TM),
        in_specs=[pl.BlockSpec((half, D_MODEL), lambda n, m: (2 * m, 0)),
                  pl.BlockSpec((half, D_MODEL), lambda n, m: (2 * m + 1, 0)),
                  pl.BlockSpec((D_MODEL, PROJ_TN), lambda n, m: (0, first + n))],
        out_specs=pl.BlockSpec((PROJ_TM, PROJ_TN), lambda n, m: (m, n)),
        out_shape=jax.ShapeDtypeStruct((T, REST_COLS), BF16),
        scratch_shapes=[pltpu.VMEM((D_MODEL, PROJ_TN), BF16)],
        compiler_params=_params("arbitrary", "arbitrary"),
        name="rest_proj",
    )(xb, xb, w)


def _attn_kernel(*refs, npc):
    q_refs = refs[:npc]
    kp_refs, kc_refs = refs[npc:2 * npc], refs[2 * npc:3 * npc]
    vp_refs, vc_refs = refs[3 * npc:4 * npc], refs[4 * npc:5 * npc]
    o_ref, lse_ref = refs[5 * npc:]
    n = pl.program_id(2)
    nh, blk = HEADS_PER_GROUP, ATTN_SPAN

    def head(pieces, h):
        hs = slice(h * HEAD_DIM, (h + 1) * HEAD_DIM)
        parts = [r[:, hs] for r in pieces]
        return parts[0] if len(parts) == 1 else jnp.concatenate(parts, axis=0)

    dn = (((1,), (1,)), ((), ()))
    qs = [head(q_refs, h) for h in range(nh)]
    sp = jnp.concatenate([lax.dot_general(qs[h], head(kp_refs, h), dn, preferred_element_type=F32)
                          for h in range(nh)], axis=0)
    sc = jnp.concatenate([lax.dot_general(qs[h], head(kc_refs, h), dn, preferred_element_type=F32)
                          for h in range(nh)], axis=0)
    row = lax.broadcasted_iota(jnp.int32, sp.shape, 0) & (blk - 1)
    col = lax.broadcasted_iota(jnp.int32, sp.shape, 1)
    sp = jnp.where(jnp.logical_and(col >= row, n > 0), sp, -jnp.inf)
    sc = jnp.where(col <= row, sc, -jnp.inf)
    m = jnp.maximum(jnp.max(sp, axis=1, keepdims=True), jnp.max(sc, axis=1, keepdims=True))
    p = jnp.concatenate([jnp.exp(sp - m).astype(BF16), jnp.exp(sc - m).astype(BF16)], axis=1)
    ones = jnp.ones((2 * blk, LANES), BF16)
    lane = lax.broadcasted_iota(jnp.int32, (blk, LANES), 1)
    lse_all = jnp.zeros((blk, LANES), F32)
    outs = []
    for h in range(nh):
        rs = slice(h * blk, (h + 1) * blk)
        v_ext = jnp.concatenate([jnp.concatenate([head(vp_refs, h), head(vc_refs, h)], axis=0), ones], axis=1)
        res = jnp.dot(p[rs], v_ext, preferred_element_type=F32)
        l = res[:, LANES:]
        outs.append((res[:, :LANES] * (1.0 / l)).astype(o_ref.dtype))
        lse_all = jnp.where(lane == h, m[rs] + jnp.log(l), lse_all)
    o_ref[0] = jnp.concatenate(outs, axis=1)
    lse_ref[0] = lse_all


def _attention_group(qkv, batch, seq, dilation):
    L = seq // dilation
    per = PROJ_TM // dilation
    pr = min(per, ATTN_SPAN)
    npc = ATTN_SPAN // pr
    nblk = L // ATTN_SPAN
    tiles_per_batch = seq // PROJ_TM
    qcol, kcol, vcol = 0, 1, 2

    def piece_block(b, res, n, p):
        pos = n * ATTN_SPAN + p * pr
        row = (b * tiles_per_batch + pos // per) * PROJ_TM + res * per + pos % per
        return row // pr

    def specs(col, back):
        return [pl.BlockSpec((pr, COL_TILE),
                             lambda b, r, n, p=p: (piece_block(b, r, jnp.maximum(n - back, 0), p), col))
                for p in range(npc)]

    in_specs = specs(qcol, 0) + specs(kcol, 1) + specs(kcol, 0) + specs(vcol, 1) + specs(vcol, 0)
    o, lse = pl.pallas_call(
        functools.partial(_attn_kernel, npc=npc),
        grid=(batch, dilation, nblk),
        in_specs=in_specs,
        out_specs=[pl.BlockSpec((1, ATTN_SPAN, ATTN_OUT_WIDTH), lambda b, r, n: (b, n, r)),
                   pl.BlockSpec((1, ATTN_SPAN, LANES), lambda b, r, n: (b, n, r))],
        out_shape=[jax.ShapeDtypeStruct((batch, L, dilation * ATTN_OUT_WIDTH), BF16),
                   jax.ShapeDtypeStruct((batch, L, dilation * LANES), F32)],
        compiler_params=_params("parallel", "parallel", "arbitrary"),
        name=f"attn_d{dilation}",
    )(*([qkv] * (5 * npc)))
    return o.reshape(batch * seq, ATTN_OUT_WIDTH), lse.reshape(batch * seq, LANES)


_TAIL = 8


def _rglru_kernel(rx_ref, rg_ref, cw_ref, cb_ref, wa_ref, ba_ref, wx_ref, bx_ref, lam_ref, o_ref,
                  xbuf, a_buf, h_buf, carry):
    tt = rx_ref.shape[0]
    t_idx = pl.program_id(2)

    @pl.when(t_idx == 0)
    def _():
        xbuf[0:_TAIL, :] = jnp.zeros((_TAIL, xbuf.shape[1]), F32)
        carry[...] = jnp.zeros(carry.shape, F32)

    x = rx_ref[...].astype(F32)
    xbuf[_TAIL:_TAIL + tt, :] = x
    xr = cb_ref[...] + cw_ref[0:1, :] * xbuf[_TAIL - 3:_TAIL - 3 + tt, :]
    for j in range(1, CONV_WIDTH):
        xr = xr + cw_ref[j:j + 1, :] * xbuf[_TAIL - 3 + j:_TAIL - 3 + j + tt, :]
    xbuf[0:_TAIL, :] = x[tt - _TAIL:tt, :]

    softplus_neg_lam = jnp.log1p(jnp.exp(-lam_ref[...]))
    for blk in range(xr.shape[1] // LRU_BLOCK_W):
        cs = slice(blk * LRU_BLOCK_W, (blk + 1) * LRU_BLOCK_W)
        xb = xr[:, cs]
        xb16 = xb.astype(BF16)
        r = jax.nn.sigmoid(jnp.dot(xb16, wa_ref[blk], preferred_element_type=F32) + ba_ref[:, cs])
        i = jax.nn.sigmoid(jnp.dot(xb16, wx_ref[blk], preferred_element_type=F32) + bx_ref[:, cs])
        log_a = (-RG_C) * r * softplus_neg_lam[:, cs]
        a = jnp.exp(log_a)
        a_buf[:, cs] = a
        h_buf[:, cs] = jnp.sqrt(1.0 - a * a) * (i * xb)

    def step(t, h):
        h = a_buf[pl.ds(t, 1), :] * h + h_buf[pl.ds(t, 1), :]
        h_buf[pl.ds(t, 1), :] = h
        return h

    carry[...] = lax.fori_loop(0, tt, step, carry[...], unroll=8)
    o_ref[...] = (h_buf[...] * jax.nn.gelu(rg_ref[...].astype(F32))).astype(o_ref.dtype)


def _rglru(rest, batch, seq, conv_w, conv_b, wa, ba, wx, bx, lam):
    tt = 512
    nt = seq // tt
    hw = COL_TILE
    assert hw % LRU_BLOCK_W == 0 and RX_COL % hw == 0 and RG_COL % hw == 0
    gb = hw // LRU_BLOCK_W
    row = lambda off: pl.BlockSpec((tt, hw), lambda b, c, t: (b * nt + t, off // hw + c))
    vec = pl.BlockSpec((1, hw), lambda b, c, t: (0, c))
    wspec = pl.BlockSpec((gb, LRU_BLOCK_W, LRU_BLOCK_W), lambda b, c, t: (c, 0, 0))
    return pl.pallas_call(
        _rglru_kernel,
        grid=(batch, LRU_WIDTH // hw, nt),
        in_specs=[row(RX_COL), row(RG_COL), pl.BlockSpec((CONV_WIDTH, hw), lambda b, c, t: (0, c)), vec,
                  wspec, vec, wspec, vec, vec],
        out_specs=pl.BlockSpec((tt, hw), lambda b, c, t: (b * nt + t, c)),
        out_shape=jax.ShapeDtypeStruct((batch * seq, LRU_WIDTH), BF16),
        scratch_shapes=[pltpu.VMEM((tt + _TAIL, hw), F32), pltpu.VMEM((tt, hw), F32),
                        pltpu.VMEM((tt, hw), F32), pltpu.VMEM((1, hw), F32)],
        compiler_params=_params("parallel", "parallel", "arbitrary"),
        name="rg_lru",
    )(rest, rest, conv_w, conv_b.reshape(1, -1), wa, ba.reshape(1, -1), wx, bx.reshape(1, -1), lam.reshape(1, -1))


def _merge_kernel(*refs):
    o0_ref, o1_ref, o2_ref, l0_ref, l1_ref, l2_ref, rec_ref = refs[:7]
    nct = D_MODEL // COL_TILE
    ga_refs = refs[7:7 + nct]
    gr_refs = refs[7 + nct:7 + 2 * nct]
    bg_ref, wa_ref, wr_ref, out_ref = refs[7 + 2 * nct:]
    l0, l1, l2 = l0_ref[...], l1_ref[...], l2_ref[...]
    parts = []
    for h in range(HEADS_PER_GROUP):
        hs = slice(h * HEAD_DIM, (h + 1) * HEAD_DIM)
        a0, a1, a2 = l0[:, h:h + 1], l1[:, h:h + 1], l2[:, h:h + 1]
        m = jnp.maximum(jnp.maximum(a0, a1), a2)
        e0, e1, e2 = jnp.exp(a0 - m), jnp.exp(a1 - m), jnp.exp(a2 - m)
        inv = 1.0 / (e0 + e1 + e2)
        mixed = ((e0 * inv) * o0_ref[:, hs].astype(F32) + (e1 * inv) * o1_ref[:, hs].astype(F32)
                 + (e2 * inv) * o2_ref[:, hs].astype(F32))
        parts.append(mixed.astype(BF16))
    attn = jnp.concatenate(parts, axis=1)
    rec = rec_ref[...]
    for j in range(nct):
        cs = slice(j * COL_TILE, (j + 1) * COL_TILE)
        ya = jnp.dot(attn, wa_ref[:, cs], preferred_element_type=F32)
        yr = jnp.dot(rec, wr_ref[:, cs], preferred_element_type=F32)
        gate_a = jax.nn.sigmoid(ga_refs[j][...].astype(F32) + bg_ref[0:1, cs])
        gate_r = jax.nn.sigmoid(gr_refs[j][...].astype(F32) + bg_ref[1:2, cs])
        out_ref[:, cs] = (gate_a * ya + gate_r * yr).astype(out_ref.dtype)


def _merge(os_, lses, rec, rest, b_gate, wa, wr):
    T = rec.shape[0]
    tm = 256
    nct = D_MODEL // COL_TILE
    row = lambda m: (m, 0)
    const = lambda m: (0, 0)
    gate_specs = [pl.BlockSpec((tm, COL_TILE), lambda m, c=(GATE_COL + k * D_MODEL) // COL_TILE + j: (m, c))
                  for k in range(2) for j in range(nct)]
    o_spec = pl.BlockSpec((tm, ATTN_OUT_WIDTH), row)
    l_spec = pl.BlockSpec((tm, LANES), row)
    resident = lambda shape: pl.BlockSpec(shape, const, pipeline_mode=pl.Buffered(1))
    return pl.pallas_call(
        _merge_kernel,
        grid=(T // tm,),
        in_specs=[o_spec, o_spec, o_spec, l_spec, l_spec, l_spec, pl.BlockSpec((tm, LRU_WIDTH), row)]
                 + gate_specs
                 + [resident((2, D_MODEL)), resident((ATTN_OUT_WIDTH, D_MODEL)), resident((LRU_WIDTH, D_MODEL))],
        out_specs=pl.BlockSpec((tm, D_MODEL), row),
        out_shape=jax.ShapeDtypeStruct((T, D_MODEL), BF16),
        compiler_params=_params("parallel"),
        name="merge",
    )(*os_, *lses, rec, *([rest] * (2 * nct)), b_gate, wa, wr)


def _layer_norm(z, g, b):
    mu = jnp.mean(z, axis=-1, keepdims=True)
    zc = z - mu
    var = jnp.mean(zc * zc, axis=-1, keepdims=True)
    return zc * lax.rsqrt(var + LN_EPS) * g + b


_HI_MASK = 0xFFFF0000


def _pack_bf16_pairs(y):
    bits = lax.bitcast_convert_type(y.astype(BF16).astype(F32), jnp.uint32)
    c = y.shape[1] // 2
    return (bits[:, :c] >> 16) | (bits[:, c:] & jnp.uint32(_HI_MASK))


def _unpack_bf16_pairs(p):
    lo = lax.bitcast_convert_type(p << 16, F32)
    hi = lax.bitcast_convert_type(p & jnp.uint32(_HI_MASK), F32)
    return lo, hi


def _out_matmul_kernel(mg_ref, w_ref, o_ref):
    o_ref[...] = jnp.dot(mg_ref[...], w_ref[...], preferred_element_type=F32).astype(o_ref.dtype)


def _out_matmul(merged, w_out):
    T = merged.shape[0]
    tm, tn = 1024, 1024
    return pl.pallas_call(
        _out_matmul_kernel,
        grid=(T // tm, D_MODEL // tn),
        in_specs=[pl.BlockSpec((tm, D_MODEL), lambda m, n: (m, 0)),
                  pl.BlockSpec((D_MODEL, tn), lambda m, n: (0, n))],
        out_specs=pl.BlockSpec((tm, tn), lambda m, n: (m, n)),
        out_shape=jax.ShapeDtypeStruct((T, D_MODEL), BF16),
        compiler_params=_params("parallel", "arbitrary"),
        name="out_matmul",
    )(merged, w_out)


def _ln1_kernel(mix_ref, x_ref, g_ref, b_ref, wr_ref, br_ref, h_ref, hp_ref, lg_ref):
    z = DEEPNORM_ALPHA * x_ref[...] + mix_ref[...].astype(F32)
    y = _layer_norm(z, g_ref[...], b_ref[...])
    h_ref[...] = y
    hp_ref[...] = _pack_bf16_pairs(y)
    lg_ref[...] = jnp.dot(y.astype(BF16), wr_ref[...], preferred_element_type=F32) + br_ref[...]


def _ln1_router(mix, x2, g, b, wr, br):
    T = mix.shape[0]
    tm = 256
    row = lambda m: (m, 0)
    const = lambda m: (0, 0)
    return pl.pallas_call(
        _ln1_kernel,
        grid=(T // tm,),
        in_specs=[pl.BlockSpec((tm, D_MODEL), row), pl.BlockSpec((tm, D_MODEL), row),
                  pl.BlockSpec((1, D_MODEL), const), pl.BlockSpec((1, D_MODEL), const),
                  pl.BlockSpec((D_MODEL, LANES), const), pl.BlockSpec((1, LANES), const)],
        out_specs=[pl.BlockSpec((tm, D_MODEL), row), pl.BlockSpec((tm, D_MODEL // 2), row),
                   pl.BlockSpec((tm, LANES), row)],
        out_shape=[jax.ShapeDtypeStruct((T, D_MODEL), F32), jax.ShapeDtypeStruct((T, D_MODEL // 2), jnp.uint32),
                   jax.ShapeDtypeStruct((T, LANES), F32)],
        compiler_params=_params("parallel"),
        name="ln1_router",
    )(mix, x2, g.reshape(1, -1), b.reshape(1, -1), wr, br)


MOE_TM = 256
W_PARTS = 4
W_STAGES = 2
GATHER_DMA_QUEUE = 1


def _moe_up_kernel(be_ref, nv_ref, tok_ref, h_hbm, wg_ref, wu_ref, hid_ref, xbuf, sem):
    b = pl.program_id(0)
    nv = nv_ref[0]
    words = MOE_TM // 2

    def row_copy(tok, slot, r):
        return pltpu.make_async_copy(h_hbm.at[pl.ds(tok, 1), :], xbuf.at[slot, pl.ds(r, 1), :], sem.at[slot])

    def start_pair(blk, slot, w):
        word = tok_ref[blk * words + w]
        row_copy(word & 0xFFFF, slot, 2 * w).start(priority=GATHER_DMA_QUEUE)
        row_copy((word >> 16) & 0xFFFF, slot, 2 * w + 1).start(priority=GATHER_DMA_QUEUE)

    def start_gather_loop(blk, slot):
        def body(w, c):
            start_pair(blk, slot, w)
            return c
        lax.fori_loop(0, words, body, 0, unroll=4)

    def start_gather_inline(blk, slot):
        for w in range(words):
            start_pair(blk, slot, w)

    def wait_gather(slot):
        def body(r, c):
            row_copy(0, slot, r).wait()
            return c
        lax.fori_loop(0, MOE_TM, body, 0, unroll=8)

    def compute(slot):
        lo, hi = _unpack_bf16_pairs(xbuf[slot])
        x = jnp.concatenate([lo.astype(BF16), hi.astype(BF16)], axis=1)
        gate = jnp.dot(x, wg_ref[0].astype(BF16), preferred_element_type=F32)
        up = jnp.dot(x, wu_ref[0].astype(BF16), preferred_element_type=F32)
        hid_ref[...] = (gate * jax.nn.sigmoid(gate) * up).astype(hid_ref.dtype)

    @pl.when(b == 0)
    def _():
        start_gather_loop(0, 0)

    @pl.when(b + 1 < nv)
    def _():
        wait_gather(b % 2)
        start_gather_inline(b + 1, (b + 1) % 2)
        compute(b % 2)

    @pl.when(b + 1 == nv)
    def _():
        wait_gather(b % 2)
        compute(b % 2)

    @pl.when(b >= nv)
    def _():
        hid_ref[...] = jnp.zeros(hid_ref.shape, hid_ref.dtype)


def _expert_weight_copies(srcs, e, stage_ref, st, sem):
    copies = []
    for which, src in enumerate(srcs):
        rows = src.shape[1] // W_PARTS
        for p in range(W_PARTS):
            rs = pl.ds(p * rows, rows)
            copies.append(pltpu.make_async_copy(src.at[e, rs, :], stage_ref.at[st, which, rs, :], sem.at[st]))
    return copies


def _expert_weight_pipeline(b, first_ref, run_ref, rune_ref, srcs, stage_ref, bf16_refs, sem):
    @pl.when(b == 0)
    def _():
        for st in range(W_STAGES):
            @pl.when(rune_ref[st] < N_EXPERTS)
            def _(st=st):
                for c in _expert_weight_copies(srcs, rune_ref[st], stage_ref, st, sem):
                    c.start()

    @pl.when(first_ref[b] == 1)
    def _():
        k = run_ref[b]
        st = k % W_STAGES
        for c in _expert_weight_copies(srcs, 0, stage_ref, st, sem):
            c.wait()
        for which, dst in enumerate(bf16_refs):
            rows = dst.shape[0] // W_PARTS
            for p in range(W_PARTS):
                rs = slice(p * rows, (p + 1) * rows)
                dst[rs, :] = stage_ref[st, which, rs, :].astype(BF16)
        e_next = rune_ref[k + W_STAGES]

        @pl.when(e_next < N_EXPERTS)
        def _():
            for c in _expert_weight_copies(srcs, e_next, stage_ref, st, sem):
                c.start()


def _moe_down_kernel(first_ref, run_ref, rune_ref, nv_ref, hid_ref, rw_ref, wd_hbm, out_ref, wstage, wdb, wsem):
    b = pl.program_id(0)
    nv = nv_ref[0]
    _expert_weight_pipeline(b, first_ref, run_ref, rune_ref, (wd_hbm,), wstage, (wdb,), wsem)

    @pl.when(b < nv)
    def _():
        y = jnp.dot(hid_ref[...], wdb[...], preferred_element_type=F32)
        out_ref[...] = _pack_bf16_pairs(y * rw_ref[...])

    @pl.when(b >= nv)
    def _():
        out_ref[...] = jnp.zeros(out_ref.shape, out_ref.dtype)


def _moe_experts(h1p, tok_words, row_w, block_e, first, run_idx, run_e, n_valid, wg, wu, wd):
    rows = row_w.shape[0]
    nb = rows // MOE_TM
    hid = pl.pallas_call(
        _moe_up_kernel,
        grid_spec=pltpu.PrefetchScalarGridSpec(
            num_scalar_prefetch=3,
            grid=(nb,),
            in_specs=[pl.BlockSpec(memory_space=pl.ANY),
                      pl.BlockSpec((1, D_MODEL, D_EXPERT), lambda b, be, nv, tok: (be[b], 0, 0)),
                      pl.BlockSpec((1, D_MODEL, D_EXPERT), lambda b, be, nv, tok: (be[b], 0, 0))],
            out_specs=pl.BlockSpec((MOE_TM, D_EXPERT), lambda b, be, nv, tok: (b, 0)),
            scratch_shapes=[pltpu.VMEM((2, MOE_TM, D_MODEL // 2), jnp.uint32), pltpu.SemaphoreType.DMA((2,))]),
        out_shape=jax.ShapeDtypeStruct((rows, D_EXPERT), BF16),
        compiler_params=_params("arbitrary"),
        name="moe_up",
    )(block_e, n_valid, tok_words, h1p, wg, wu)
    return pl.pallas_call(
        _moe_down_kernel,
        grid_spec=pltpu.PrefetchScalarGridSpec(
            num_scalar_prefetch=4,
            grid=(nb,),
            in_specs=[pl.BlockSpec((MOE_TM, D_EXPERT), lambda b, *_: (b, 0)),
                      pl.BlockSpec((MOE_TM, 1), lambda b, *_: (b, 0)),
                      pl.BlockSpec(memory_space=pl.ANY)],
            out_specs=pl.BlockSpec((MOE_TM, D_MODEL // 2), lambda b, *_: (b, 0)),
            scratch_shapes=[pltpu.VMEM((W_STAGES, 1, D_EXPERT, D_MODEL), F32), pltpu.VMEM((D_EXPERT, D_MODEL), BF16),
                            pltpu.SemaphoreType.DMA((W_STAGES,))]),
        out_shape=jax.ShapeDtypeStruct((rows, D_MODEL // 2), jnp.uint32),
        compiler_params=_params("arbitrary"),
        name="moe_down",
    )(first, run_idx, run_e, n_valid, hid, row_w.reshape(rows, 1), wd)


FIN_TM = 256


def _final_kernel(pos_ref, h_ref, y_hbm, g_ref, b_ref, out_ref, ybuf, sem):
    i = pl.program_id(0)
    n = pl.num_programs(0)

    def row_copy(src_row, slot, r):
        return pltpu.make_async_copy(y_hbm.at[pl.ds(src_row, 1), :], ybuf.at[slot, pl.ds(r, 1), :], sem.at[slot])

    def start_token(tile, slot, r):
        word = pos_ref[tile * FIN_TM + r]
        row_copy(word & 0xFFFF, slot, r).start()
        row_copy((word >> 16) & 0xFFFF, slot, FIN_TM + r).start()

    def start_gather_loop(tile, slot):
        def body(r, c):
            start_token(tile, slot, r)
            return c
        lax.fori_loop(0, FIN_TM, body, 0, unroll=4)

    def start_gather_inline(tile, slot):
        for r in range(FIN_TM):
            start_token(tile, slot, r)

    def wait_gather(slot):
        def body(r, c):
            row_copy(0, slot, r).wait()
            return c
        lax.fori_loop(0, TOP_K * FIN_TM, body, 0, unroll=8)

    def compute(slot):
        ff_lo, ff_hi = _unpack_bf16_pairs(ybuf[slot, 0:FIN_TM, :])
        for k in range(1, TOP_K):
            lo, hi = _unpack_bf16_pairs(ybuf[slot, k * FIN_TM:(k + 1) * FIN_TM, :])
            ff_lo, ff_hi = ff_lo + lo, ff_hi + hi
        z = DEEPNORM_ALPHA * h_ref[...] + jnp.concatenate([ff_lo, ff_hi], axis=1)
        out_ref[...] = _layer_norm(z, g_ref[...], b_ref[...])

    @pl.when(i == 0)
    def _():
        start_gather_loop(0, 0)

    @pl.when(i + 1 < n)
    def _():
        wait_gather(i % 2)
        start_gather_inline(i + 1, (i + 1) % 2)
        compute(i % 2)

    @pl.when(i + 1 == n)
    def _():
        wait_gather(i % 2)
        compute(i % 2)


def _moe_combine_ln2(h1, moe_rows, pos_words, g, b):
    T = h1.shape[0]
    assert TOP_K == 2
    const = lambda i, pos: (0, 0)
    grid_spec = pltpu.PrefetchScalarGridSpec(
        num_scalar_prefetch=1,
        grid=(T // FIN_TM,),
        in_specs=[pl.BlockSpec((FIN_TM, D_MODEL), lambda i, pos: (i, 0)),
                  pl.BlockSpec(memory_space=pl.ANY),
                  pl.BlockSpec((1, D_MODEL), const), pl.BlockSpec((1, D_MODEL), const)],
        out_specs=pl.BlockSpec((FIN_TM, D_MODEL), lambda i, pos: (i, 0)),
        scratch_shapes=[pltpu.VMEM((2, TOP_K * FIN_TM, D_MODEL // 2), jnp.uint32), pltpu.SemaphoreType.DMA((2,))],
    )
    return pl.pallas_call(
        _final_kernel,
        grid_spec=grid_spec,
        out_shape=jax.ShapeDtypeStruct((T, D_MODEL), F32),
        compiler_params=_params("arbitrary"),
        name="moe_combine_ln2",
    )(pos_words, h1, moe_rows, g.reshape(1, -1), b.reshape(1, -1))


def _pack_u16_pairs(lo, hi):
    return (lo.astype(jnp.int32) | (hi.astype(jnp.int32) << 16)).astype(jnp.int32)


def _routing(logits):
    T = logits.shape[0]
    logits_g = logits[:, :N_GROUPS]
    logits_e = logits[:, N_GROUPS:N_GROUPS + N_EXPERTS].reshape(T, N_GROUPS, EXPERTS_PER_GROUP)
    p_g = jax.nn.softmax(logits_g, axis=-1)
    g_idx = jnp.argmax(logits_g, axis=-1)
    p_grp = jnp.take_along_axis(p_g, g_idx[:, None], axis=-1)
    le = jnp.take_along_axis(logits_e, g_idx[:, None, None], axis=1)[:, 0]
    top_v, top_i = lax.top_k(le, TOP_K)
    weight = p_grp * jax.nn.softmax(top_v, axis=-1)
    expert_id = g_idx[:, None].astype(jnp.int32) * EXPERTS_PER_GROUP + top_i.astype(jnp.int32)

    A = T * TOP_K
    flat_e = expert_id.reshape(-1)
    flat_w = weight.reshape(-1)
    experts = jnp.arange(N_EXPERTS, dtype=jnp.int32)
    counts = jnp.sum((flat_e[:, None] == experts[None, :]).astype(jnp.int32), axis=0)
    starts = jnp.cumsum(counts) - counts
    padded = (counts + MOE_TM - 1) // MOE_TM * MOE_TM
    pends = jnp.cumsum(padded)
    pstarts = pends - padded
    order = jnp.argsort(flat_e, stable=True).astype(jnp.int32)
    inv = jnp.argsort(order).astype(jnp.int32)
    pos = (inv + (pstarts - starts)[flat_e]).astype(jnp.int32)
    nb = A // MOE_TM + N_EXPERTS
    assert nb * MOE_TM < 2 ** 16 and T < 2 ** 16
    n_valid = (pends[-1] // MOE_TM).astype(jnp.int32)
    blk = jnp.arange(nb, dtype=jnp.int32)
    blk_start = jnp.minimum(blk, n_valid - 1) * MOE_TM
    block_e = jnp.minimum(jnp.sum((blk_start[:, None] >= pends[None, :]).astype(jnp.int32), axis=1), N_EXPERTS - 1)
    i_in = (blk * MOE_TM - pstarts[block_e])[:, None] + jnp.arange(MOE_TM, dtype=jnp.int32)[None, :]
    valid = i_in < counts[block_e][:, None]
    a_r = order[jnp.clip(starts[block_e][:, None] + i_in, 0, A - 1)]
    row_tok = jnp.where(valid, a_r // TOP_K, 0).astype(jnp.int32)
    row_w = jnp.where(valid, flat_w[a_r], 0.0).astype(F32).reshape(-1)
    tok_words = _pack_u16_pairs(row_tok[:, 0::2], row_tok[:, 1::2]).reshape(-1)
    pos2 = pos.reshape(T, TOP_K)
    pos_words = _pack_u16_pairs(pos2[:, 0], pos2[:, 1])
    prev_e = jnp.concatenate([jnp.full((1,), -1, jnp.int32), block_e[:-1]])
    first = jnp.logical_and(block_e != prev_e, blk < n_valid).astype(jnp.int32)
    run_idx = jnp.maximum(jnp.cumsum(first) - 1, 0).astype(jnp.int32)
    run_e = jnp.concatenate([jnp.sort(jnp.where(counts > 0, experts, N_EXPERTS)),
                             jnp.full((W_STAGES,), N_EXPERTS, jnp.int32)]).astype(jnp.int32)
    return tok_words, row_w, block_e, first, run_idx, run_e, n_valid.reshape(1), pos_words


def kernel(x, positions, w_in, b_gate, conv_w, conv_b, w_rg_a, b_rg_a, w_rg_x, b_rg_x, lru_lambda, w_attn_proj, w_rec_proj, w_out, ln1_g, ln1_b, w_router_group, b_router_group, w_router_expert, b_router_expert, w_gate, w_up, w_down, ln2_g, ln2_b):
    B, S, D = x.shape
    T = B * S
    h = x.reshape(T, D)
    c, sa, sb = _rope_tables(positions)
    for layer in range(DEPTH):
        xb = h.astype(BF16)
        os_, lses = [], []
        for g, (_, dilation) in enumerate(ATTN_GROUPS):
            qkv = _qkv_proj(xb, w_in[layer], c, sa, sb, g, dilation)
            o, lse = _attention_group(qkv, B, S, dilation)
            os_.append(o)
            lses.append(lse)
        rest = _rest_proj(xb, w_in[layer])
        rec = _rglru(rest, B, S, conv_w[layer], conv_b[layer], w_rg_a[layer].astype(BF16), b_rg_a[layer],
                     w_rg_x[layer].astype(BF16), b_rg_x[layer], lru_lambda[layer])
        merged = _merge(os_, lses, rec, rest, b_gate[layer], w_attn_proj[layer].astype(BF16),
                        w_rec_proj[layer].astype(BF16))
        w_router = jnp.concatenate([w_router_group[layer], w_router_expert[layer]], axis=1)
        w_router = jnp.pad(w_router, ((0, 0), (0, LANES - w_router.shape[1])))
        b_router = jnp.concatenate([b_router_group[layer], b_router_expert[layer]])
        b_router = jnp.pad(b_router, (0, LANES - b_router.shape[0])).reshape(1, LANES)
        mix = _out_matmul(merged, w_out[layer].astype(BF16))
        h1, h1p, logits = _ln1_router(mix, h, ln1_g[layer], ln1_b[layer], w_router.astype(BF16), b_router)
        tok_words, row_w, block_e, first, run_idx, run_e, n_valid, pos_words = _routing(logits)
        moe_rows = _moe_experts(h1p, tok_words, row_w, block_e, first, run_idx, run_e, n_valid,
                                w_gate[layer], w_up[layer], w_down[layer])
        h = _moe_combine_ln2(h1, moe_rows, pos_words, ln2_g[layer], ln2_b[layer])
    return h.reshape(B, S, D)
```

```python
import functools

import jax
import jax.numpy as jnp
from jax import lax
from jax.experimental import pallas as pl
from jax.experimental.pallas import tpu as pltpu

D_MODEL = 4096
HEAD_DIM = 128
ATTN_GROUPS = ((128, 1), (512, 4), (2048, 16))
HEADS_PER_GROUP = 8
N_ATTN_HEADS = HEADS_PER_GROUP * len(ATTN_GROUPS)
ATTN_WIDTH = N_ATTN_HEADS * HEAD_DIM
ATTN_OUT_WIDTH = HEADS_PER_GROUP * HEAD_DIM
ROT_DIM = HEAD_DIM // 4
ROPE_THETA = 500000.0
ATTN_SPAN = 128
LRU_WIDTH = 2048
LRU_BLOCKS = 8
LRU_BLOCK_W = LRU_WIDTH // LRU_BLOCKS
CONV_WIDTH = 4
RG_C = 8.0
N_GROUPS = 8
EXPERTS_PER_GROUP = 8
N_EXPERTS = N_GROUPS * EXPERTS_PER_GROUP
TOP_K = 2
D_EXPERT = 512
LN_EPS = 1e-5
DEPTH = 1
DEEPNORM_ALPHA = (2 * DEPTH) ** 0.25
REST_OFF = 3 * ATTN_WIDTH
REST_COLS = 2 * LRU_WIDTH + 2 * D_MODEL
RX_COL = 0
RG_COL = LRU_WIDTH
GATE_COL = 2 * LRU_WIDTH

LANES = 128
COL_TILE = 1024
PROJ_TM = 1024
PROJ_TN = 512
VMEM_LIMIT = 56 * 1024 * 1024

F32 = jnp.float32
BF16 = jnp.bfloat16


def _params(*sem, vmem=VMEM_LIMIT):
    return pltpu.CompilerParams(dimension_semantics=sem, vmem_limit_bytes=vmem)


def _rope_table_kernel(pos_ref, invf_ref, c_ref, sa_ref, sb_ref):
    half = ROT_DIM // 2
    ang = pos_ref[...].astype(F32) * invf_ref[...]
    lane = lax.broadcasted_iota(jnp.int32, ang.shape, 1)
    cos = jnp.cos(ang)
    sin = jnp.sin(ang)
    c_ref[...] = jnp.where(lane < ROT_DIM, cos, 1.0)
    sa_ref[...] = jnp.where(lane < half, -sin, 0.0)
    sb_ref[...] = jnp.where((lane >= half) & (lane < ROT_DIM), sin, 0.0)


def _rope_tables(positions):
    T = positions.size
    tm = 2048
    half = ROT_DIM // 2
    inv_freq = jnp.power(jnp.float32(ROPE_THETA), -jnp.arange(half, dtype=F32) * 2.0 / ROT_DIM)
    invf = jnp.zeros((1, LANES), F32).at[0, :ROT_DIM].set(jnp.concatenate([inv_freq, inv_freq]))
    tab = jax.ShapeDtypeStruct((T, LANES), F32)
    return pl.pallas_call(
        _rope_table_kernel,
        grid=(T // tm,),
        in_specs=[pl.BlockSpec((tm, 1), lambda i: (i, 0)), pl.BlockSpec((1, LANES), lambda i: (0, 0))],
        out_specs=[pl.BlockSpec((tm, LANES), lambda i: (i, 0))] * 3,
        out_shape=[tab, tab, tab],
        compiler_params=_params("arbitrary"),
        name="rope_tables",
    )(positions.reshape(T, 1), invf)


def _qkv_proj_kernel(x_ref, w_ref, c_ref, sa_ref, sb_ref, o_ref, wb_ref, slab_ref, *, d):
    n = pl.program_id(0)
    m = pl.program_id(1)
    tiles_per_sect = ATTN_OUT_WIDTH // PROJ_TN

    @pl.when(m == 0)
    def _():
        wb_ref[...] = w_ref[...].astype(BF16)

    sect = n // tiles_per_sect
    rot = sect < 2
    scale = jnp.where(sect == 0, HEAD_DIM ** -0.5, 1.0).astype(F32)
    c = jnp.where(rot, c_ref[...], 1.0)
    sa = jnp.where(rot, sa_ref[...], 0.0)
    sb = jnp.where(rot, sb_ref[...], 0.0)
    x = x_ref[...]
    per = PROJ_TM // d
    half_w = PROJ_TN // 2
    heads_per_half = half_w // HEAD_DIM
    for half in range(2):
        acc = jnp.dot(x, wb_ref[:, half * half_w:(half + 1) * half_w], preferred_element_type=F32)
        for hh in range(heads_per_half):
            h = half * heads_per_half + hh
            hs = slice(h * HEAD_DIM, (h + 1) * HEAD_DIM)
            t = acc[:, hh * HEAD_DIM:(hh + 1) * HEAD_DIM]
            u = pltpu.bitcast(t.astype(BF16), jnp.uint32)
            ta = pltpu.bitcast(pltpu.roll(u, HEAD_DIM - ROT_DIM // 2, 1), BF16).astype(F32)
            tb = pltpu.bitcast(pltpu.roll(u, ROT_DIM // 2, 1), BF16).astype(F32)
            r = (t * c + ta * sa + tb * sb) * scale
            if d == 1:
                o_ref[:, hs] = r.astype(o_ref.dtype)
            else:
                slab_ref[h] = r
                for res in range(d):
                    rows = slab_ref[h, pl.ds(res, per, stride=d), :]
                    o_ref[res * per:(res + 1) * per, hs] = rows.astype(o_ref.dtype)


def _qkv_proj(xb, w, c, sa, sb, g, d):
    T = xb.shape[0]
    tiles_per_sect = ATTN_OUT_WIDTH // PROJ_TN
    sect_tiles = ATTN_WIDTH // PROJ_TN
    tab_spec = pl.BlockSpec((PROJ_TM, LANES), lambda n, m: (m, 0))
    wcol = lambda n: (n // tiles_per_sect) * sect_tiles + g * tiles_per_sect + n % tiles_per_sect
    return pl.pallas_call(
        functools.partial(_qkv_proj_kernel, d=d),
        grid=(3 * tiles_per_sect, T // PROJ_TM),
        in_specs=[pl.BlockSpec((PROJ_TM, D_MODEL), lambda n, m: (m, 0)),
                  pl.BlockSpec((D_MODEL, PROJ_TN), lambda n, m: (0, wcol(n))),
                  tab_spec, tab_spec, tab_spec],
        out_specs=pl.BlockSpec((PROJ_TM, PROJ_TN), lambda n, m: (m, n)),
        out_shape=jax.ShapeDtypeStruct((T, 3 * ATTN_OUT_WIDTH), BF16),
        scratch_shapes=[pltpu.VMEM((D_MODEL, PROJ_TN), BF16),
                        pltpu.VMEM((PROJ_TN // HEAD_DIM, PROJ_TM, LANES), F32)],
        compiler_params=_params("arbitrary", "arbitrary"),
        name=f"qkv_proj_d{d}",
    )(xb, w, c, sa, sb)


def _rest_proj_kernel(x_ref, w_ref, o_ref, wb_ref):
    @pl.when(pl.program_id(1) == 0)
    def _():
        wb_ref[...] = w_ref[...].astype(BF16)

    o_ref[...] = jnp.dot(x_ref[...], wb_ref[...], preferred_element_type=F32).astype(o_ref.dtype)


def _rest_proj(xb, w):
    T = xb.shape[0]
    first = REST_OFF // PROJ_TN
    return pl.pallas_call(
        _rest_proj_kernel,
        grid=(REST_COLS // PROJ_TN, T // PROJ_TM),
        in_specs=[pl.BlockSpec((PROJ_TM, D_MODEL), lambda n, m: (m, 0)),
                  pl.BlockSpec((D_MODEL, PROJ_TN), lambda n, m: (0, first + n))],
        out_specs=pl.BlockSpec((PROJ_TM, PROJ_TN), lambda n, m: (m, n)),
        out_shape=jax.ShapeDtypeStruct((T, REST_COLS), BF16),
        scratch_shapes=[pltpu.VMEM((D_MODEL, PROJ_TN), BF16)],
        compiler_params=_params("arbitrary", "arbitrary"),
        name="rest_proj",
    )(xb, w)


def _attn_kernel(*refs, npc, d):
    q_refs = refs[:npc]
    kp_refs, kc_refs = refs[npc:2 * npc], refs[2 * npc:3 * npc]
    vp_refs, vc_refs = refs[3 * npc:4 * npc], refs[4 * npc:5 * npc]
    o_ref, lse_ref = refs[5 * npc:]
    n = pl.program_id(1)
    res_class = pl.program_id(2)
    nh, blk = HEADS_PER_GROUP, ATTN_SPAN
    rows_out = slice(None) if d == 1 else pl.ds(res_class, blk, stride=d)

    def head(pieces, h):
        hs = slice(h * HEAD_DIM, (h + 1) * HEAD_DIM)
        parts = [r[:, hs] for r in pieces]
        return parts[0] if len(parts) == 1 else jnp.concatenate(parts, axis=0)

    dn = (((1,), (1,)), ((), ()))
    qs = [head(q_refs, h) for h in range(nh)]
    sp = jnp.concatenate([lax.dot_general(qs[h], head(kp_refs, h), dn, preferred_element_type=F32)
                          for h in range(nh)], axis=0)
    sc = jnp.concatenate([lax.dot_general(qs[h], head(kc_refs, h), dn, preferred_element_type=F32)
                          for h in range(nh)], axis=0)
    row = lax.broadcasted_iota(jnp.int32, sp.shape, 0) & (blk - 1)
    col = lax.broadcasted_iota(jnp.int32, sp.shape, 1)
    sp = jnp.where(jnp.logical_and(col >= row, n > 0), sp, -jnp.inf)
    sc = jnp.where(col <= row, sc, -jnp.inf)
    m = jnp.maximum(jnp.max(sp, axis=1, keepdims=True), jnp.max(sc, axis=1, keepdims=True))
    p = jnp.concatenate([jnp.exp(sp - m).astype(BF16), jnp.exp(sc - m).astype(BF16)], axis=1)
    ones = jnp.ones((2 * blk, LANES), BF16)
    lane = lax.broadcasted_iota(jnp.int32, (blk, LANES), 1)
    lse_all = jnp.zeros((blk, LANES), F32)
    for h in range(nh):
        rs = slice(h * blk, (h + 1) * blk)
        v_ext = jnp.concatenate([jnp.concatenate([head(vp_refs, h), head(vc_refs, h)], axis=0), ones], axis=1)
        res = jnp.dot(p[rs], v_ext, preferred_element_type=F32)
        l = res[:, LANES:]
        o_ref[h, rows_out, :] = res[:, :LANES] * (1.0 / l)
        lse_all = jnp.where(lane == h, m[rs] + jnp.log(l), lse_all)
    lse_ref[rows_out, :] = lse_all


def _attention_group(qkv, batch, seq, dilation):
    L = seq // dilation
    per = PROJ_TM // dilation
    pr = min(per, ATTN_SPAN)
    npc = ATTN_SPAN // pr
    nblk = L // ATTN_SPAN
    tiles_per_batch = seq // PROJ_TM
    qcol, kcol, vcol = 0, 1, 2

    def piece_block(b, res, n, p):
        pos = n * ATTN_SPAN + p * pr
        row = (b * tiles_per_batch + pos // per) * PROJ_TM + res * per + pos % per
        return row // pr

    def specs(col, back):
        return [pl.BlockSpec((pr, COL_TILE),
                             lambda b, n, r, p=p: (piece_block(b, r, jnp.maximum(n - back, 0), p), col))
                for p in range(npc)]

    in_specs = specs(qcol, 0) + specs(kcol, 1) + specs(kcol, 0) + specs(vcol, 1) + specs(vcol, 0)
    tok_rows = ATTN_SPAN * dilation
    return pl.pallas_call(
        functools.partial(_attn_kernel, npc=npc, d=dilation),
        grid=(batch, nblk, dilation),
        in_specs=in_specs,
        out_specs=[pl.BlockSpec((HEADS_PER_GROUP, tok_rows, HEAD_DIM), lambda b, n, r: (0, b * nblk + n, 0)),
                   pl.BlockSpec((tok_rows, LANES), lambda b, n, r: (b * nblk + n, 0))],
        out_shape=[jax.ShapeDtypeStruct((HEADS_PER_GROUP, batch * seq, HEAD_DIM), F32),
                   jax.ShapeDtypeStruct((batch * seq, LANES), F32)],
        compiler_params=_params("parallel", "arbitrary", "arbitrary"),
        name=f"attn_d{dilation}",
    )(*([qkv] * (5 * npc)))


_TAIL = 8


def _rglru_kernel(rx_ref, rg_ref, cw_ref, cb_ref, wa_ref, ba_ref, wx_ref, bx_ref, lam_ref, o_ref,
                  xbuf, a_buf, h_buf, carry):
    tt = rx_ref.shape[0]
    t_idx = pl.program_id(2)

    @pl.when(t_idx == 0)
    def _():
        xbuf[0:_TAIL, :] = jnp.zeros((_TAIL, xbuf.shape[1]), F32)
        carry[...] = jnp.zeros(carry.shape, F32)

    x = rx_ref[...].astype(F32)
    xbuf[_TAIL:_TAIL + tt, :] = x
    xr = cb_ref[...] + cw_ref[0:1, :] * xbuf[_TAIL - 3:_TAIL - 3 + tt, :]
    for j in range(1, CONV_WIDTH):
        xr = xr + cw_ref[j:j + 1, :] * xbuf[_TAIL - 3 + j:_TAIL - 3 + j + tt, :]
    xbuf[0:_TAIL, :] = x[tt - _TAIL:tt, :]

    softplus_neg_lam = jnp.log1p(jnp.exp(-lam_ref[...]))
    for blk in range(xr.shape[1] // LRU_BLOCK_W):
        cs = slice(blk * LRU_BLOCK_W, (blk + 1) * LRU_BLOCK_W)
        xb = xr[:, cs]
        xb16 = xb.astype(BF16)
        r = jax.nn.sigmoid(jnp.dot(xb16, wa_ref[blk], preferred_element_type=F32) + ba_ref[:, cs])
        i = jax.nn.sigmoid(jnp.dot(xb16, wx_ref[blk], preferred_element_type=F32) + bx_ref[:, cs])
        log_a = (-RG_C) * r * softplus_neg_lam[:, cs]
        a = jnp.exp(log_a)
        a_buf[:, cs] = a
        h_buf[:, cs] = jnp.sqrt(1.0 - a * a) * (i * xb)

    def step(t, h):
        h = a_buf[pl.ds(t, 1), :] * h + h_buf[pl.ds(t, 1), :]
        h_buf[pl.ds(t, 1), :] = h
        return h

    carry[...] = lax.fori_loop(0, tt, step, carry[...], unroll=8)
    o_ref[...] = (h_buf[...] * jax.nn.gelu(rg_ref[...].astype(F32))).astype(o_ref.dtype)


def _rglru(rest, batch, seq, conv_w, conv_b, wa, ba, wx, bx, lam):
    tt = 512
    nt = seq // tt
    hw = COL_TILE
    assert hw % LRU_BLOCK_W == 0 and RX_COL % hw == 0 and RG_COL % hw == 0
    gb = hw // LRU_BLOCK_W
    row = lambda off: pl.BlockSpec((tt, hw), lambda b, c, t: (b * nt + t, off // hw + c))
    vec = pl.BlockSpec((1, hw), lambda b, c, t: (0, c))
    wspec = pl.BlockSpec((gb, LRU_BLOCK_W, LRU_BLOCK_W), lambda b, c, t: (c, 0, 0))
    return pl.pallas_call(
        _rglru_kernel,
        grid=(batch, LRU_WIDTH // hw, nt),
        in_specs=[row(RX_COL), row(RG_COL), pl.BlockSpec((CONV_WIDTH, hw), lambda b, c, t: (0, c)), vec,
                  wspec, vec, wspec, vec, vec],
        out_specs=pl.BlockSpec((tt, hw), lambda b, c, t: (b * nt + t, c)),
        out_shape=jax.ShapeDtypeStruct((batch * seq, LRU_WIDTH), BF16),
        scratch_shapes=[pltpu.VMEM((tt + _TAIL, hw), F32), pltpu.VMEM((tt, hw), F32),
                        pltpu.VMEM((tt, hw), F32), pltpu.VMEM((1, hw), F32)],
        compiler_params=_params("parallel", "parallel", "arbitrary"),
        name="rg_lru",
    )(rest, rest, conv_w, conv_b.reshape(1, -1), wa, ba.reshape(1, -1), wx, bx.reshape(1, -1), lam.reshape(1, -1))


def _merge_kernel(*refs):
    o0_ref, o1_ref, o2_ref, l0_ref, l1_ref, l2_ref, rec_ref = refs[:7]
    nct = D_MODEL // COL_TILE
    ga_refs = refs[7:7 + nct]
    gr_refs = refs[7 + nct:7 + 2 * nct]
    bg_ref, wa_ref, wr_ref, out_ref = refs[7 + 2 * nct:]
    l0, l1, l2 = l0_ref[...], l1_ref[...], l2_ref[...]
    parts = []
    for h in range(HEADS_PER_GROUP):
        a0, a1, a2 = l0[:, h:h + 1], l1[:, h:h + 1], l2[:, h:h + 1]
        m = jnp.maximum(jnp.maximum(a0, a1), a2)
        e0, e1, e2 = jnp.exp(a0 - m), jnp.exp(a1 - m), jnp.exp(a2 - m)
        inv = 1.0 / (e0 + e1 + e2)
        mixed = (e0 * inv) * o0_ref[h] + (e1 * inv) * o1_ref[h] + (e2 * inv) * o2_ref[h]
        parts.append(mixed.astype(BF16))
    attn = jnp.concatenate(parts, axis=1)
    rec = rec_ref[...]
    for j in range(nct):
        cs = slice(j * COL_TILE, (j + 1) * COL_TILE)
        ya = jnp.dot(attn, wa_ref[:, cs], preferred_element_type=F32)
        yr = jnp.dot(rec, wr_ref[:, cs], preferred_element_type=F32)
        gate_a = jax.nn.sigmoid(ga_refs[j][...].astype(F32) + bg_ref[0:1, cs])
        gate_r = jax.nn.sigmoid(gr_refs[j][...].astype(F32) + bg_ref[1:2, cs])
        out_ref[:, cs] = (gate_a * ya + gate_r * yr).astype(out_ref.dtype)


def _merge(os_, lses, rec, rest, b_gate, wa, wr):
    T = rec.shape[0]
    tm = 256
    nct = D_MODEL // COL_TILE
    row = lambda m: (m, 0)
    const = lambda m: (0, 0)
    gate_specs = [pl.BlockSpec((tm, COL_TILE), lambda m, c=(GATE_COL + k * D_MODEL) // COL_TILE + j: (m, c))
                  for k in range(2) for j in range(nct)]
    o_spec = pl.BlockSpec((HEADS_PER_GROUP, tm, HEAD_DIM), lambda m: (0, m, 0))
    l_spec = pl.BlockSpec((tm, LANES), row)
    resident = lambda shape: pl.BlockSpec(shape, const, pipeline_mode=pl.Buffered(1))
    return pl.pallas_call(
        _merge_kernel,
        grid=(T // tm,),
        in_specs=[o_spec, o_spec, o_spec, l_spec, l_spec, l_spec, pl.BlockSpec((tm, LRU_WIDTH), row)]
                 + gate_specs
                 + [resident((2, D_MODEL)), resident((ATTN_OUT_WIDTH, D_MODEL)), resident((LRU_WIDTH, D_MODEL))],
        out_specs=pl.BlockSpec((tm, D_MODEL), row),
        out_shape=jax.ShapeDtypeStruct((T, D_MODEL), BF16),
        compiler_params=_params("parallel"),
        name="merge",
    )(*os_, *lses, rec, *([rest] * (2 * nct)), b_gate, wa, wr)


def _layer_norm(z, g, b):
    mu = jnp.mean(z, axis=-1, keepdims=True)
    zc = z - mu
    var = jnp.mean(zc * zc, axis=-1, keepdims=True)
    return zc * lax.rsqrt(var + LN_EPS) * g + b


_HI_MASK = 0xFFFF0000


def _pack_bf16_pairs(y):
    bits = lax.bitcast_convert_type(y.astype(BF16).astype(F32), jnp.uint32)
    c = y.shape[1] // 2
    return (bits[:, :c] >> 16) | (bits[:, c:] & jnp.uint32(_HI_MASK))


def _unpack_bf16_pairs(p):
    lo = lax.bitcast_convert_type(p << 16, F32)
    hi = lax.bitcast_convert_type(p & jnp.uint32(_HI_MASK), F32)
    return lo, hi


def _out_matmul_kernel(mg_ref, w_ref, o_ref):
    o_ref[...] = jnp.dot(mg_ref[...], w_ref[...], preferred_element_type=F32).astype(o_ref.dtype)


def _out_matmul(merged, w_out):
    T = merged.shape[0]
    tm, tn = 1024, 1024
    return pl.pallas_call(
        _out_matmul_kernel,
        grid=(T // tm, D_MODEL // tn),
        in_specs=[pl.BlockSpec((tm, D_MODEL), lambda m, n: (m, 0)),
                  pl.BlockSpec((D_MODEL, tn), lambda m, n: (0, n))],
        out_specs=pl.BlockSpec((tm, tn), lambda m, n: (m, n)),
        out_shape=jax.ShapeDtypeStruct((T, D_MODEL), BF16),
        compiler_params=_params("parallel", "arbitrary"),
        name="out_matmul",
    )(merged, w_out)


def _ln1_kernel(mix_ref, x_ref, g_ref, b_ref, wr_ref, br_ref, h_ref, hp_ref, lg_ref):
    z = DEEPNORM_ALPHA * x_ref[...] + mix_ref[...].astype(F32)
    y = _layer_norm(z, g_ref[...], b_ref[...])
    h_ref[...] = y
    hp_ref[...] = _pack_bf16_pairs(y)
    lg_ref[...] = jnp.dot(y.astype(BF16), wr_ref[...], preferred_element_type=F32) + br_ref[...]


def _ln1_router(mix, x2, g, b, wr, br):
    T = mix.shape[0]
    tm = 256
    row = lambda m: (m, 0)
    const = lambda m: (0, 0)
    return pl.pallas_call(
        _ln1_kernel,
        grid=(T // tm,),
        in_specs=[pl.BlockSpec((tm, D_MODEL), row), pl.BlockSpec((tm, D_MODEL), row),
                  pl.BlockSpec((1, D_MODEL), const), pl.BlockSpec((1, D_MODEL), const),
                  pl.BlockSpec((D_MODEL, LANES), const), pl.BlockSpec((1, LANES), const)],
        out_specs=[pl.BlockSpec((tm, D_MODEL), row), pl.BlockSpec((tm, D_MODEL // 2), row),
                   pl.BlockSpec((tm, LANES), row)],
        out_shape=[jax.ShapeDtypeStruct((T, D_MODEL), F32), jax.ShapeDtypeStruct((T, D_MODEL // 2), jnp.uint32),
                   jax.ShapeDtypeStruct((T, LANES), F32)],
        compiler_params=_params("parallel"),
        name="ln1_router",
    )(mix, x2, g.reshape(1, -1), b.reshape(1, -1), wr, br)


MOE_TM = 256
W_PARTS = 4
W_STAGES = 2


def _moe_up_kernel(be_ref, nv_ref, tok_ref, h_hbm, wg_ref, wu_ref, hid_ref, xbuf, sem):
    b = pl.program_id(0)
    nv = nv_ref[0]
    words = MOE_TM // 2

    def row_copy(tok, slot, r):
        return pltpu.make_async_copy(h_hbm.at[pl.ds(tok, 1), :], xbuf.at[slot, pl.ds(r, 1), :], sem.at[slot])

    def start_pair(blk, slot, w):
        word = tok_ref[blk * words + w]
        row_copy(word & 0xFFFF, slot, 2 * w).start()
        row_copy((word >> 16) & 0xFFFF, slot, 2 * w + 1).start()

    def start_gather_loop(blk, slot):
        def body(w, c):
            start_pair(blk, slot, w)
            return c
        lax.fori_loop(0, words, body, 0, unroll=4)

    def start_gather_inline(blk, slot):
        for w in range(words):
            start_pair(blk, slot, w)

    def wait_gather(slot):
        def body(r, c):
            row_copy(0, slot, r).wait()
            return c
        lax.fori_loop(0, MOE_TM, body, 0, unroll=8)

    def compute(slot):
        lo, hi = _unpack_bf16_pairs(xbuf[slot])
        x = jnp.concatenate([lo.astype(BF16), hi.astype(BF16)], axis=1)
        gate = jnp.dot(x, wg_ref[0].astype(BF16), preferred_element_type=F32)
        up = jnp.dot(x, wu_ref[0].astype(BF16), preferred_element_type=F32)
        hid_ref[...] = (gate * jax.nn.sigmoid(gate) * up).astype(hid_ref.dtype)

    @pl.when(b == 0)
    def _():
        start_gather_loop(0, 0)

    @pl.when(b + 1 < nv)
    def _():
        wait_gather(b % 2)
        start_gather_inline(b + 1, (b + 1) % 2)
        compute(b % 2)

    @pl.when(b + 1 == nv)
    def _():
        wait_gather(b % 2)
        compute(b % 2)

    @pl.when(b >= nv)
    def _():
        hid_ref[...] = jnp.zeros(hid_ref.shape, hid_ref.dtype)


def _expert_weight_copies(srcs, e, stage_ref, st, sem):
    copies = []
    for which, src in enumerate(srcs):
        rows = src.shape[1] // W_PARTS
        for p in range(W_PARTS):
            rs = pl.ds(p * rows, rows)
            copies.append(pltpu.make_async_copy(src.at[e, rs, :], stage_ref.at[st, which, rs, :], sem.at[st]))
    return copies


def _expert_weight_pipeline(b, first_ref, run_ref, rune_ref, srcs, stage_ref, bf16_refs, sem):
    @pl.when(b == 0)
    def _():
        for st in range(W_STAGES):
            @pl.when(rune_ref[st] < N_EXPERTS)
            def _(st=st):
                for c in _expert_weight_copies(srcs, rune_ref[st], stage_ref, st, sem):
                    c.start()

    @pl.when(first_ref[b] == 1)
    def _():
        k = run_ref[b]
        st = k % W_STAGES
        for c in _expert_weight_copies(srcs, 0, stage_ref, st, sem):
            c.wait()
        for which, dst in enumerate(bf16_refs):
            rows = dst.shape[0] // W_PARTS
            for p in range(W_PARTS):
                rs = slice(p * rows, (p + 1) * rows)
                dst[rs, :] = stage_ref[st, which, rs, :].astype(BF16)
        e_next = rune_ref[k + W_STAGES]

        @pl.when(e_next < N_EXPERTS)
        def _():
            for c in _expert_weight_copies(srcs, e_next, stage_ref, st, sem):
                c.start()


def _moe_down_kernel(first_ref, run_ref, rune_ref, nv_ref, hid_ref, rw_ref, wd_hbm, out_ref, wstage, wdb, wsem):
    b = pl.program_id(0)
    nv = nv_ref[0]
    _expert_weight_pipeline(b, first_ref, run_ref, rune_ref, (wd_hbm,), wstage, (wdb,), wsem)

    @pl.when(b < nv)
    def _():
        y = jnp.dot(hid_ref[...], wdb[...], preferred_element_type=F32)
        out_ref[...] = _pack_bf16_pairs(y * rw_ref[...])

    @pl.when(b >= nv)
    def _():
        out_ref[...] = jnp.zeros(out_ref.shape, out_ref.dtype)


def _moe_experts(h1p, tok_words, row_w, block_e, first, run_idx, run_e, n_valid, wg, wu, wd):
    rows = row_w.shape[0]
    nb = rows // MOE_TM
    hid = pl.pallas_call(
        _moe_up_kernel,
        grid_spec=pltpu.PrefetchScalarGridSpec(
            num_scalar_prefetch=3,
            grid=(nb,),
            in_specs=[pl.BlockSpec(memory_space=pl.ANY),
                      pl.BlockSpec((1, D_MODEL, D_EXPERT), lambda b, be, nv, tok: (be[b], 0, 0)),
                      pl.BlockSpec((1, D_MODEL, D_EXPERT), lambda b, be, nv, tok: (be[b], 0, 0))],
            out_specs=pl.BlockSpec((MOE_TM, D_EXPERT), lambda b, be, nv, tok: (b, 0)),
            scratch_shapes=[pltpu.VMEM((2, MOE_TM, D_MODEL // 2), jnp.uint32), pltpu.SemaphoreType.DMA((2,))]),
        out_shape=jax.ShapeDtypeStruct((rows, D_EXPERT), BF16),
        compiler_params=_params("arbitrary"),
        name="moe_up",
    )(block_e, n_valid, tok_words, h1p, wg, wu)
    return pl.pallas_call(
        _moe_down_kernel,
        grid_spec=pltpu.PrefetchScalarGridSpec(
            num_scalar_prefetch=4,
            grid=(nb,),
            in_specs=[pl.BlockSpec((MOE_TM, D_EXPERT), lambda b, *_: (b, 0)),
                      pl.BlockSpec((MOE_TM, 1), lambda b, *_: (b, 0)),
                      pl.BlockSpec(memory_space=pl.ANY)],
            out_specs=pl.BlockSpec((MOE_TM, D_MODEL // 2), lambda b, *_: (b, 0)),
            scratch_shapes=[pltpu.VMEM((W_STAGES, 1, D_EXPERT, D_MODEL), F32), pltpu.VMEM((D_EXPERT, D_MODEL), BF16),
                            pltpu.SemaphoreType.DMA((W_STAGES,))]),
        out_shape=jax.ShapeDtypeStruct((rows, D_MODEL // 2), jnp.uint32),
        compiler_params=_params("arbitrary"),
        name="moe_down",
    )(first, run_idx, run_e, n_valid, hid, row_w.reshape(rows, 1), wd)


FIN_TM = 256


def _final_kernel(pos_ref, h_ref, y_hbm, g_ref, b_ref, out_ref, ybuf, sem):
    i = pl.program_id(0)
    n = pl.num_programs(0)

    def row_copy(src_row, slot, r):
        return pltpu.make_async_copy(y_hbm.at[pl.ds(src_row, 1), :], ybuf.at[slot, pl.ds(r, 1), :], sem.at[slot])

    def start_token(tile, slot, r):
        word = pos_ref[tile * FIN_TM + r]
        row_copy(word & 0xFFFF, slot, r).start()
        row_copy((word >> 16) & 0xFFFF, slot, FIN_TM + r).start()

    def start_gather_loop(tile, slot):
        def body(r, c):
            start_token(tile, slot, r)
            return c
        lax.fori_loop(0, FIN_TM, body, 0, unroll=4)

    def start_gather_inline(tile, slot):
        for r in range(FIN_TM):
            start_token(tile, slot, r)

    def wait_gather(slot):
        def body(r, c):
            row_copy(0, slot, r).wait()
            return c
        lax.fori_loop(0, TOP_K * FIN_TM, body, 0, unroll=8)

    def compute(slot):
        ff_lo, ff_hi = _unpack_bf16_pairs(ybuf[slot, 0:FIN_TM, :])
        for k in range(1, TOP_K):
            lo, hi = _unpack_bf16_pairs(ybuf[slot, k * FIN_TM:(k + 1) * FIN_TM, :])
            ff_lo, ff_hi = ff_lo + lo, ff_hi + hi
        z = DEEPNORM_ALPHA * h_ref[...] + jnp.concatenate([ff_lo, ff_hi], axis=1)
        out_ref[...] = _layer_norm(z, g_ref[...], b_ref[...])

    @pl.when(i == 0)
    def _():
        start_gather_loop(0, 0)

    @pl.when(i + 1 < n)
    def _():
        wait_gather(i % 2)
        start_gather_inline(i + 1, (i + 1) % 2)
        compute(i % 2)

    @pl.when(i + 1 == n)
    def _():
        wait_gather(i % 2)
        compute(i % 2)


def _moe_combine_ln2(h1, moe_rows, pos_words, g, b):
    T = h1.shape[0]
    assert TOP_K == 2
    const = lambda i, pos: (0, 0)
    grid_spec = pltpu.PrefetchScalarGridSpec(
        num_scalar_prefetch=1,
        grid=(T // FIN_TM,),
        in_specs=[pl.BlockSpec((FIN_TM, D_MODEL), lambda i, pos: (i, 0)),
                  pl.BlockSpec(memory_space=pl.ANY),
                  pl.BlockSpec((1, D_MODEL), const), pl.BlockSpec((1, D_MODEL), const)],
        out_specs=pl.BlockSpec((FIN_TM, D_MODEL), lambda i, pos: (i, 0)),
        scratch_shapes=[pltpu.VMEM((2, TOP_K * FIN_TM, D_MODEL // 2), jnp.uint32), pltpu.SemaphoreType.DMA((2,))],
    )
    return pl.pallas_call(
        _final_kernel,
        grid_spec=grid_spec,
        out_shape=jax.ShapeDtypeStruct((T, D_MODEL), F32),
        compiler_params=_params("arbitrary"),
        name="moe_combine_ln2",
    )(pos_words, h1, moe_rows, g.reshape(1, -1), b.reshape(1, -1))


def _pack_u16_pairs(lo, hi):
    return (lo.astype(jnp.int32) | (hi.astype(jnp.int32) << 16)).astype(jnp.int32)


def _routing(logits):
    T = logits.shape[0]
    logits_g = logits[:, :N_GROUPS]
    logits_e = logits[:, N_GROUPS:N_GROUPS + N_EXPERTS].reshape(T, N_GROUPS, EXPERTS_PER_GROUP)
    p_g = jax.nn.softmax(logits_g, axis=-1)
    g_idx = jnp.argmax(logits_g, axis=-1)
    p_grp = jnp.take_along_axis(p_g, g_idx[:, None], axis=-1)
    le = jnp.take_along_axis(logits_e, g_idx[:, None, None], axis=1)[:, 0]
    top_v, top_i = lax.top_k(le, TOP_K)
    weight = p_grp * jax.nn.softmax(top_v, axis=-1)
    expert_id = g_idx[:, None].astype(jnp.int32) * EXPERTS_PER_GROUP + top_i.astype(jnp.int32)

    A = T * TOP_K
    flat_e = expert_id.reshape(-1)
    flat_w = weight.reshape(-1)
    experts = jnp.arange(N_EXPERTS, dtype=jnp.int32)
    counts = jnp.sum((flat_e[:, None] == experts[None, :]).astype(jnp.int32), axis=0)
    starts = jnp.cumsum(counts) - counts
    padded = (counts + MOE_TM - 1) // MOE_TM * MOE_TM
    pends = jnp.cumsum(padded)
    pstarts = pends - padded
    order = jnp.argsort(flat_e, stable=True).astype(jnp.int32)
    inv = jnp.argsort(order).astype(jnp.int32)
    pos = (inv + (pstarts - starts)[flat_e]).astype(jnp.int32)
    nb = A // MOE_TM + N_EXPERTS
    assert nb * MOE_TM < 2 ** 16 and T < 2 ** 16
    n_valid = (pends[-1] // MOE_TM).astype(jnp.int32)
    blk = jnp.arange(nb, dtype=jnp.int32)
    blk_start = jnp.minimum(blk, n_valid - 1) * MOE_TM
    block_e = jnp.minimum(jnp.sum((blk_start[:, None] >= pends[None, :]).astype(jnp.int32), axis=1), N_EXPERTS - 1)
    i_in = (blk * MOE_TM - pstarts[block_e])[:, None] + jnp.arange(MOE_TM, dtype=jnp.int32)[None, :]
    valid = i_in < counts[block_e][:, None]
    a_r = order[jnp.clip(starts[block_e][:, None] + i_in, 0, A - 1)]
    row_tok = jnp.where(valid, a_r // TOP_K, 0).astype(jnp.int32)
    row_w = jnp.where(valid, flat_w[a_r], 0.0).astype(F32).reshape(-1)
    tok_words = _pack_u16_pairs(row_tok[:, 0::2], row_tok[:, 1::2]).reshape(-1)
    pos2 = pos.reshape(T, TOP_K)
    pos_words = _pack_u16_pairs(pos2[:, 0], pos2[:, 1])
    prev_e = jnp.concatenate([jnp.full((1,), -1, jnp.int32), block_e[:-1]])
    first = jnp.logical_and(block_e != prev_e, blk < n_valid).astype(jnp.int32)
    run_idx = jnp.maximum(jnp.cumsum(first) - 1, 0).astype(jnp.int32)
    run_e = jnp.concatenate([jnp.sort(jnp.where(counts > 0, experts, N_EXPERTS)),
                             jnp.full((W_STAGES,), N_EXPERTS, jnp.int32)]).astype(jnp.int32)
    return tok_words, row_w, block_e, first, run_idx, run_e, n_valid.reshape(1), pos_words


def kernel(x, positions, w_in, b_gate, conv_w, conv_b, w_rg_a, b_rg_a, w_rg_x, b_rg_x, lru_lambda, w_attn_proj, w_rec_proj, w_out, ln1_g, ln1_b, w_router_group, b_router_group, w_router_expert, b_router_expert, w_gate, w_up, w_down, ln2_g, ln2_b):
    B, S, D = x.shape
    T = B * S
    h = x.reshape(T, D)
    c, sa, sb = _rope_tables(positions)
    for layer in range(DEPTH):
        xb = h.astype(BF16)
        os_, lses = [], []
        for g, (_, dilation) in enumerate(ATTN_GROUPS):
            qkv = _qkv_proj(xb, w_in[layer], c, sa, sb, g, dilation)
            o, lse = _attention_group(qkv, B, S, dilation)
            os_.append(o)
            lses.append(lse)
        rest = _rest_proj(xb, w_in[layer])
        rec = _rglru(rest, B, S, conv_w[layer], conv_b[layer], w_rg_a[layer].astype(BF16), b_rg_a[layer],
                     w_rg_x[layer].astype(BF16), b_rg_x[layer], lru_lambda[layer])
        merged = _merge(os_, lses, rec, rest, b_gate[layer], w_attn_proj[layer].astype(BF16),
                        w_rec_proj[layer].astype(BF16))
        w_router = jnp.concatenate([w_router_group[layer], w_router_expert[layer]], axis=1)
        w_router = jnp.pad(w_router, ((0, 0), (0, LANES - w_router.shape[1])))
        b_router = jnp.concatenate([b_router_group[layer], b_router_expert[layer]])
        b_router = jnp.pad(b_router, (0, LANES - b_router.shape[0])).reshape(1, LANES)
        mix = _out_matmul(merged, w_out[layer].astype(BF16))
        h1, h1p, logits = _ln1_router(mix, h, ln1_g[layer], ln1_b[layer], w_router.astype(BF16), b_router)
        tok_words, row_w, block_e, first, run_idx, run_e, n_valid, pos_words = _routing(logits)
        moe_rows = _moe_experts(h1p, tok_words, row_w, block_e, first, run_idx, run_e, n_valid,
                                w_gate[layer], w_up[layer], w_down[layer])
        h = _moe_combine_ln2(h1, moe_rows, pos_words, ln2_g[layer], ln2_b[layer])
    return h.reshape(B, S, D)
```

```python
import functools

import jax
import jax.numpy as jnp
from jax import lax
from jax.experimental import pallas as pl
from jax.experimental.pallas import tpu as pltpu

D_MODEL = 4096
HEAD_DIM = 128
ATTN_GROUPS = ((128, 1), (512, 4), (2048, 16))
HEADS_PER_GROUP = 8
N_ATTN_HEADS = HEADS_PER_GROUP * len(ATTN_GROUPS)
ATTN_WIDTH = N_ATTN_HEADS * HEAD_DIM
ATTN_OUT_WIDTH = HEADS_PER_GROUP * HEAD_DIM
ROT_DIM = HEAD_DIM // 4
ROPE_THETA = 500000.0
ATTN_SPAN = 128
LRU_WIDTH = 2048
LRU_BLOCKS = 8
LRU_BLOCK_W = LRU_WIDTH // LRU_BLOCKS
CONV_WIDTH = 4
RG_C = 8.0
N_GROUPS = 8
EXPERTS_PER_GROUP = 8
N_EXPERTS = N_GROUPS * EXPERTS_PER_GROUP
TOP_K = 2
D_EXPERT = 512
LN_EPS = 1e-5
DEPTH = 1
DEEPNORM_ALPHA = (2 * DEPTH) ** 0.25
REST_OFF = 3 * ATTN_WIDTH
REST_COLS = 2 * LRU_WIDTH + 2 * D_MODEL
RX_COL = 0
RG_COL = LRU_WIDTH
GATE_COL = 2 * LRU_WIDTH

LANES = 128
COL_TILE = 1024
PROJ_TM = 1024
PROJ_TN = 512
VMEM_LIMIT = 56 * 1024 * 1024

F32 = jnp.float32
BF16 = jnp.bfloat16


def _params(*sem, vmem=VMEM_LIMIT):
    return pltpu.CompilerParams(dimension_semantics=sem, vmem_limit_bytes=vmem)


def _rope_table_kernel(pos_ref, invf_ref, c_ref, sa_ref, sb_ref):
    half = ROT_DIM // 2
    ang = pos_ref[...].astype(F32) * invf_ref[...]
    lane = lax.broadcasted_iota(jnp.int32, ang.shape, 1)
    cos = jnp.cos(ang)
    sin = jnp.sin(ang)
    c_ref[...] = jnp.where(lane < ROT_DIM, cos, 1.0)
    sa_ref[...] = jnp.where(lane < half, -sin, 0.0)
    sb_ref[...] = jnp.where((lane >= half) & (lane < ROT_DIM), sin, 0.0)


def _rope_tables(positions):
    T = positions.size
    tm = 2048
    half = ROT_DIM // 2
    inv_freq = jnp.power(jnp.float32(ROPE_THETA), -jnp.arange(half, dtype=F32) * 2.0 / ROT_DIM)
    invf = jnp.zeros((1, LANES), F32).at[0, :ROT_DIM].set(jnp.concatenate([inv_freq, inv_freq]))
    tab = jax.ShapeDtypeStruct((T, LANES), F32)
    return pl.pallas_call(
        _rope_table_kernel,
        grid=(T // tm,),
        in_specs=[pl.BlockSpec((tm, 1), lambda i: (i, 0)), pl.BlockSpec((1, LANES), lambda i: (0, 0))],
        out_specs=[pl.BlockSpec((tm, LANES), lambda i: (i, 0))] * 3,
        out_shape=[tab, tab, tab],
        compiler_params=_params("arbitrary"),
        name="rope_tables",
    )(positions.reshape(T, 1), invf)


def _qkv_proj_kernel(x_ref, w_ref, c_ref, sa_ref, sb_ref, o_ref, wb_ref, acc_a, acc_b, slab_ref, *, d, row_tiles):
    s = pl.program_id(0)
    tiles_per_sect = ATTN_OUT_WIDTH // PROJ_TN
    heads = PROJ_TN // HEAD_DIM
    per = PROJ_TM // d

    @pl.when(s == 0)
    def _():
        acc_b[...] = jnp.zeros(acc_b.shape, F32)

    @pl.when(s % row_tiles == 0)
    def _():
        wb_ref[...] = w_ref[...].astype(BF16)

    def body(cur_ref, prev_ref):
        cur_ref[...] = jnp.dot(x_ref[...], wb_ref[...], preferred_element_type=F32)
        n_prev = jnp.maximum(s - 1, 0) // row_tiles
        sect = n_prev // tiles_per_sect
        rot = sect < 2
        scale = jnp.where(sect == 0, HEAD_DIM ** -0.5, 1.0).astype(F32)
        c = jnp.where(rot, c_ref[...], 1.0)
        sa = jnp.where(rot, sa_ref[...], 0.0)
        sb = jnp.where(rot, sb_ref[...], 0.0)
        for h in range(heads):
            hs = slice(h * HEAD_DIM, (h + 1) * HEAD_DIM)
            t = prev_ref[:, hs]
            u = pltpu.bitcast(t.astype(BF16), jnp.uint32)
            ta = pltpu.bitcast(pltpu.roll(u, HEAD_DIM - ROT_DIM // 2, 1), BF16).astype(F32)
            tb = pltpu.bitcast(pltpu.roll(u, ROT_DIM // 2, 1), BF16).astype(F32)
            r = (t * c + ta * sa + tb * sb) * scale
            if d == 1:
                o_ref[:, hs] = r.astype(o_ref.dtype)
            else:
                slab_ref[h] = r
                for res in range(d):
                    rows = slab_ref[h, pl.ds(res, per, stride=d), :]
                    o_ref[res * per:(res + 1) * per, hs] = rows.astype(o_ref.dtype)

    @pl.when(s % 2 == 0)
    def _():
        body(acc_a, acc_b)

    @pl.when(s % 2 == 1)
    def _():
        body(acc_b, acc_a)


def _qkv_proj(xb, w, c, sa, sb, g, d):
    T = xb.shape[0]
    tiles_per_sect = ATTN_OUT_WIDTH // PROJ_TN
    sect_tiles = ATTN_WIDTH // PROJ_TN
    row_tiles = T // PROJ_TM
    n_tiles = 3 * tiles_per_sect * row_tiles
    wcol = lambda n: (n // tiles_per_sect) * sect_tiles + g * tiles_per_sect + n % tiles_per_sect
    cur = lambda s: jnp.minimum(s, n_tiles - 1)
    fin = lambda s: jnp.maximum(s - 1, 0)
    tab_spec = pl.BlockSpec((PROJ_TM, LANES), lambda s: (fin(s) % row_tiles, 0))
    acc = pltpu.VMEM((PROJ_TM, PROJ_TN), F32)
    return pl.pallas_call(
        functools.partial(_qkv_proj_kernel, d=d, row_tiles=row_tiles),
        grid=(n_tiles + 1,),
        in_specs=[pl.BlockSpec((PROJ_TM, D_MODEL), lambda s: (cur(s) % row_tiles, 0)),
                  pl.BlockSpec((D_MODEL, PROJ_TN), lambda s: (0, wcol(cur(s) // row_tiles))),
                  tab_spec, tab_spec, tab_spec],
        out_specs=pl.BlockSpec((PROJ_TM, PROJ_TN), lambda s: (fin(s) % row_tiles, fin(s) // row_tiles)),
        out_shape=jax.ShapeDtypeStruct((T, 3 * ATTN_OUT_WIDTH), BF16),
        scratch_shapes=[pltpu.VMEM((D_MODEL, PROJ_TN), BF16), acc, acc,
                        pltpu.VMEM((PROJ_TN // HEAD_DIM, PROJ_TM, LANES), F32)],
        compiler_params=_params("arbitrary"),
        name=f"qkv_proj_d{d}",
    )(xb, w, c, sa, sb)


def _rest_proj_kernel(x_ref, w_ref, o_ref, wb_ref):
    @pl.when(pl.program_id(1) == 0)
    def _():
        wb_ref[...] = w_ref[...].astype(BF16)

    o_ref[...] = jnp.dot(x_ref[...], wb_ref[...], preferred_element_type=F32).astype(o_ref.dtype)


def _rest_proj(xb, w):
    T = xb.shape[0]
    first = REST_OFF // PROJ_TN
    return pl.pallas_call(
        _rest_proj_kernel,
        grid=(REST_COLS // PROJ_TN, T // PROJ_TM),
        in_specs=[pl.BlockSpec((PROJ_TM, D_MODEL), lambda n, m: (m, 0)),
                  pl.BlockSpec((D_MODEL, PROJ_TN), lambda n, m: (0, first + n))],
        out_specs=pl.BlockSpec((PROJ_TM, PROJ_TN), lambda n, m: (m, n)),
        out_shape=jax.ShapeDtypeStruct((T, REST_COLS), BF16),
        scratch_shapes=[pltpu.VMEM((D_MODEL, PROJ_TN), BF16)],
        compiler_params=_params("arbitrary", "arbitrary"),
        name="rest_proj",
    )(xb, w)


def _attn_kernel(*refs, npc, d):
    q_refs = refs[:npc]
    kp_refs, kc_refs = refs[npc:2 * npc], refs[2 * npc:3 * npc]
    vp_refs, vc_refs = refs[3 * npc:4 * npc], refs[4 * npc:5 * npc]
    o_ref, lse_ref = refs[5 * npc:]
    n = pl.program_id(1)
    res_class = pl.program_id(2)
    nh, blk = HEADS_PER_GROUP, ATTN_SPAN
    rows_out = slice(None) if d == 1 else pl.ds(res_class, blk, stride=d)

    def head(pieces, h):
        hs = slice(h * HEAD_DIM, (h + 1) * HEAD_DIM)
        parts = [r[:, hs] for r in pieces]
        return parts[0] if len(parts) == 1 else jnp.concatenate(parts, axis=0)

    dn = (((1,), (1,)), ((), ()))
    qs = [head(q_refs, h) for h in range(nh)]
    sp = jnp.concatenate([lax.dot_general(qs[h], head(kp_refs, h), dn, preferred_element_type=F32)
                          for h in range(nh)], axis=0)
    sc = jnp.concatenate([lax.dot_general(qs[h], head(kc_refs, h), dn, preferred_element_type=F32)
                          for h in range(nh)], axis=0)
    row = lax.broadcasted_iota(jnp.int32, sp.shape, 0) & (blk - 1)
    col = lax.broadcasted_iota(jnp.int32, sp.shape, 1)
    sp = jnp.where(jnp.logical_and(col >= row, n > 0), sp, -jnp.inf)
    sc = jnp.where(col <= row, sc, -jnp.inf)
    m = jnp.maximum(jnp.max(sp, axis=1, keepdims=True), jnp.max(sc, axis=1, keepdims=True))
    p = jnp.concatenate([jnp.exp(sp - m).astype(BF16), jnp.exp(sc - m).astype(BF16)], axis=1)
    ones = jnp.ones((2 * blk, LANES), BF16)
    lane = lax.broadcasted_iota(jnp.int32, (blk, LANES), 1)
    lse_all = jnp.zeros((blk, LANES), F32)
    for h in range(nh):
        rs = slice(h * blk, (h + 1) * blk)
        v_ext = jnp.concatenate([jnp.concatenate([head(vp_refs, h), head(vc_refs, h)], axis=0), ones], axis=1)
        res = jnp.dot(p[rs], v_ext, preferred_element_type=F32)
        l = res[:, LANES:]
        o_ref[h, rows_out, :] = res[:, :LANES] * (1.0 / l)
        lse_all = jnp.where(lane == h, m[rs] + jnp.log(l), lse_all)
    lse_ref[rows_out, :] = lse_all


def _attention_group(qkv, batch, seq, dilation):
    L = seq // dilation
    per = PROJ_TM // dilation
    pr = min(per, ATTN_SPAN)
    npc = ATTN_SPAN // pr
    nblk = L // ATTN_SPAN
    tiles_per_batch = seq // PROJ_TM
    qcol, kcol, vcol = 0, 1, 2

    def piece_block(b, res, n, p):
        pos = n * ATTN_SPAN + p * pr
        row = (b * tiles_per_batch + pos // per) * PROJ_TM + res * per + pos % per
        return row // pr

    def specs(col, back):
        return [pl.BlockSpec((pr, COL_TILE),
                             lambda b, n, r, p=p: (piece_block(b, r, jnp.maximum(n - back, 0), p), col))
                for p in range(npc)]

    in_specs = specs(qcol, 0) + specs(kcol, 1) + specs(kcol, 0) + specs(vcol, 1) + specs(vcol, 0)
    tok_rows = ATTN_SPAN * dilation
    return pl.pallas_call(
        functools.partial(_attn_kernel, npc=npc, d=dilation),
        grid=(batch, nblk, dilation),
        in_specs=in_specs,
        out_specs=[pl.BlockSpec((HEADS_PER_GROUP, tok_rows, HEAD_DIM), lambda b, n, r: (0, b * nblk + n, 0)),
                   pl.BlockSpec((tok_rows, LANES), lambda b, n, r: (b * nblk + n, 0))],
        out_shape=[jax.ShapeDtypeStruct((HEADS_PER_GROUP, batch * seq, HEAD_DIM), F32),
                   jax.ShapeDtypeStruct((batch * seq, LANES), F32)],
        compiler_params=_params("parallel", "arbitrary", "arbitrary"),
        name=f"attn_d{dilation}",
    )(*([qkv] * (5 * npc)))


_TAIL = 8


def _rglru_kernel(rx_ref, rg_ref, cw_ref, cb_ref, wa_ref, ba_ref, wx_ref, bx_ref, lam_ref, o_ref,
                  xbuf, a_buf, h_buf, carry):
    tt = rx_ref.shape[0]
    t_idx = pl.program_id(2)

    @pl.when(t_idx == 0)
    def _():
        xbuf[0:_TAIL, :] = jnp.zeros((_TAIL, xbuf.shape[1]), F32)
        carry[...] = jnp.zeros(carry.shape, F32)

    x = rx_ref[...].astype(F32)
    xbuf[_TAIL:_TAIL + tt, :] = x
    xr = cb_ref[...] + cw_ref[0:1, :] * xbuf[_TAIL - 3:_TAIL - 3 + tt, :]
    for j in range(1, CONV_WIDTH):
        xr = xr + cw_ref[j:j + 1, :] * xbuf[_TAIL - 3 + j:_TAIL - 3 + j + tt, :]
    xbuf[0:_TAIL, :] = x[tt - _TAIL:tt, :]

    softplus_neg_lam = jnp.log1p(jnp.exp(-lam_ref[...]))
    for blk in range(xr.shape[1] // LRU_BLOCK_W):
        cs = slice(blk * LRU_BLOCK_W, (blk + 1) * LRU_BLOCK_W)
        xb = xr[:, cs]
        xb16 = xb.astype(BF16)
        r = jax.nn.sigmoid(jnp.dot(xb16, wa_ref[blk], preferred_element_type=F32) + ba_ref[:, cs])
        i = jax.nn.sigmoid(jnp.dot(xb16, wx_ref[blk], preferred_element_type=F32) + bx_ref[:, cs])
        log_a = (-RG_C) * r * softplus_neg_lam[:, cs]
        a = jnp.exp(log_a)
        a_buf[:, cs] = a
        h_buf[:, cs] = jnp.sqrt(1.0 - a * a) * (i * xb)

    def step(t, h):
        h = a_buf[pl.ds(t, 1), :] * h + h_buf[pl.ds(t, 1), :]
        h_buf[pl.ds(t, 1), :] = h
        return h

    carry[...] = lax.fori_loop(0, tt, step, carry[...], unroll=8)
    o_ref[...] = (h_buf[...] * jax.nn.gelu(rg_ref[...].astype(F32))).astype(o_ref.dtype)


def _rglru(rest, batch, seq, conv_w, conv_b, wa, ba, wx, bx, lam):
    tt = 512
    nt = seq // tt
    hw = COL_TILE
    assert hw % LRU_BLOCK_W == 0 and RX_COL % hw == 0 and RG_COL % hw == 0
    gb = hw // LRU_BLOCK_W
    row = lambda off: pl.BlockSpec((tt, hw), lambda b, c, t: (b * nt + t, off // hw + c))
    vec = pl.BlockSpec((1, hw), lambda b, c, t: (0, c))
    wspec = pl.BlockSpec((gb, LRU_BLOCK_W, LRU_BLOCK_W), lambda b, c, t: (c, 0, 0))
    return pl.pallas_call(
        _rglru_kernel,
        grid=(batch, LRU_WIDTH // hw, nt),
        in_specs=[row(RX_COL), row(RG_COL), pl.BlockSpec((CONV_WIDTH, hw), lambda b, c, t: (0, c)), vec,
                  wspec, vec, wspec, vec, vec],
        out_specs=pl.BlockSpec((tt, hw), lambda b, c, t: (b * nt + t, c)),
        out_shape=jax.ShapeDtypeStruct((batch * seq, LRU_WIDTH), BF16),
        scratch_shapes=[pltpu.VMEM((tt + _TAIL, hw), F32), pltpu.VMEM((tt, hw), F32),
                        pltpu.VMEM((tt, hw), F32), pltpu.VMEM((1, hw), F32)],
        compiler_params=_params("parallel", "parallel", "arbitrary"),
        name="rg_lru",
    )(rest, rest, conv_w, conv_b.reshape(1, -1), wa, ba.reshape(1, -1), wx, bx.reshape(1, -1), lam.reshape(1, -1))


def _merge_kernel(*refs):
    o0_ref, o1_ref, o2_ref, l0_ref, l1_ref, l2_ref, rec_ref = refs[:7]
    nct = D_MODEL // COL_TILE
    ga_refs = refs[7:7 + nct]
    gr_refs = refs[7 + nct:7 + 2 * nct]
    bg_ref, wa_ref, wr_ref, out_ref = refs[7 + 2 * nct:]
    l0, l1, l2 = l0_ref[...], l1_ref[...], l2_ref[...]
    parts = []
    for h in range(HEADS_PER_GROUP):
        a0, a1, a2 = l0[:, h:h + 1], l1[:, h:h + 1], l2[:, h:h + 1]
        m = jnp.maximum(jnp.maximum(a0, a1), a2)
        e0, e1, e2 = jnp.exp(a0 - m), jnp.exp(a1 - m), jnp.exp(a2 - m)
        inv = 1.0 / (e0 + e1 + e2)
        mixed = (e0 * inv) * o0_ref[h] + (e1 * inv) * o1_ref[h] + (e2 * inv) * o2_ref[h]
        parts.append(mixed.astype(BF16))
    attn = jnp.concatenate(parts, axis=1)
    rec = rec_ref[...]
    for j in range(nct):
        cs = slice(j * COL_TILE, (j + 1) * COL_TILE)
        ya = jnp.dot(attn, wa_ref[:, cs], preferred_element_type=F32)
        yr = jnp.dot(rec, wr_ref[:, cs], preferred_element_type=F32)
        gate_a = jax.nn.sigmoid(ga_refs[j][...].astype(F32) + bg_ref[0:1, cs])
        gate_r = jax.nn.sigmoid(gr_refs[j][...].astype(F32) + bg_ref[1:2, cs])
        out_ref[:, cs] = (gate_a * ya + gate_r * yr).astype(out_ref.dtype)


def _merge(os_, lses, rec, rest, b_gate, wa, wr):
    T = rec.shape[0]
    tm = 256
    nct = D_MODEL // COL_TILE
    row = lambda m: (m, 0)
    const = lambda m: (0, 0)
    gate_specs = [pl.BlockSpec((tm, COL_TILE), lambda m, c=(GATE_COL + k * D_MODEL) // COL_TILE + j: (m, c))
                  for k in range(2) for j in range(nct)]
    o_spec = pl.BlockSpec((HEADS_PER_GROUP, tm, HEAD_DIM), lambda m: (0, m, 0))
    l_spec = pl.BlockSpec((tm, LANES), row)
    resident = lambda shape: pl.BlockSpec(shape, const, pipeline_mode=pl.Buffered(1))
    return pl.pallas_call(
        _merge_kernel,
        grid=(T // tm,),
        in_specs=[o_spec, o_spec, o_spec, l_spec, l_spec, l_spec, pl.BlockSpec((tm, LRU_WIDTH), row)]
                 + gate_specs
                 + [resident((2, D_MODEL)), resident((ATTN_OUT_WIDTH, D_MODEL)), resident((LRU_WIDTH, D_MODEL))],
        out_specs=pl.BlockSpec((tm, D_MODEL), row),
        out_shape=jax.ShapeDtypeStruct((T, D_MODEL), BF16),
        compiler_params=_params("parallel"),
        name="merge",
    )(*os_, *lses, rec, *([rest] * (2 * nct)), b_gate, wa, wr)


def _layer_norm(z, g, b):
    mu = jnp.mean(z, axis=-1, keepdims=True)
    zc = z - mu
    var = jnp.mean(zc * zc, axis=-1, keepdims=True)
    return zc * lax.rsqrt(var + LN_EPS) * g + b


_HI_MASK = 0xFFFF0000


def _pack_bf16_pairs(y):
    bits = lax.bitcast_convert_type(y.astype(BF16).astype(F32), jnp.uint32)
    c = y.shape[1] // 2
    return (bits[:, :c] >> 16) | (bits[:, c:] & jnp.uint32(_HI_MASK))


def _unpack_bf16_pairs(p):
    lo = lax.bitcast_convert_type(p << 16, F32)
    hi = lax.bitcast_convert_type(p & jnp.uint32(_HI_MASK), F32)
    return lo, hi


def _out_matmul_kernel(mg_ref, w_ref, o_ref):
    o_ref[...] = jnp.dot(mg_ref[...], w_ref[...], preferred_element_type=F32).astype(o_ref.dtype)


def _out_matmul(merged, w_out):
    T = merged.shape[0]
    tm, tn = 1024, 1024
    return pl.pallas_call(
        _out_matmul_kernel,
        grid=(T // tm, D_MODEL // tn),
        in_specs=[pl.BlockSpec((tm, D_MODEL), lambda m, n: (m, 0)),
                  pl.BlockSpec((D_MODEL, tn), lambda m, n: (0, n))],
        out_specs=pl.BlockSpec((tm, tn), lambda m, n: (m, n)),
        out_shape=jax.ShapeDtypeStruct((T, D_MODEL), BF16),
        compiler_params=_params("parallel", "arbitrary"),
        name="out_matmul",
    )(merged, w_out)


def _ln1_kernel(mix_ref, x_ref, g_ref, b_ref, wr_ref, br_ref, h_ref, hp_ref, lg_ref):
    z = DEEPNORM_ALPHA * x_ref[...] + mix_ref[...].astype(F32)
    y = _layer_norm(z, g_ref[...], b_ref[...])
    h_ref[...] = y
    hp_ref[...] = _pack_bf16_pairs(y)
    lg_ref[...] = jnp.dot(y.astype(BF16), wr_ref[...], preferred_element_type=F32) + br_ref[...]


def _ln1_router(mix, x2, g, b, wr, br):
    T = mix.shape[0]
    tm = 256
    row = lambda m: (m, 0)
    const = lambda m: (0, 0)
    return pl.pallas_call(
        _ln1_kernel,
        grid=(T // tm,),
        in_specs=[pl.BlockSpec((tm, D_MODEL), row), pl.BlockSpec((tm, D_MODEL), row),
                  pl.BlockSpec((1, D_MODEL), const), pl.BlockSpec((1, D_MODEL), const),
                  pl.BlockSpec((D_MODEL, LANES), const), pl.BlockSpec((1, LANES), const)],
        out_specs=[pl.BlockSpec((tm, D_MODEL), row), pl.BlockSpec((tm, D_MODEL // 2), row),
                   pl.BlockSpec((tm, LANES), row)],
        out_shape=[jax.ShapeDtypeStruct((T, D_MODEL), F32), jax.ShapeDtypeStruct((T, D_MODEL // 2), jnp.uint32),
                   jax.ShapeDtypeStruct((T, LANES), F32)],
        compiler_params=_params("parallel"),
        name="ln1_router",
    )(mix, x2, g.reshape(1, -1), b.reshape(1, -1), wr, br)


MOE_TM = 256
W_PARTS = 4
W_STAGES = 2


def _moe_up_kernel(be_ref, nv_ref, tok_ref, h_hbm, wg_ref, wu_ref, hid_ref, xbuf, sem):
    b = pl.program_id(0)
    nv = nv_ref[0]
    words = MOE_TM // 2

    def row_copy(tok, slot, r):
        return pltpu.make_async_copy(h_hbm.at[pl.ds(tok, 1), :], xbuf.at[slot, pl.ds(r, 1), :], sem.at[slot])

    def start_pair(blk, slot, w):
        word = tok_ref[blk * words + w]
        row_copy(word & 0xFFFF, slot, 2 * w).start()
        row_copy((word >> 16) & 0xFFFF, slot, 2 * w + 1).start()

    def start_gather_loop(blk, slot):
        def body(w, c):
            start_pair(blk, slot, w)
            return c
        lax.fori_loop(0, words, body, 0, unroll=4)

    def start_gather_inline(blk, slot):
        for w in range(words):
            start_pair(blk, slot, w)

    def wait_gather(slot):
        def body(r, c):
            row_copy(0, slot, r).wait()
            return c
        lax.fori_loop(0, MOE_TM, body, 0, unroll=8)

    def compute(slot):
        lo, hi = _unpack_bf16_pairs(xbuf[slot])
        x = jnp.concatenate([lo.astype(BF16), hi.astype(BF16)], axis=1)
        gate = jnp.dot(x, wg_ref[0].astype(BF16), preferred_element_type=F32)
        up = jnp.dot(x, wu_ref[0].astype(BF16), preferred_element_type=F32)
        hid_ref[...] = (gate * jax.nn.sigmoid(gate) * up).astype(hid_ref.dtype)

    @pl.when(b == 0)
    def _():
        start_gather_loop(0, 0)

    @pl.when(b + 1 < nv)
    def _():
        wait_gather(b % 2)
        start_gather_inline(b + 1, (b + 1) % 2)
        compute(b % 2)

    @pl.when(b + 1 == nv)
    def _():
        wait_gather(b % 2)
        compute(b % 2)

    @pl.when(b >= nv)
    def _():
        hid_ref[...] = jnp.zeros(hid_ref.shape, hid_ref.dtype)


def _expert_weight_copies(srcs, e, stage_ref, st, sem):
    copies = []
    for which, src in enumerate(srcs):
        rows = src.shape[1] // W_PARTS
        for p in range(W_PARTS):
            rs = pl.ds(p * rows, rows)
            copies.append(pltpu.make_async_copy(src.at[e, rs, :], stage_ref.at[st, which, rs, :], sem.at[st]))
    return copies


def _expert_weight_pipeline(b, first_ref, run_ref, rune_ref, srcs, stage_ref, bf16_refs, sem):
    @pl.when(b == 0)
    def _():
        for st in range(W_STAGES):
            @pl.when(rune_ref[st] < N_EXPERTS)
            def _(st=st):
                for c in _expert_weight_copies(srcs, rune_ref[st], stage_ref, st, sem):
                    c.start()

    @pl.when(first_ref[b] == 1)
    def _():
        k = run_ref[b]
        st = k % W_STAGES
        for c in _expert_weight_copies(srcs, 0, stage_ref, st, sem):
            c.wait()
        for which, dst in enumerate(bf16_refs):
            rows = dst.shape[0] // W_PARTS
            for p in range(W_PARTS):
                rs = slice(p * rows, (p + 1) * rows)
                dst[rs, :] = stage_ref[st, which, rs, :].astype(BF16)
        e_next = rune_ref[k + W_STAGES]

        @pl.when(e_next < N_EXPERTS)
        def _():
            for c in _expert_weight_copies(srcs, e_next, stage_ref, st, sem):
                c.start()


def _moe_down_kernel(first_ref, run_ref, rune_ref, nv_ref, hid_ref, rw_ref, wd_hbm, out_ref, wstage, wdb, wsem):
    b = pl.program_id(0)
    nv = nv_ref[0]
    _expert_weight_pipeline(b, first_ref, run_ref, rune_ref, (wd_hbm,), wstage, (wdb,), wsem)

    @pl.when(b < nv)
    def _():
        y = jnp.dot(hid_ref[...], wdb[...], preferred_element_type=F32)
        out_ref[...] = _pack_bf16_pairs(y * rw_ref[...])

    @pl.when(b >= nv)
    def _():
        out_ref[...] = jnp.zeros(out_ref.shape, out_ref.dtype)


def _moe_experts(h1p, tok_words, row_w, block_e, first, run_idx, run_e, n_valid, wg, wu, wd):
    rows = row_w.shape[0]
    nb = rows // MOE_TM
    hid = pl.pallas_call(
        _moe_up_kernel,
        grid_spec=pltpu.PrefetchScalarGridSpec(
            num_scalar_prefetch=3,
            grid=(nb,),
            in_specs=[pl.BlockSpec(memory_space=pl.ANY),
                      pl.BlockSpec((1, D_MODEL, D_EXPERT), lambda b, be, nv, tok: (be[b], 0, 0)),
                      pl.BlockSpec((1, D_MODEL, D_EXPERT), lambda b, be, nv, tok: (be[b], 0, 0))],
            out_specs=pl.BlockSpec((MOE_TM, D_EXPERT), lambda b, be, nv, tok: (b, 0)),
            scratch_shapes=[pltpu.VMEM((2, MOE_TM, D_MODEL // 2), jnp.uint32), pltpu.SemaphoreType.DMA((2,))]),
        out_shape=jax.ShapeDtypeStruct((rows, D_EXPERT), BF16),
        compiler_params=_params("arbitrary"),
        name="moe_up",
    )(block_e, n_valid, tok_words, h1p, wg, wu)
    return pl.pallas_call(
        _moe_down_kernel,
        grid_spec=pltpu.PrefetchScalarGridSpec(
            num_scalar_prefetch=4,
            grid=(nb,),
            in_specs=[pl.BlockSpec((MOE_TM, D_EXPERT), lambda b, *_: (b, 0)),
                      pl.BlockSpec((MOE_TM, 1), lambda b, *_: (b, 0)),
                      pl.BlockSpec(memory_space=pl.ANY)],
            out_specs=pl.BlockSpec((MOE_TM, D_MODEL // 2), lambda b, *_: (b, 0)),
            scratch_shapes=[pltpu.VMEM((W_STAGES, 1, D_EXPERT, D_MODEL), F32), pltpu.VMEM((D_EXPERT, D_MODEL), BF16),
                            pltpu.SemaphoreType.DMA((W_STAGES,))]),
        out_shape=jax.ShapeDtypeStruct((rows, D_MODEL // 2), jnp.uint32),
        compiler_params=_params("arbitrary"),
        name="moe_down",
    )(first, run_idx, run_e, n_valid, hid, row_w.reshape(rows, 1), wd)


FIN_TM = 256


def _final_kernel(pos_ref, h_ref, y_hbm, g_ref, b_ref, out_ref, ybuf, sem):
    i = pl.program_id(0)
    n = pl.num_programs(0)

    def row_copy(src_row, slot, r):
        return pltpu.make_async_copy(y_hbm.at[pl.ds(src_row, 1), :], ybuf.at[slot, pl.ds(r, 1), :], sem.at[slot])

    def start_token(tile, slot, r):
        word = pos_ref[tile * FIN_TM + r]
        row_copy(word & 0xFFFF, slot, r).start()
        row_copy((word >> 16) & 0xFFFF, slot, FIN_TM + r).start()

    def start_gather_loop(tile, slot):
        def body(r, c):
            start_token(tile, slot, r)
            return c
        lax.fori_loop(0, FIN_TM, body, 0, unroll=4)

    def start_gather_inline(tile, slot):
        for r in range(FIN_TM):
            start_token(tile, slot, r)

    def wait_gather(slot):
        def body(r, c):
            row_copy(0, slot, r).wait()
            return c
        lax.fori_loop(0, TOP_K * FIN_TM, body, 0, unroll=8)

    def compute(slot):
        ff_lo, ff_hi = _unpack_bf16_pairs(ybuf[slot, 0:FIN_TM, :])
        for k in range(1, TOP_K):
            lo, hi = _unpack_bf16_pairs(ybuf[slot, k * FIN_TM:(k + 1) * FIN_TM, :])
            ff_lo, ff_hi = ff_lo + lo, ff_hi + hi
        z = DEEPNORM_ALPHA * h_ref[...] + jnp.concatenate([ff_lo, ff_hi], axis=1)
        out_ref[...] = _layer_norm(z, g_ref[...], b_ref[...])

    @pl.when(i == 0)
    def _():
        start_gather_loop(0, 0)

    @pl.when(i + 1 < n)
    def _():
        wait_gather(i % 2)
        start_gather_inline(i + 1, (i + 1) % 2)
        compute(i % 2)

    @pl.when(i + 1 == n)
    def _():
        wait_gather(i % 2)
        compute(i % 2)


def _moe_combine_ln2(h1, moe_rows, pos_words, g, b):
    T = h1.shape[0]
    assert TOP_K == 2
    const = lambda i, pos: (0, 0)
    grid_spec = pltpu.PrefetchScalarGridSpec(
        num_scalar_prefetch=1,
        grid=(T // FIN_TM,),
        in_specs=[pl.BlockSpec((FIN_TM, D_MODEL), lambda i, pos: (i, 0)),
                  pl.BlockSpec(memory_space=pl.ANY),
                  pl.BlockSpec((1, D_MODEL), const), pl.BlockSpec((1, D_MODEL), const)],
        out_specs=pl.BlockSpec((FIN_TM, D_MODEL), lambda i, pos: (i, 0)),
        scratch_shapes=[pltpu.VMEM((2, TOP_K * FIN_TM, D_MODEL // 2), jnp.uint32), pltpu.SemaphoreType.DMA((2,))],
    )
    return pl.pallas_call(
        _final_kernel,
        grid_spec=grid_spec,
        out_shape=jax.ShapeDtypeStruct((T, D_MODEL), F32),
        compiler_params=_params("arbitrary"),
        name="moe_combine_ln2",
    )(pos_words, h1, moe_rows, g.reshape(1, -1), b.reshape(1, -1))


def _pack_u16_pairs(lo, hi):
    return (lo.astype(jnp.int32) | (hi.astype(jnp.int32) << 16)).astype(jnp.int32)


def _routing(logits):
    T = logits.shape[0]
    logits_g = logits[:, :N_GROUPS]
    logits_e = logits[:, N_GROUPS:N_GROUPS + N_EXPERTS].reshape(T, N_GROUPS, EXPERTS_PER_GROUP)
    p_g = jax.nn.softmax(logits_g, axis=-1)
    g_idx = jnp.argmax(logits_g, axis=-1)
    p_grp = jnp.take_along_axis(p_g, g_idx[:, None], axis=-1)
    le = jnp.take_along_axis(logits_e, g_idx[:, None, None], axis=1)[:, 0]
    top_v, top_i = lax.top_k(le, TOP_K)
    weight = p_grp * jax.nn.softmax(top_v, axis=-1)
    expert_id = g_idx[:, None].astype(jnp.int32) * EXPERTS_PER_GROUP + top_i.astype(jnp.int32)

    A = T * TOP_K
    flat_e = expert_id.reshape(-1)
    flat_w = weight.reshape(-1)
    experts = jnp.arange(N_EXPERTS, dtype=jnp.int32)
    counts = jnp.sum((flat_e[:, None] == experts[None, :]).astype(jnp.int32), axis=0)
    starts = jnp.cumsum(counts) - counts
    padded = (counts + MOE_TM - 1) // MOE_TM * MOE_TM
    pends = jnp.cumsum(padded)
    pstarts = pends - padded
    order = jnp.argsort(flat_e, stable=True).astype(jnp.int32)
    inv = jnp.argsort(order).astype(jnp.int32)
    pos = (inv + (pstarts - starts)[flat_e]).astype(jnp.int32)
    nb = A // MOE_TM + N_EXPERTS
    assert nb * MOE_TM < 2 ** 16 and T < 2 ** 16
    n_valid = (pends[-1] // MOE_TM).astype(jnp.int32)
    blk = jnp.arange(nb, dtype=jnp.int32)
    blk_start = jnp.minimum(blk, n_valid - 1) * MOE_TM
    block_e = jnp.minimum(jnp.sum((blk_start[:, None] >= pends[None, :]).astype(jnp.int32), axis=1), N_EXPERTS - 1)
    i_in = (blk * MOE_TM - pstarts[block_e])[:, None] + jnp.arange(MOE_TM, dtype=jnp.int32)[None, :]
    valid = i_in < counts[block_e][:, None]
    a_r = order[jnp.clip(starts[block_e][:, None] + i_in, 0, A - 1)]
    row_tok = jnp.where(valid, a_r // TOP_K, 0).astype(jnp.int32)
    row_w = jnp.where(valid, flat_w[a_r], 0.0).astype(F32).reshape(-1)
    tok_words = _pack_u16_pairs(row_tok[:, 0::2], row_tok[:, 1::2]).reshape(-1)
    pos2 = pos.reshape(T, TOP_K)
    pos_words = _pack_u16_pairs(pos2[:, 0], pos2[:, 1])
    prev_e = jnp.concatenate([jnp.full((1,), -1, jnp.int32), block_e[:-1]])
    first = jnp.logical_and(block_e != prev_e, blk < n_valid).astype(jnp.int32)
    run_idx = jnp.maximum(jnp.cumsum(first) - 1, 0).astype(jnp.int32)
    run_e = jnp.concatenate([jnp.sort(jnp.where(counts > 0, experts, N_EXPERTS)),
                             jnp.full((W_STAGES,), N_EXPERTS, jnp.int32)]).astype(jnp.int32)
    return tok_words, row_w, block_e, first, run_idx, run_e, n_valid.reshape(1), pos_words


def kernel(x, positions, w_in, b_gate, conv_w, conv_b, w_rg_a, b_rg_a, w_rg_x, b_rg_x, lru_lambda, w_attn_proj, w_rec_proj, w_out, ln1_g, ln1_b, w_router_group, b_router_group, w_router_expert, b_router_expert, w_gate, w_up, w_down, ln2_g, ln2_b):
    B, S, D = x.shape
    T = B * S
    h = x.reshape(T, D)
    c, sa, sb = _rope_tables(positions)
    for layer in range(DEPTH):
        xb = h.astype(BF16)
        os_, lses = [], []
        for g, (_, dilation) in enumerate(ATTN_GROUPS):
            qkv = _qkv_proj(xb, w_in[layer], c, sa, sb, g, dilation)
            o, lse = _attention_group(qkv, B, S, dilation)
            os_.append(o)
            lses.append(lse)
        rest = _rest_proj(xb, w_in[layer])
        rec = _rglru(rest, B, S, conv_w[layer], conv_b[layer], w_rg_a[layer].astype(BF16), b_rg_a[layer],
                     w_rg_x[layer].astype(BF16), b_rg_x[layer], lru_lambda[layer])
        merged = _merge(os_, lses, rec, rest, b_gate[layer], w_attn_proj[layer].astype(BF16),
                        w_rec_proj[layer].astype(BF16))
        w_router = jnp.concatenate([w_router_group[layer], w_router_expert[layer]], axis=1)
        w_router = jnp.pad(w_router, ((0, 0), (0, LANES - w_router.shape[1])))
        b_router = jnp.concatenate([b_router_group[layer], b_router_expert[layer]])
        b_router = jnp.pad(b_router, (0, LANES - b_router.shape[0])).reshape(1, LANES)
        mix = _out_matmul(merged, w_out[layer].astype(BF16))
        h1, h1p, logits = _ln1_router(mix, h, ln1_g[layer], ln1_b[layer], w_router.astype(BF16), b_router)
        tok_words, row_w, block_e, first, run_idx, run_e, n_valid, pos_words = _routing(logits)
        moe_rows = _moe_experts(h1p, tok_words, row_w, block_e, first, run_idx, run_e, n_valid,
                                w_gate[layer], w_up[layer], w_down[layer])
        h = _moe_combine_ln2(h1, moe_rows, pos_words, ln2_g[layer], ln2_b[layer])
    return h.reshape(B, S, D)
```

```python
import functools

import jax
import jax.numpy as jnp
from jax import lax
from jax.experimental import pallas as pl
from jax.experimental.pallas import tpu as pltpu

D_MODEL = 4096
HEAD_DIM = 128
ATTN_GROUPS = ((128, 1), (512, 4), (2048, 16))
HEADS_PER_GROUP = 8
N_ATTN_HEADS = HEADS_PER_GROUP * len(ATTN_GROUPS)
ATTN_WIDTH = N_ATTN_HEADS * HEAD_DIM
ATTN_OUT_WIDTH = HEADS_PER_GROUP * HEAD_DIM
ROT_DIM = HEAD_DIM // 4
ROPE_THETA = 500000.0
ATTN_SPAN = 128
LRU_WIDTH = 2048
LRU_BLOCKS = 8
LRU_BLOCK_W = LRU_WIDTH // LRU_BLOCKS
CONV_WIDTH = 4
RG_C = 8.0
N_GROUPS = 8
EXPERTS_PER_GROUP = 8
N_EXPERTS = N_GROUPS * EXPERTS_PER_GROUP
TOP_K = 2
D_EXPERT = 512
LN_EPS = 1e-5
DEPTH = 1
DEEPNORM_ALPHA = (2 * DEPTH) ** 0.25
REST_OFF = 3 * ATTN_WIDTH
REST_COLS = 2 * LRU_WIDTH + 2 * D_MODEL
RX_COL = 0
RG_COL = LRU_WIDTH
GATE_COL = 2 * LRU_WIDTH

LANES = 128
COL_TILE = 1024
PROJ_TM = 1024
PROJ_TN = 512
VMEM_LIMIT = 56 * 1024 * 1024

F32 = jnp.float32
BF16 = jnp.bfloat16


def _params(*sem, vmem=VMEM_LIMIT):
    return pltpu.CompilerParams(dimension_semantics=sem, vmem_limit_bytes=vmem)


def _rope_table_kernel(pos_ref, invf_ref, c_ref, sa_ref, sb_ref):
    half = ROT_DIM // 2
    ang = pos_ref[...].astype(F32) * invf_ref[...]
    lane = lax.broadcasted_iota(jnp.int32, ang.shape, 1)
    cos = jnp.cos(ang)
    sin = jnp.sin(ang)
    c_ref[...] = jnp.where(lane < ROT_DIM, cos, 1.0)
    sa_ref[...] = jnp.where(lane < half, -sin, 0.0)
    sb_ref[...] = jnp.where((lane >= half) & (lane < ROT_DIM), sin, 0.0)


def _rope_tables(positions):
    T = positions.size
    tm = 2048
    half = ROT_DIM // 2
    inv_freq = jnp.power(jnp.float32(ROPE_THETA), -jnp.arange(half, dtype=F32) * 2.0 / ROT_DIM)
    invf = jnp.zeros((1, LANES), F32).at[0, :ROT_DIM].set(jnp.concatenate([inv_freq, inv_freq]))
    tab = jax.ShapeDtypeStruct((T, LANES), F32)
    return pl.pallas_call(
        _rope_table_kernel,
        grid=(T // tm,),
        in_specs=[pl.BlockSpec((tm, 1), lambda i: (i, 0)), pl.BlockSpec((1, LANES), lambda i: (0, 0))],
        out_specs=[pl.BlockSpec((tm, LANES), lambda i: (i, 0))] * 3,
        out_shape=[tab, tab, tab],
        compiler_params=_params("arbitrary"),
        name="rope_tables",
    )(positions.reshape(T, 1), invf)


def _qkv_proj_kernel(x_ref, w_ref, c_ref, sa_ref, sb_ref, o_ref, wb_ref, acc_a, acc_b, slab_ref, *, d, row_tiles):
    s = pl.program_id(0)
    tiles_per_sect = ATTN_OUT_WIDTH // PROJ_TN
    heads = PROJ_TN // HEAD_DIM
    per = PROJ_TM // d

    @pl.when(s == 0)
    def _():
        acc_b[...] = jnp.zeros(acc_b.shape, F32)

    @pl.when(s % row_tiles == 0)
    def _():
        wb_ref[...] = w_ref[...].astype(BF16)

    def body(cur_ref, prev_ref):
        cur_ref[...] = jnp.dot(x_ref[...], wb_ref[...], preferred_element_type=F32)
        n_prev = jnp.maximum(s - 1, 0) // row_tiles
        sect = n_prev // tiles_per_sect
        rot = sect < 2
        scale = jnp.where(sect == 0, HEAD_DIM ** -0.5, 1.0).astype(F32)
        c = jnp.where(rot, c_ref[...], 1.0)
        sa = jnp.where(rot, sa_ref[...], 0.0)
        sb = jnp.where(rot, sb_ref[...], 0.0)
        for h in range(heads):
            hs = slice(h * HEAD_DIM, (h + 1) * HEAD_DIM)
            t = prev_ref[:, hs]
            u = pltpu.bitcast(t.astype(BF16), jnp.uint32)
            ta = pltpu.bitcast(pltpu.roll(u, HEAD_DIM - ROT_DIM // 2, 1), BF16).astype(F32)
            tb = pltpu.bitcast(pltpu.roll(u, ROT_DIM // 2, 1), BF16).astype(F32)
            r = (t * c + ta * sa + tb * sb) * scale
            if d == 1:
                o_ref[:, hs] = r.astype(o_ref.dtype)
            else:
                slab_ref[h] = r
                for res in range(d):
                    rows = slab_ref[h, pl.ds(res, per, stride=d), :]
                    o_ref[res * per:(res + 1) * per, hs] = rows.astype(o_ref.dtype)

    @pl.when(s % 2 == 0)
    def _():
        body(acc_a, acc_b)

    @pl.when(s % 2 == 1)
    def _():
        body(acc_b, acc_a)


def _qkv_proj(xb, w, c, sa, sb, g, d):
    T = xb.shape[0]
    tiles_per_sect = ATTN_OUT_WIDTH // PROJ_TN
    sect_tiles = ATTN_WIDTH // PROJ_TN
    row_tiles = T // PROJ_TM
    n_tiles = 3 * tiles_per_sect * row_tiles
    wcol = lambda n: (n // tiles_per_sect) * sect_tiles + g * tiles_per_sect + n % tiles_per_sect
    cur = lambda s: jnp.minimum(s, n_tiles - 1)
    fin = lambda s: jnp.maximum(s - 1, 0)
    tab_spec = pl.BlockSpec((PROJ_TM, LANES), lambda s: (fin(s) % row_tiles, 0))
    acc = pltpu.VMEM((PROJ_TM, PROJ_TN), F32)
    return pl.pallas_call(
        functools.partial(_qkv_proj_kernel, d=d, row_tiles=row_tiles),
        grid=(n_tiles + 1,),
        in_specs=[pl.BlockSpec((PROJ_TM, D_MODEL), lambda s: (cur(s) % row_tiles, 0)),
                  pl.BlockSpec((D_MODEL, PROJ_TN), lambda s: (0, wcol(cur(s) // row_tiles))),
                  tab_spec, tab_spec, tab_spec],
        out_specs=pl.BlockSpec((PROJ_TM, PROJ_TN), lambda s: (fin(s) % row_tiles, fin(s) // row_tiles)),
        out_shape=jax.ShapeDtypeStruct((T, 3 * ATTN_OUT_WIDTH), BF16),
        scratch_shapes=[pltpu.VMEM((D_MODEL, PROJ_TN), BF16), acc, acc,
                        pltpu.VMEM((PROJ_TN // HEAD_DIM, PROJ_TM, LANES), F32)],
        compiler_params=_params("arbitrary"),
        name=f"qkv_proj_d{d}",
    )(xb, w, c, sa, sb)


def _rest_proj_kernel(x_ref, w_ref, o_ref, wb_ref):
    @pl.when(pl.program_id(1) == 0)
    def _():
        wb_ref[...] = w_ref[...].astype(BF16)

    o_ref[...] = jnp.dot(x_ref[...], wb_ref[...], preferred_element_type=F32).astype(o_ref.dtype)


def _rest_proj(xb, w):
    T = xb.shape[0]
    first = REST_OFF // PROJ_TN
    return pl.pallas_call(
        _rest_proj_kernel,
        grid=(REST_COLS // PROJ_TN, T // PROJ_TM),
        in_specs=[pl.BlockSpec((PROJ_TM, D_MODEL), lambda n, m: (m, 0)),
                  pl.BlockSpec((D_MODEL, PROJ_TN), lambda n, m: (0, first + n))],
        out_specs=pl.BlockSpec((PROJ_TM, PROJ_TN), lambda n, m: (m, n)),
        out_shape=jax.ShapeDtypeStruct((T, REST_COLS), BF16),
        scratch_shapes=[pltpu.VMEM((D_MODEL, PROJ_TN), BF16)],
        compiler_params=_params("arbitrary", "arbitrary"),
        name="rest_proj",
    )(xb, w)


def _attn_kernel(*refs, npc, d):
    q_refs = refs[:npc]
    kp_refs, kc_refs = refs[npc:2 * npc], refs[2 * npc:3 * npc]
    vp_refs, vc_refs = refs[3 * npc:4 * npc], refs[4 * npc:5 * npc]
    o_ref, lse_ref = refs[5 * npc:]
    n = pl.program_id(1)
    res_class = pl.program_id(2)
    nh, blk = HEADS_PER_GROUP, ATTN_SPAN
    rows_out = slice(None) if d == 1 else pl.ds(res_class, blk, stride=d)

    def head(pieces, h):
        hs = slice(h * HEAD_DIM, (h + 1) * HEAD_DIM)
        parts = [r[:, hs] for r in pieces]
        return parts[0] if len(parts) == 1 else jnp.concatenate(parts, axis=0)

    dn = (((1,), (1,)), ((), ()))
    qs = [head(q_refs, h) for h in range(nh)]
    sp = jnp.concatenate([lax.dot_general(qs[h], head(kp_refs, h), dn, preferred_element_type=F32)
                          for h in range(nh)], axis=0)
    sc = jnp.concatenate([lax.dot_general(qs[h], head(kc_refs, h), dn, preferred_element_type=F32)
                          for h in range(nh)], axis=0)
    row = lax.broadcasted_iota(jnp.int32, sp.shape, 0) & (blk - 1)
    col = lax.broadcasted_iota(jnp.int32, sp.shape, 1)
    sp = jnp.where(jnp.logical_and(col >= row, n > 0), sp, -jnp.inf)
    sc = jnp.where(col <= row, sc, -jnp.inf)
    m = jnp.maximum(jnp.max(sp, axis=1, keepdims=True), jnp.max(sc, axis=1, keepdims=True))
    p = jnp.concatenate([jnp.exp(sp - m).astype(BF16), jnp.exp(sc - m).astype(BF16)], axis=1)
    ones = jnp.ones((2 * blk, LANES), BF16)
    lane = lax.broadcasted_iota(jnp.int32, (blk, LANES), 1)
    lse_all = jnp.zeros((blk, LANES), F32)
    for h in range(nh):
        rs = slice(h * blk, (h + 1) * blk)
        v_ext = jnp.concatenate([jnp.concatenate([head(vp_refs, h), head(vc_refs, h)], axis=0), ones], axis=1)
        res = jnp.dot(p[rs], v_ext, preferred_element_type=F32)
        l = res[:, LANES:]
        o_ref[h, rows_out, :] = res[:, :LANES] * (1.0 / l)
        lse_all = jnp.where(lane == h, m[rs] + jnp.log(l), lse_all)
    lse_ref[rows_out, :] = lse_all


def _attention_group(qkv, batch, seq, dilation):
    L = seq // dilation
    per = PROJ_TM // dilation
    pr = min(per, ATTN_SPAN)
    npc = ATTN_SPAN // pr
    nblk = L // ATTN_SPAN
    tiles_per_batch = seq // PROJ_TM
    qcol, kcol, vcol = 0, 1, 2

    def piece_block(b, res, n, p):
        pos = n * ATTN_SPAN + p * pr
        row = (b * tiles_per_batch + pos // per) * PROJ_TM + res * per + pos % per
        return row // pr

    def specs(col, back):
        return [pl.BlockSpec((pr, COL_TILE),
                             lambda b, n, r, p=p: (piece_block(b, r, jnp.maximum(n - back, 0), p), col))
                for p in range(npc)]

    in_specs = specs(qcol, 0) + specs(kcol, 1) + specs(kcol, 0) + specs(vcol, 1) + specs(vcol, 0)
    tok_rows = ATTN_SPAN * dilation
    return pl.pallas_call(
        functools.partial(_attn_kernel, npc=npc, d=dilation),
        grid=(batch, nblk, dilation),
        in_specs=in_specs,
        out_specs=[pl.BlockSpec((HEADS_PER_GROUP, tok_rows, HEAD_DIM), lambda b, n, r: (0, b * nblk + n, 0)),
                   pl.BlockSpec((tok_rows, LANES), lambda b, n, r: (b * nblk + n, 0))],
        out_shape=[jax.ShapeDtypeStruct((HEADS_PER_GROUP, batch * seq, HEAD_DIM), F32),
                   jax.ShapeDtypeStruct((batch * seq, LANES), F32)],
        compiler_params=_params("parallel", "arbitrary", "arbitrary"),
        name=f"attn_d{dilation}",
    )(*([qkv] * (5 * npc)))


_TAIL = 8


def _rglru_kernel(rx_ref, rg_ref, cw_ref, cb_ref, wa_ref, ba_ref, wx_ref, bx_ref, lam_ref, o_ref,
                  xbuf, a_buf, h_buf, carry):
    tt = rx_ref.shape[0]
    t_idx = pl.program_id(2)

    @pl.when(t_idx == 0)
    def _():
        xbuf[0:_TAIL, :] = jnp.zeros((_TAIL, xbuf.shape[1]), F32)
        carry[...] = jnp.zeros(carry.shape, F32)

    x = rx_ref[...].astype(F32)
    xbuf[_TAIL:_TAIL + tt, :] = x
    xr = cb_ref[...] + cw_ref[0:1, :] * xbuf[_TAIL - 3:_TAIL - 3 + tt, :]
    for j in range(1, CONV_WIDTH):
        xr = xr + cw_ref[j:j + 1, :] * xbuf[_TAIL - 3 + j:_TAIL - 3 + j + tt, :]
    xbuf[0:_TAIL, :] = x[tt - _TAIL:tt, :]

    softplus_neg_lam = jnp.log1p(jnp.exp(-lam_ref[...]))
    for blk in range(xr.shape[1] // LRU_BLOCK_W):
        cs = slice(blk * LRU_BLOCK_W, (blk + 1) * LRU_BLOCK_W)
        xb = xr[:, cs]
        xb16 = xb.astype(BF16)
        r = jax.nn.sigmoid(jnp.dot(xb16, wa_ref[blk], preferred_element_type=F32) + ba_ref[:, cs])
        i = jax.nn.sigmoid(jnp.dot(xb16, wx_ref[blk], preferred_element_type=F32) + bx_ref[:, cs])
        log_a = (-RG_C) * r * softplus_neg_lam[:, cs]
        a = jnp.exp(log_a)
        a_buf[:, cs] = a
        h_buf[:, cs] = jnp.sqrt(1.0 - a * a) * (i * xb)

    def step(t, h):
        h = a_buf[pl.ds(t, 1), :] * h + h_buf[pl.ds(t, 1), :]
        h_buf[pl.ds(t, 1), :] = h
        return h

    carry[...] = lax.fori_loop(0, tt, step, carry[...], unroll=8)
    o_ref[...] = (h_buf[...] * jax.nn.gelu(rg_ref[...].astype(F32))).astype(o_ref.dtype)


def _rglru(rest, batch, seq, conv_w, conv_b, wa, ba, wx, bx, lam):
    tt = 512
    nt = seq // tt
    hw = COL_TILE
    assert hw % LRU_BLOCK_W == 0 and RX_COL % hw == 0 and RG_COL % hw == 0
    gb = hw // LRU_BLOCK_W
    row = lambda off: pl.BlockSpec((tt, hw), lambda b, c, t: (b * nt + t, off // hw + c))
    vec = pl.BlockSpec((1, hw), lambda b, c, t: (0, c))
    wspec = pl.BlockSpec((gb, LRU_BLOCK_W, LRU_BLOCK_W), lambda b, c, t: (c, 0, 0))
    return pl.pallas_call(
        _rglru_kernel,
        grid=(batch, LRU_WIDTH // hw, nt),
        in_specs=[row(RX_COL), row(RG_COL), pl.BlockSpec((CONV_WIDTH, hw), lambda b, c, t: (0, c)), vec,
                  wspec, vec, wspec, vec, vec],
        out_specs=pl.BlockSpec((tt, hw), lambda b, c, t: (b * nt + t, c)),
        out_shape=jax.ShapeDtypeStruct((batch * seq, LRU_WIDTH), BF16),
        scratch_shapes=[pltpu.VMEM((tt + _TAIL, hw), F32), pltpu.VMEM((tt, hw), F32),
                        pltpu.VMEM((tt, hw), F32), pltpu.VMEM((1, hw), F32)],
        compiler_params=_params("parallel", "parallel", "arbitrary"),
        name="rg_lru",
    )(rest, rest, conv_w, conv_b.reshape(1, -1), wa, ba.reshape(1, -1), wx, bx.reshape(1, -1), lam.reshape(1, -1))


def _merge_kernel(*refs):
    o0_ref, o1_ref, o2_ref, l0_ref, l1_ref, l2_ref, rec_ref = refs[:7]
    nct = D_MODEL // COL_TILE
    ga_refs = refs[7:7 + nct]
    gr_refs = refs[7 + nct:7 + 2 * nct]
    bg_ref, wa_ref, wr_ref, out_ref = refs[7 + 2 * nct:]
    l0, l1, l2 = l0_ref[...], l1_ref[...], l2_ref[...]
    parts = []
    for h in range(HEADS_PER_GROUP):
        a0, a1, a2 = l0[:, h:h + 1], l1[:, h:h + 1], l2[:, h:h + 1]
        m = jnp.maximum(jnp.maximum(a0, a1), a2)
        e0, e1, e2 = jnp.exp(a0 - m), jnp.exp(a1 - m), jnp.exp(a2 - m)
        inv = 1.0 / (e0 + e1 + e2)
        mixed = (e0 * inv) * o0_ref[h] + (e1 * inv) * o1_ref[h] + (e2 * inv) * o2_ref[h]
        parts.append(mixed.astype(BF16))
    attn = jnp.concatenate(parts, axis=1)
    rec = rec_ref[...]
    for j in range(nct):
        cs = slice(j * COL_TILE, (j + 1) * COL_TILE)
        ya = jnp.dot(attn, wa_ref[:, cs], preferred_element_type=F32)
        yr = jnp.dot(rec, wr_ref[:, cs], preferred_element_type=F32)
        gate_a = jax.nn.sigmoid(ga_refs[j][...].astype(F32) + bg_ref[0:1, cs])
        gate_r = jax.nn.sigmoid(gr_refs[j][...].astype(F32) + bg_ref[1:2, cs])
        out_ref[:, cs] = (gate_a * ya + gate_r * yr).astype(out_ref.dtype)


def _merge(os_, lses, rec, rest, b_gate, wa, wr):
    T = rec.shape[0]
    tm = 256
    nct = D_MODEL // COL_TILE
    row = lambda m: (m, 0)
    const = lambda m: (0, 0)
    gate_specs = [pl.BlockSpec((tm, COL_TILE), lambda m, c=(GATE_COL + k * D_MODEL) // COL_TILE + j: (m, c))
                  for k in range(2) for j in range(nct)]
    o_spec = pl.BlockSpec((HEADS_PER_GROUP, tm, HEAD_DIM), lambda m: (0, m, 0))
    l_spec = pl.BlockSpec((tm, LANES), row)
    resident = lambda shape: pl.BlockSpec(shape, const, pipeline_mode=pl.Buffered(1))
    return pl.pallas_call(
        _merge_kernel,
        grid=(T // tm,),
        in_specs=[o_spec, o_spec, o_spec, l_spec, l_spec, l_spec, pl.BlockSpec((tm, LRU_WIDTH), row)]
                 + gate_specs
                 + [resident((2, D_MODEL)), resident((ATTN_OUT_WIDTH, D_MODEL)), resident((LRU_WIDTH, D_MODEL))],
        out_specs=pl.BlockSpec((tm, D_MODEL), row),
        out_shape=jax.ShapeDtypeStruct((T, D_MODEL), BF16),
        compiler_params=_params("parallel"),
        name="merge",
    )(*os_, *lses, rec, *([rest] * (2 * nct)), b_gate, wa, wr)


def _layer_norm(z, g, b):
    mu = jnp.mean(z, axis=-1, keepdims=True)
    zc = z - mu
    var = jnp.mean(zc * zc, axis=-1, keepdims=True)
    return zc * lax.rsqrt(var + LN_EPS) * g + b


_HI_MASK = 0xFFFF0000
SLAB_ROWS = D_MODEL // 2 // LANES


def _pack_bf16_pairs(y):
    bits = lax.bitcast_convert_type(y.astype(BF16).astype(F32), jnp.uint32)
    c = y.shape[1] // 2
    return (bits[:, :c] >> 16) | (bits[:, c:] & jnp.uint32(_HI_MASK))


def _unpack_bf16_pairs(p):
    lo = lax.bitcast_convert_type(p << 16, F32)
    hi = lax.bitcast_convert_type(p & jnp.uint32(_HI_MASK), F32)
    return lo, hi


def _out_matmul_kernel(mg_ref, w_ref, o_ref):
    o_ref[...] = jnp.dot(mg_ref[...], w_ref[...], preferred_element_type=F32).astype(o_ref.dtype)


def _out_matmul(merged, w_out):
    T = merged.shape[0]
    tm, tn = 1024, 1024
    return pl.pallas_call(
        _out_matmul_kernel,
        grid=(T // tm, D_MODEL // tn),
        in_specs=[pl.BlockSpec((tm, D_MODEL), lambda m, n: (m, 0)),
                  pl.BlockSpec((D_MODEL, tn), lambda m, n: (0, n))],
        out_specs=pl.BlockSpec((tm, tn), lambda m, n: (m, n)),
        out_shape=jax.ShapeDtypeStruct((T, D_MODEL), BF16),
        compiler_params=_params("parallel", "arbitrary"),
        name="out_matmul",
    )(merged, w_out)


def _ln1_kernel(mix_ref, x_ref, g_ref, b_ref, wr_ref, br_ref, h_ref, hp_ref, lg_ref):
    z = DEEPNORM_ALPHA * x_ref[...] + mix_ref[...].astype(F32)
    y = _layer_norm(z, g_ref[...], b_ref[...])
    h_ref[...] = y
    packed = _pack_bf16_pairs(y)
    for c in range(SLAB_ROWS):
        hp_ref[pl.ds(c, y.shape[0], stride=SLAB_ROWS), :] = packed[:, c * LANES:(c + 1) * LANES]
    lg_ref[...] = jnp.dot(y.astype(BF16), wr_ref[...], preferred_element_type=F32) + br_ref[...]


def _ln1_router(mix, x2, g, b, wr, br):
    T = mix.shape[0]
    tm = 256
    row = lambda m: (m, 0)
    const = lambda m: (0, 0)
    return pl.pallas_call(
        _ln1_kernel,
        grid=(T // tm,),
        in_specs=[pl.BlockSpec((tm, D_MODEL), row), pl.BlockSpec((tm, D_MODEL), row),
                  pl.BlockSpec((1, D_MODEL), const), pl.BlockSpec((1, D_MODEL), const),
                  pl.BlockSpec((D_MODEL, LANES), const), pl.BlockSpec((1, LANES), const)],
        out_specs=[pl.BlockSpec((tm, D_MODEL), row), pl.BlockSpec((tm * SLAB_ROWS, LANES), row),
                   pl.BlockSpec((tm, LANES), row)],
        out_shape=[jax.ShapeDtypeStruct((T, D_MODEL), F32), jax.ShapeDtypeStruct((T * SLAB_ROWS, LANES), jnp.uint32),
                   jax.ShapeDtypeStruct((T, LANES), F32)],
        compiler_params=_params("parallel"),
        name="ln1_router",
    )(mix, x2, g.reshape(1, -1), b.reshape(1, -1), wr, br)


MOE_TM = 256
W_PARTS = 4
W_STAGES = 2


def _moe_up_kernel(be_ref, nv_ref, tok_ref, h_hbm, wg_ref, wu_ref, hid_ref, xbuf, sem):
    b = pl.program_id(0)
    nv = nv_ref[0]
    words = MOE_TM // 2

    def row_copy(tok, slot, r):
        return pltpu.make_async_copy(h_hbm.at[tok], xbuf.at[slot, pl.ds(r * SLAB_ROWS, SLAB_ROWS), :], sem.at[slot])

    def start_pair(blk, slot, w):
        word = tok_ref[blk * words + w]
        row_copy(word & 0xFFFF, slot, 2 * w).start()
        row_copy((word >> 16) & 0xFFFF, slot, 2 * w + 1).start()

    def start_gather_loop(blk, slot):
        def body(w, c):
            start_pair(blk, slot, w)
            return c
        lax.fori_loop(0, words, body, 0, unroll=4)

    def start_gather_inline(blk, slot):
        for w in range(words):
            start_pair(blk, slot, w)

    def wait_gather(slot):
        def body(r, c):
            row_copy(0, slot, r).wait()
            return c
        lax.fori_loop(0, MOE_TM, body, 0, unroll=8)

    def compute(slot):
        halves = [_unpack_bf16_pairs(xbuf[slot, pl.ds(c, MOE_TM, stride=SLAB_ROWS), :]) for c in range(SLAB_ROWS)]
        x = jnp.concatenate([lo.astype(BF16) for lo, _ in halves] + [hi.astype(BF16) for _, hi in halves], axis=1)
        gate = jnp.dot(x, wg_ref[0].astype(BF16), preferred_element_type=F32)
        up = jnp.dot(x, wu_ref[0].astype(BF16), preferred_element_type=F32)
        hid_ref[...] = (gate * jax.nn.sigmoid(gate) * up).astype(hid_ref.dtype)

    @pl.when(b == 0)
    def _():
        start_gather_loop(0, 0)

    @pl.when(b + 1 < nv)
    def _():
        wait_gather(b % 2)
        start_gather_inline(b + 1, (b + 1) % 2)
        compute(b % 2)

    @pl.when(b + 1 == nv)
    def _():
        wait_gather(b % 2)
        compute(b % 2)

    @pl.when(b >= nv)
    def _():
        hid_ref[...] = jnp.zeros(hid_ref.shape, hid_ref.dtype)


def _expert_weight_copies(srcs, e, stage_ref, st, sem):
    copies = []
    for which, src in enumerate(srcs):
        rows = src.shape[1] // W_PARTS
        for p in range(W_PARTS):
            rs = pl.ds(p * rows, rows)
            copies.append(pltpu.make_async_copy(src.at[e, rs, :], stage_ref.at[st, which, rs, :], sem.at[st]))
    return copies


def _expert_weight_pipeline(b, first_ref, run_ref, rune_ref, srcs, stage_ref, bf16_refs, sem):
    @pl.when(b == 0)
    def _():
        for st in range(W_STAGES):
            @pl.when(rune_ref[st] < N_EXPERTS)
            def _(st=st):
                for c in _expert_weight_copies(srcs, rune_ref[st], stage_ref, st, sem):
                    c.start()

    @pl.when(first_ref[b] == 1)
    def _():
        k = run_ref[b]
        st = k % W_STAGES
        for c in _expert_weight_copies(srcs, 0, stage_ref, st, sem):
            c.wait()
        for which, dst in enumerate(bf16_refs):
            rows = dst.shape[0] // W_PARTS
            for p in range(W_PARTS):
                rs = slice(p * rows, (p + 1) * rows)
                dst[rs, :] = stage_ref[st, which, rs, :].astype(BF16)
        e_next = rune_ref[k + W_STAGES]

        @pl.when(e_next < N_EXPERTS)
        def _():
            for c in _expert_weight_copies(srcs, e_next, stage_ref, st, sem):
                c.start()


def _moe_down_kernel(first_ref, run_ref, rune_ref, nv_ref, hid_ref, rw_ref, wd_hbm, out_ref, wstage, wdb, wsem):
    b = pl.program_id(0)
    nv = nv_ref[0]
    _expert_weight_pipeline(b, first_ref, run_ref, rune_ref, (wd_hbm,), wstage, (wdb,), wsem)

    @pl.when(b < nv)
    def _():
        y = jnp.dot(hid_ref[...], wdb[...], preferred_element_type=F32)
        out_ref[...] = _pack_bf16_pairs(y * rw_ref[...])

    @pl.when(b >= nv)
    def _():
        out_ref[...] = jnp.zeros(out_ref.shape, out_ref.dtype)


def _moe_experts(h1p, tok_words, row_w, block_e, first, run_idx, run_e, n_valid, wg, wu, wd):
    rows = row_w.shape[0]
    nb = rows // MOE_TM
    hid = pl.pallas_call(
        _moe_up_kernel,
        grid_spec=pltpu.PrefetchScalarGridSpec(
            num_scalar_prefetch=3,
            grid=(nb,),
            in_specs=[pl.BlockSpec(memory_space=pl.ANY),
                      pl.BlockSpec((1, D_MODEL, D_EXPERT), lambda b, be, nv, tok: (be[b], 0, 0)),
                      pl.BlockSpec((1, D_MODEL, D_EXPERT), lambda b, be, nv, tok: (be[b], 0, 0))],
            out_specs=pl.BlockSpec((MOE_TM, D_EXPERT), lambda b, be, nv, tok: (b, 0)),
            scratch_shapes=[pltpu.VMEM((2, MOE_TM * SLAB_ROWS, LANES), jnp.uint32), pltpu.SemaphoreType.DMA((2,))]),
        out_shape=jax.ShapeDtypeStruct((rows, D_EXPERT), BF16),
        compiler_params=_params("arbitrary"),
        name="moe_up",
    )(block_e, n_valid, tok_words, h1p.reshape(-1, SLAB_ROWS, LANES), wg, wu)
    return pl.pallas_call(
        _moe_down_kernel,
        grid_spec=pltpu.PrefetchScalarGridSpec(
            num_scalar_prefetch=4,
            grid=(nb,),
            in_specs=[pl.BlockSpec((MOE_TM, D_EXPERT), lambda b, *_: (b, 0)),
                      pl.BlockSpec((MOE_TM, 1), lambda b, *_: (b, 0)),
                      pl.BlockSpec(memory_space=pl.ANY)],
            out_specs=pl.BlockSpec((MOE_TM, D_MODEL // 2), lambda b, *_: (b, 0)),
            scratch_shapes=[pltpu.VMEM((W_STAGES, 1, D_EXPERT, D_MODEL), F32), pltpu.VMEM((D_EXPERT, D_MODEL), BF16),
                            pltpu.SemaphoreType.DMA((W_STAGES,))]),
        out_shape=jax.ShapeDtypeStruct((rows, D_MODEL // 2), jnp.uint32),
        compiler_params=_params("arbitrary"),
        name="moe_down",
    )(first, run_idx, run_e, n_valid, hid, row_w.reshape(rows, 1), wd)


FIN_TM = 256


def _final_kernel(pos_ref, h_ref, y_hbm, g_ref, b_ref, out_ref, ybuf, sem):
    i = pl.program_id(0)
    n = pl.num_programs(0)

    def row_copy(src_row, slot, r):
        return pltpu.make_async_copy(y_hbm.at[pl.ds(src_row, 1), :], ybuf.at[slot, pl.ds(r, 1), :], sem.at[slot])

    def start_token(tile, slot, r):
        word = pos_ref[tile * FIN_TM + r]
        row_copy(word & 0xFFFF, slot, r).start()
        row_copy((word >> 16) & 0xFFFF, slot, FIN_TM + r).start()

    def start_gather_loop(tile, slot):
        def body(r, c):
            start_token(tile, slot, r)
            return c
        lax.fori_loop(0, FIN_TM, body, 0, unroll=4)

    def start_gather_inline(tile, slot):
        for r in range(FIN_TM):
            start_token(tile, slot, r)

    def wait_gather(slot):
        def body(r, c):
            row_copy(0, slot, r).wait()
            return c
        lax.fori_loop(0, TOP_K * FIN_TM, body, 0, unroll=8)

    def compute(slot):
        ff_lo, ff_hi = _unpack_bf16_pairs(ybuf[slot, 0:FIN_TM, :])
        for k in range(1, TOP_K):
            lo, hi = _unpack_bf16_pairs(ybuf[slot, k * FIN_TM:(k + 1) * FIN_TM, :])
            ff_lo, ff_hi = ff_lo + lo, ff_hi + hi
        z = DEEPNORM_ALPHA * h_ref[...] + jnp.concatenate([ff_lo, ff_hi], axis=1)
        out_ref[...] = _layer_norm(z, g_ref[...], b_ref[...])

    @pl.when(i == 0)
    def _():
        start_gather_loop(0, 0)

    @pl.when(i + 1 < n)
    def _():
        wait_gather(i % 2)
        start_gather_inline(i + 1, (i + 1) % 2)
        compute(i % 2)

    @pl.when(i + 1 == n)
    def _():
        wait_gather(i % 2)
        compute(i % 2)


def _moe_combine_ln2(h1, moe_rows, pos_words, g, b):
    T = h1.shape[0]
    assert TOP_K == 2
    const = lambda i, pos: (0, 0)
    grid_spec = pltpu.PrefetchScalarGridSpec(
        num_scalar_prefetch=1,
        grid=(T // FIN_TM,),
        in_specs=[pl.BlockSpec((FIN_TM, D_MODEL), lambda i, pos: (i, 0)),
                  pl.BlockSpec(memory_space=pl.ANY),
                  pl.BlockSpec((1, D_MODEL), const), pl.BlockSpec((1, D_MODEL), const)],
        out_specs=pl.BlockSpec((FIN_TM, D_MODEL), lambda i, pos: (i, 0)),
        scratch_shapes=[pltpu.VMEM((2, TOP_K * FIN_TM, D_MODEL // 2), jnp.uint32), pltpu.SemaphoreType.DMA((2,))],
    )
    return pl.pallas_call(
        _final_kernel,
        grid_spec=grid_spec,
        out_shape=jax.ShapeDtypeStruct((T, D_MODEL), F32),
        compiler_params=_params("arbitrary"),
        name="moe_combine_ln2",
    )(pos_words, h1, moe_rows, g.reshape(1, -1), b.reshape(1, -1))


def _pack_u16_pairs(lo, hi):
    return (lo.astype(jnp.int32) | (hi.astype(jnp.int32) << 16)).astype(jnp.int32)


def _routing(logits):
    T = logits.shape[0]
    logits_g = logits[:, :N_GROUPS]
    logits_e = logits[:, N_GROUPS:N_GROUPS + N_EXPERTS].reshape(T, N_GROUPS, EXPERTS_PER_GROUP)
    p_g = jax.nn.softmax(logits_g, axis=-1)
    g_idx = jnp.argmax(logits_g, axis=-1)
    p_grp = jnp.take_along_axis(p_g, g_idx[:, None], axis=-1)
    le = jnp.take_along_axis(logits_e, g_idx[:, None, None], axis=1)[:, 0]
    top_v, top_i = lax.top_k(le, TOP_K)
    weight = p_grp * jax.nn.softmax(top_v, axis=-1)
    expert_id = g_idx[:, None].astype(jnp.int32) * EXPERTS_PER_GROUP + top_i.astype(jnp.int32)

    A = T * TOP_K
    flat_e = expert_id.reshape(-1)
    flat_w = weight.reshape(-1)
    experts = jnp.arange(N_EXPERTS, dtype=jnp.int32)
    counts = jnp.sum((flat_e[:, None] == experts[None, :]).astype(jnp.int32), axis=0)
    starts = jnp.cumsum(counts) - counts
    padded = (counts + MOE_TM - 1) // MOE_TM * MOE_TM
    pends = jnp.cumsum(padded)
    pstarts = pends - padded
    order = jnp.argsort(flat_e, stable=True).astype(jnp.int32)
    inv = jnp.argsort(order).astype(jnp.int32)
    pos = (inv + (pstarts - starts)[flat_e]).astype(jnp.int32)
    nb = A // MOE_TM + N_EXPERTS
    assert nb * MOE_TM < 2 ** 16 and T < 2 ** 16
    n_valid = (pends[-1] // MOE_TM).astype(jnp.int32)
    blk = jnp.arange(nb, dtype=jnp.int32)
    blk_start = jnp.minimum(blk, n_valid - 1) * MOE_TM
    block_e = jnp.minimum(jnp.sum((blk_start[:, None] >= pends[None, :]).astype(jnp.int32), axis=1), N_EXPERTS - 1)
    i_in = (blk * MOE_TM - pstarts[block_e])[:, None] + jnp.arange(MOE_TM, dtype=jnp.int32)[None, :]
    valid = i_in < counts[block_e][:, None]
    a_r = order[jnp.clip(starts[block_e][:, None] + i_in, 0, A - 1)]
    row_tok = jnp.where(valid, a_r // TOP_K, 0).astype(jnp.int32)
    row_w = jnp.where(valid, flat_w[a_r], 0.0).astype(F32).reshape(-1)
    tok_words = _pack_u16_pairs(row_tok[:, 0::2], row_tok[:, 1::2]).reshape(-1)
    pos2 = pos.reshape(T, TOP_K)
    pos_words = _pack_u16_pairs(pos2[:, 0], pos2[:, 1])
    prev_e = jnp.concatenate([jnp.full((1,), -1, jnp.int32), block_e[:-1]])
    first = jnp.logical_and(block_e != prev_e, blk < n_valid).astype(jnp.int32)
    run_idx = jnp.maximum(jnp.cumsum(first) - 1, 0).astype(jnp.int32)
    run_e = jnp.concatenate([jnp.sort(jnp.where(counts > 0, experts, N_EXPERTS)),
                             jnp.full((W_STAGES,), N_EXPERTS, jnp.int32)]).astype(jnp.int32)
    return tok_words, row_w, block_e, first, run_idx, run_e, n_valid.reshape(1), pos_words


def kernel(x, positions, w_in, b_gate, conv_w, conv_b, w_rg_a, b_rg_a, w_rg_x, b_rg_x, lru_lambda, w_attn_proj, w_rec_proj, w_out, ln1_g, ln1_b, w_router_group, b_router_group, w_router_expert, b_router_expert, w_gate, w_up, w_down, ln2_g, ln2_b):
    B, S, D = x.shape
    T = B * S
    h = x.reshape(T, D)
    c, sa, sb = _rope_tables(positions)
    for layer in range(DEPTH):
        xb = h.astype(BF16)
        os_, lses = [], []
        for g, (_, dilation) in enumerate(ATTN_GROUPS):
            qkv = _qkv_proj(xb, w_in[layer], c, sa, sb, g, dilation)
            o, lse = _attention_group(qkv, B, S, dilation)
            os_.append(o)
            lses.append(lse)
        rest = _rest_proj(xb, w_in[layer])
        rec = _rglru(rest, B, S, conv_w[layer], conv_b[layer], w_rg_a[layer].astype(BF16), b_rg_a[layer],
                     w_rg_x[layer].astype(BF16), b_rg_x[layer], lru_lambda[layer])
        merged = _merge(os_, lses, rec, rest, b_gate[layer], w_attn_proj[layer].astype(BF16),
                        w_rec_proj[layer].astype(BF16))
        w_router = jnp.concatenate([w_router_group[layer], w_router_expert[layer]], axis=1)
        w_router = jnp.pad(w_router, ((0, 0), (0, LANES - w_router.shape[1])))
        b_router = jnp.concatenate([b_router_group[layer], b_router_expert[layer]])
        b_router = jnp.pad(b_router, (0, LANES - b_router.shape[0])).reshape(1, LANES)
        mix = _out_matmul(merged, w_out[layer].astype(BF16))
        h1, h1p, logits = _ln1_router(mix, h, ln1_g[layer], ln1_b[layer], w_router.astype(BF16), b_router)
        tok_words, row_w, block_e, first, run_idx, run_e, n_valid, pos_words = _routing(logits)
        moe_rows = _moe_experts(h1p, tok_words, row_w, block_e, first, run_idx, run_e, n_valid,
                                w_gate[layer], w_up[layer], w_down[layer])
        h = _moe_combine_ln2(h1, moe_rows, pos_words, ln2_g[layer], ln2_b[layer])
    return h.reshape(B, S, D)
```

```python
import functools

import jax
import jax.numpy as jnp
from jax import lax
from jax.experimental import pallas as pl
from jax.experimental.pallas import tpu as pltpu

D_MODEL = 4096
HEAD_DIM = 128
ATTN_GROUPS = ((128, 1), (512, 4), (2048, 16))
HEADS_PER_GROUP = 8
N_ATTN_HEADS = HEADS_PER_GROUP * len(ATTN_GROUPS)
ATTN_WIDTH = N_ATTN_HEADS * HEAD_DIM
ATTN_OUT_WIDTH = HEADS_PER_GROUP * HEAD_DIM
ROT_DIM = HEAD_DIM // 4
ROPE_THETA = 500000.0
ATTN_SPAN = 128
LRU_WIDTH = 2048
LRU_BLOCKS = 8
LRU_BLOCK_W = LRU_WIDTH // LRU_BLOCKS
CONV_WIDTH = 4
RG_C = 8.0
N_GROUPS = 8
EXPERTS_PER_GROUP = 8
N_EXPERTS = N_GROUPS * EXPERTS_PER_GROUP
TOP_K = 2
D_EXPERT = 512
LN_EPS = 1e-5
DEPTH = 1
DEEPNORM_ALPHA = (2 * DEPTH) ** 0.25
REST_OFF = 3 * ATTN_WIDTH
REST_COLS = 2 * LRU_WIDTH + 2 * D_MODEL
RX_COL = 0
RG_COL = LRU_WIDTH
GATE_COL = 2 * LRU_WIDTH

LANES = 128
COL_TILE = 1024
PROJ_TM = 1024
PROJ_TN = 512
VMEM_LIMIT = 56 * 1024 * 1024

F32 = jnp.float32
BF16 = jnp.bfloat16


def _params(*sem, vmem=VMEM_LIMIT):
    return pltpu.CompilerParams(dimension_semantics=sem, vmem_limit_bytes=vmem)


def _rope_table_kernel(pos_ref, invf_ref, c_ref, sa_ref, sb_ref):
    half = ROT_DIM // 2
    ang = pos_ref[...].astype(F32) * invf_ref[...]
    lane = lax.broadcasted_iota(jnp.int32, ang.shape, 1)
    cos = jnp.cos(ang)
    sin = jnp.sin(ang)
    c_ref[...] = jnp.where(lane < ROT_DIM, cos, 1.0)
    sa_ref[...] = jnp.where(lane < half, -sin, 0.0)
    sb_ref[...] = jnp.where((lane >= half) & (lane < ROT_DIM), sin, 0.0)


def _rope_tables(positions):
    T = positions.size
    tm = 2048
    half = ROT_DIM // 2
    inv_freq = jnp.power(jnp.float32(ROPE_THETA), -jnp.arange(half, dtype=F32) * 2.0 / ROT_DIM)
    invf = jnp.zeros((1, LANES), F32).at[0, :ROT_DIM].set(jnp.concatenate([inv_freq, inv_freq]))
    tab = jax.ShapeDtypeStruct((T, LANES), F32)
    return pl.pallas_call(
        _rope_table_kernel,
        grid=(T // tm,),
        in_specs=[pl.BlockSpec((tm, 1), lambda i: (i, 0)), pl.BlockSpec((1, LANES), lambda i: (0, 0))],
        out_specs=[pl.BlockSpec((tm, LANES), lambda i: (i, 0))] * 3,
        out_shape=[tab, tab, tab],
        compiler_params=_params("arbitrary"),
        name="rope_tables",
    )(positions.reshape(T, 1), invf)


def _qkv_proj_kernel(x_ref, w_ref, c_ref, sa_ref, sb_ref, o_ref, wb_ref, acc_a, acc_b, slab_ref, *, d, row_tiles):
    s = pl.program_id(0)
    tiles_per_sect = ATTN_OUT_WIDTH // PROJ_TN
    heads = PROJ_TN // HEAD_DIM
    per = PROJ_TM // d

    @pl.when(s == 0)
    def _():
        acc_b[...] = jnp.zeros(acc_b.shape, F32)

    @pl.when(s % row_tiles == 0)
    def _():
        wb_ref[...] = w_ref[...].astype(BF16)

    def body(cur_ref, prev_ref):
        cur_ref[...] = jnp.dot(x_ref[...], wb_ref[...], preferred_element_type=F32)
        n_prev = jnp.maximum(s - 1, 0) // row_tiles
        sect = n_prev // tiles_per_sect
        rot = sect < 2
        scale = jnp.where(sect == 0, HEAD_DIM ** -0.5, 1.0).astype(F32)
        c = jnp.where(rot, c_ref[...], 1.0)
        sa = jnp.where(rot, sa_ref[...], 0.0)
        sb = jnp.where(rot, sb_ref[...], 0.0)
        for h in range(heads):
            hs = slice(h * HEAD_DIM, (h + 1) * HEAD_DIM)
            t = prev_ref[:, hs]
            u = pltpu.bitcast(t.astype(BF16), jnp.uint32)
            ta = pltpu.bitcast(pltpu.roll(u, HEAD_DIM - ROT_DIM // 2, 1), BF16).astype(F32)
            tb = pltpu.bitcast(pltpu.roll(u, ROT_DIM // 2, 1), BF16).astype(F32)
            r = (t * c + ta * sa + tb * sb) * scale
            if d == 1:
                o_ref[:, hs] = r.astype(o_ref.dtype)
            else:
                slab_ref[h] = r
                for res in range(d):
                    rows = slab_ref[h, pl.ds(res, per, stride=d), :]
                    o_ref[res * per:(res + 1) * per, hs] = rows.astype(o_ref.dtype)

    @pl.when(s % 2 == 0)
    def _():
        body(acc_a, acc_b)

    @pl.when(s % 2 == 1)
    def _():
        body(acc_b, acc_a)


def _qkv_proj(xb, w, c, sa, sb, g, d):
    T = xb.shape[0]
    tiles_per_sect = ATTN_OUT_WIDTH // PROJ_TN
    sect_tiles = ATTN_WIDTH // PROJ_TN
    row_tiles = T // PROJ_TM
    n_tiles = 3 * tiles_per_sect * row_tiles
    wcol = lambda n: (n // tiles_per_sect) * sect_tiles + g * tiles_per_sect + n % tiles_per_sect
    cur = lambda s: jnp.minimum(s, n_tiles - 1)
    fin = lambda s: jnp.maximum(s - 1, 0)
    tab_spec = pl.BlockSpec((PROJ_TM, LANES), lambda s: (fin(s) % row_tiles, 0))
    acc = pltpu.VMEM((PROJ_TM, PROJ_TN), F32)
    return pl.pallas_call(
        functools.partial(_qkv_proj_kernel, d=d, row_tiles=row_tiles),
        grid=(n_tiles + 1,),
        in_specs=[pl.BlockSpec((PROJ_TM, D_MODEL), lambda s: (cur(s) % row_tiles, 0)),
                  pl.BlockSpec((D_MODEL, PROJ_TN), lambda s: (0, wcol(cur(s) // row_tiles))),
                  tab_spec, tab_spec, tab_spec],
        out_specs=pl.BlockSpec((PROJ_TM, PROJ_TN), lambda s: (fin(s) % row_tiles, fin(s) // row_tiles)),
        out_shape=jax.ShapeDtypeStruct((T, 3 * ATTN_OUT_WIDTH), BF16),
        scratch_shapes=[pltpu.VMEM((D_MODEL, PROJ_TN), BF16), acc, acc,
                        pltpu.VMEM((PROJ_TN // HEAD_DIM, PROJ_TM, LANES), F32)],
        compiler_params=_params("arbitrary"),
        name=f"qkv_proj_d{d}",
    )(xb, w, c, sa, sb)


def _rest_proj_kernel(x_ref, w_ref, o_ref, wb_ref):
    @pl.when(pl.program_id(1) == 0)
    def _():
        wb_ref[...] = w_ref[...].astype(BF16)

    o_ref[...] = jnp.dot(x_ref[...], wb_ref[...], preferred_element_type=F32).astype(o_ref.dtype)


def _rest_proj(xb, w):
    T = xb.shape[0]
    first = REST_OFF // PROJ_TN
    return pl.pallas_call(
        _rest_proj_kernel,
        grid=(REST_COLS // PROJ_TN, T // PROJ_TM),
        in_specs=[pl.BlockSpec((PROJ_TM, D_MODEL), lambda n, m: (m, 0)),
                  pl.BlockSpec((D_MODEL, PROJ_TN), lambda n, m: (0, first + n))],
        out_specs=pl.BlockSpec((PROJ_TM, PROJ_TN), lambda n, m: (m, n)),
        out_shape=jax.ShapeDtypeStruct((T, REST_COLS), BF16),
        scratch_shapes=[pltpu.VMEM((D_MODEL, PROJ_TN), BF16)],
        compiler_params=_params("arbitrary", "arbitrary"),
        name="rest_proj",
    )(xb, w)


def _attn_kernel(*refs, npc, d):
    q_refs = refs[:npc]
    kp_refs, kc_refs = refs[npc:2 * npc], refs[2 * npc:3 * npc]
    vp_refs, vc_refs = refs[3 * npc:4 * npc], refs[4 * npc:5 * npc]
    o_ref, lse_ref = refs[5 * npc:]
    n = pl.program_id(1)
    res_class = pl.program_id(2)
    nh, blk = HEADS_PER_GROUP, ATTN_SPAN
    rows_out = slice(None) if d == 1 else pl.ds(res_class, blk, stride=d)

    def head(pieces, h):
        hs = slice(h * HEAD_DIM, (h + 1) * HEAD_DIM)
        parts = [r[:, hs] for r in pieces]
        return parts[0] if len(parts) == 1 else jnp.concatenate(parts, axis=0)

    dn = (((1,), (1,)), ((), ()))
    qs = [head(q_refs, h) for h in range(nh)]
    sp = jnp.concatenate([lax.dot_general(qs[h], head(kp_refs, h), dn, preferred_element_type=F32)
                          for h in range(nh)], axis=0)
    sc = jnp.concatenate([lax.dot_general(qs[h], head(kc_refs, h), dn, preferred_element_type=F32)
                          for h in range(nh)], axis=0)
    row = lax.broadcasted_iota(jnp.int32, sp.shape, 0) & (blk - 1)
    col = lax.broadcasted_iota(jnp.int32, sp.shape, 1)
    sp = jnp.where(jnp.logical_and(col >= row, n > 0), sp, -jnp.inf)
    sc = jnp.where(col <= row, sc, -jnp.inf)
    m = jnp.maximum(jnp.max(sp, axis=1, keepdims=True), jnp.max(sc, axis=1, keepdims=True))
    p = jnp.concatenate([jnp.exp(sp - m).astype(BF16), jnp.exp(sc - m).astype(BF16)], axis=1)
    ones = jnp.ones((2 * blk, LANES), BF16)
    lane = lax.broadcasted_iota(jnp.int32, (blk, LANES), 1)
    lse_all = jnp.zeros((blk, LANES), F32)
    for h in range(nh):
        rs = slice(h * blk, (h + 1) * blk)
        v_ext = jnp.concatenate([jnp.concatenate([head(vp_refs, h), head(vc_refs, h)], axis=0), ones], axis=1)
        res = jnp.dot(p[rs], v_ext, preferred_element_type=F32)
        l = res[:, LANES:]
        o_ref[h, rows_out, :] = res[:, :LANES] * (1.0 / l)
        lse_all = jnp.where(lane == h, m[rs] + jnp.log(l), lse_all)
    lse_ref[rows_out, :] = lse_all


def _attention_group(qkv, batch, seq, dilation):
    L = seq // dilation
    per = PROJ_TM // dilation
    pr = min(per, ATTN_SPAN)
    npc = ATTN_SPAN // pr
    nblk = L // ATTN_SPAN
    tiles_per_batch = seq // PROJ_TM
    qcol, kcol, vcol = 0, 1, 2

    def piece_block(b, res, n, p):
        pos = n * ATTN_SPAN + p * pr
        row = (b * tiles_per_batch + pos // per) * PROJ_TM + res * per + pos % per
        return row // pr

    def specs(col, back):
        return [pl.BlockSpec((pr, COL_TILE),
                             lambda b, n, r, p=p: (piece_block(b, r, jnp.maximum(n - back, 0), p), col))
                for p in range(npc)]

    in_specs = specs(qcol, 0) + specs(kcol, 1) + specs(kcol, 0) + specs(vcol, 1) + specs(vcol, 0)
    tok_rows = ATTN_SPAN * dilation
    return pl.pallas_call(
        functools.partial(_attn_kernel, npc=npc, d=dilation),
        grid=(batch, nblk, dilation),
        in_specs=in_specs,
        out_specs=[pl.BlockSpec((HEADS_PER_GROUP, tok_rows, HEAD_DIM), lambda b, n, r: (0, b * nblk + n, 0)),
                   pl.BlockSpec((tok_rows, LANES), lambda b, n, r: (b * nblk + n, 0))],
        out_shape=[jax.ShapeDtypeStruct((HEADS_PER_GROUP, batch * seq, HEAD_DIM), F32),
                   jax.ShapeDtypeStruct((batch * seq, LANES), F32)],
        compiler_params=_params("parallel", "arbitrary", "arbitrary"),
        name=f"attn_d{dilation}",
    )(*([qkv] * (5 * npc)))


_TAIL = 8


def _rglru_kernel(rx_ref, rg_ref, cw_ref, cb_ref, wa_ref, ba_ref, wx_ref, bx_ref, lam_ref, o_ref,
                  xbuf, a_buf, h_buf, carry):
    tt = rx_ref.shape[0]
    t_idx = pl.program_id(2)

    @pl.when(t_idx == 0)
    def _():
        xbuf[0:_TAIL, :] = jnp.zeros((_TAIL, xbuf.shape[1]), F32)
        carry[...] = jnp.zeros(carry.shape, F32)

    x = rx_ref[...].astype(F32)
    xbuf[_TAIL:_TAIL + tt, :] = x
    xr = cb_ref[...] + cw_ref[0:1, :] * xbuf[_TAIL - 3:_TAIL - 3 + tt, :]
    for j in range(1, CONV_WIDTH):
        xr = xr + cw_ref[j:j + 1, :] * xbuf[_TAIL - 3 + j:_TAIL - 3 + j + tt, :]
    xbuf[0:_TAIL, :] = x[tt - _TAIL:tt, :]

    softplus_neg_lam = jnp.log1p(jnp.exp(-lam_ref[...]))
    for blk in range(xr.shape[1] // LRU_BLOCK_W):
        cs = slice(blk * LRU_BLOCK_W, (blk + 1) * LRU_BLOCK_W)
        xb = xr[:, cs]
        xb16 = xb.astype(BF16)
        r = jax.nn.sigmoid(jnp.dot(xb16, wa_ref[blk], preferred_element_type=F32) + ba_ref[:, cs])
        i = jax.nn.sigmoid(jnp.dot(xb16, wx_ref[blk], preferred_element_type=F32) + bx_ref[:, cs])
        log_a = (-RG_C) * r * softplus_neg_lam[:, cs]
        a = jnp.exp(log_a)
        a_buf[:, cs] = a
        h_buf[:, cs] = jnp.sqrt(1.0 - a * a) * (i * xb)

    def step(t, h):
        h = a_buf[pl.ds(t, 1), :] * h + h_buf[pl.ds(t, 1), :]
        h_buf[pl.ds(t, 1), :] = h
        return h

    carry[...] = lax.fori_loop(0, tt, step, carry[...], unroll=8)
    o_ref[...] = (h_buf[...] * jax.nn.gelu(rg_ref[...].astype(F32))).astype(o_ref.dtype)


def _rglru(rest, batch, seq, conv_w, conv_b, wa, ba, wx, bx, lam):
    tt = 512
    nt = seq // tt
    hw = COL_TILE
    assert hw % LRU_BLOCK_W == 0 and RX_COL % hw == 0 and RG_COL % hw == 0
    gb = hw // LRU_BLOCK_W
    row = lambda off: pl.BlockSpec((tt, hw), lambda b, c, t: (b * nt + t, off // hw + c))
    vec = pl.BlockSpec((1, hw), lambda b, c, t: (0, c))
    wspec = pl.BlockSpec((gb, LRU_BLOCK_W, LRU_BLOCK_W), lambda b, c, t: (c, 0, 0))
    return pl.pallas_call(
        _rglru_kernel,
        grid=(batch, LRU_WIDTH // hw, nt),
        in_specs=[row(RX_COL), row(RG_COL), pl.BlockSpec((CONV_WIDTH, hw), lambda b, c, t: (0, c)), vec,
                  wspec, vec, wspec, vec, vec],
        out_specs=pl.BlockSpec((tt, hw), lambda b, c, t: (b * nt + t, c)),
        out_shape=jax.ShapeDtypeStruct((batch * seq, LRU_WIDTH), BF16),
        scratch_shapes=[pltpu.VMEM((tt + _TAIL, hw), F32), pltpu.VMEM((tt, hw), F32),
                        pltpu.VMEM((tt, hw), F32), pltpu.VMEM((1, hw), F32)],
        compiler_params=_params("parallel", "parallel", "arbitrary"),
        name="rg_lru",
    )(rest, rest, conv_w, conv_b.reshape(1, -1), wa, ba.reshape(1, -1), wx, bx.reshape(1, -1), lam.reshape(1, -1))


def _merge_kernel(*refs):
    o0_ref, o1_ref, o2_ref, l0_ref, l1_ref, l2_ref, rec_ref = refs[:7]
    nct = D_MODEL // COL_TILE
    ga_refs = refs[7:7 + nct]
    gr_refs = refs[7 + nct:7 + 2 * nct]
    bg_ref, wa_ref, wr_ref, out_ref = refs[7 + 2 * nct:]
    l0, l1, l2 = l0_ref[...], l1_ref[...], l2_ref[...]
    parts = []
    for h in range(HEADS_PER_GROUP):
        a0, a1, a2 = l0[:, h:h + 1], l1[:, h:h + 1], l2[:, h:h + 1]
        m = jnp.maximum(jnp.maximum(a0, a1), a2)
        e0, e1, e2 = jnp.exp(a0 - m), jnp.exp(a1 - m), jnp.exp(a2 - m)
        inv = 1.0 / (e0 + e1 + e2)
        mixed = (e0 * inv) * o0_ref[h] + (e1 * inv) * o1_ref[h] + (e2 * inv) * o2_ref[h]
        parts.append(mixed.astype(BF16))
    attn = jnp.concatenate(parts, axis=1)
    rec = rec_ref[...]
    for j in range(nct):
        cs = slice(j * COL_TILE, (j + 1) * COL_TILE)
        ya = jnp.dot(attn, wa_ref[:, cs], preferred_element_type=F32)
        yr = jnp.dot(rec, wr_ref[:, cs], preferred_element_type=F32)
        gate_a = jax.nn.sigmoid(ga_refs[j][...].astype(F32) + bg_ref[0:1, cs])
        gate_r = jax.nn.sigmoid(gr_refs[j][...].astype(F32) + bg_ref[1:2, cs])
        out_ref[:, cs] = (gate_a * ya + gate_r * yr).astype(out_ref.dtype)


def _merge(os_, lses, rec, rest, b_gate, wa, wr):
    T = rec.shape[0]
    tm = 256
    nct = D_MODEL // COL_TILE
    row = lambda m: (m, 0)
    const = lambda m: (0, 0)
    gate_specs = [pl.BlockSpec((tm, COL_TILE), lambda m, c=(GATE_COL + k * D_MODEL) // COL_TILE + j: (m, c))
                  for k in range(2) for j in range(nct)]
    o_spec = pl.BlockSpec((HEADS_PER_GROUP, tm, HEAD_DIM), lambda m: (0, m, 0))
    l_spec = pl.BlockSpec((tm, LANES), row)
    resident = lambda shape: pl.BlockSpec(shape, const, pipeline_mode=pl.Buffered(1))
    return pl.pallas_call(
        _merge_kernel,
        grid=(T // tm,),
        in_specs=[o_spec, o_spec, o_spec, l_spec, l_spec, l_spec, pl.BlockSpec((tm, LRU_WIDTH), row)]
                 + gate_specs
                 + [resident((2, D_MODEL)), resident((ATTN_OUT_WIDTH, D_MODEL)), resident((LRU_WIDTH, D_MODEL))],
        out_specs=pl.BlockSpec((tm, D_MODEL), row),
        out_shape=jax.ShapeDtypeStruct((T, D_MODEL), BF16),
        compiler_params=_params("parallel"),
        name="merge",
    )(*os_, *lses, rec, *([rest] * (2 * nct)), b_gate, wa, wr)


def _layer_norm(z, g, b):
    mu = jnp.mean(z, axis=-1, keepdims=True)
    zc = z - mu
    var = jnp.mean(zc * zc, axis=-1, keepdims=True)
    return zc * lax.rsqrt(var + LN_EPS) * g + b


_HI_MASK = 0xFFFF0000
SLAB_ROWS = D_MODEL // 2 // LANES


def _pack_bf16_pairs(y):
    bits = lax.bitcast_convert_type(y.astype(BF16).astype(F32), jnp.uint32)
    c = y.shape[1] // 2
    return (bits[:, :c] >> 16) | (bits[:, c:] & jnp.uint32(_HI_MASK))


def _unpack_bf16_pairs(p):
    lo = lax.bitcast_convert_type(p << 16, F32)
    hi = lax.bitcast_convert_type(p & jnp.uint32(_HI_MASK), F32)
    return lo, hi


def _out_matmul_kernel(mg_ref, w_ref, o_ref):
    o_ref[...] = jnp.dot(mg_ref[...], w_ref[...], preferred_element_type=F32).astype(o_ref.dtype)


def _out_matmul(merged, w_out):
    T = merged.shape[0]
    tm, tn = 1024, 1024
    return pl.pallas_call(
        _out_matmul_kernel,
        grid=(T // tm, D_MODEL // tn),
        in_specs=[pl.BlockSpec((tm, D_MODEL), lambda m, n: (m, 0)),
                  pl.BlockSpec((D_MODEL, tn), lambda m, n: (0, n))],
        out_specs=pl.BlockSpec((tm, tn), lambda m, n: (m, n)),
        out_shape=jax.ShapeDtypeStruct((T, D_MODEL), BF16),
        compiler_params=_params("parallel", "arbitrary"),
        name="out_matmul",
    )(merged, w_out)


def _ln1_kernel(mix_ref, x_ref, g_ref, b_ref, wr_ref, br_ref, h_ref, hp_ref, lg_ref):
    z = DEEPNORM_ALPHA * x_ref[...] + mix_ref[...].astype(F32)
    y = _layer_norm(z, g_ref[...], b_ref[...])
    h_ref[...] = y
    packed = _pack_bf16_pairs(y)
    for c in range(SLAB_ROWS):
        hp_ref[pl.ds(c, y.shape[0], stride=SLAB_ROWS), :] = packed[:, c * LANES:(c + 1) * LANES]
    lg_ref[...] = jnp.dot(y.astype(BF16), wr_ref[...], preferred_element_type=F32) + br_ref[...]


def _ln1_router(mix, x2, g, b, wr, br):
    T = mix.shape[0]
    tm = 256
    row = lambda m: (m, 0)
    const = lambda m: (0, 0)
    return pl.pallas_call(
        _ln1_kernel,
        grid=(T // tm,),
        in_specs=[pl.BlockSpec((tm, D_MODEL), row), pl.BlockSpec((tm, D_MODEL), row),
                  pl.BlockSpec((1, D_MODEL), const), pl.BlockSpec((1, D_MODEL), const),
                  pl.BlockSpec((D_MODEL, LANES), const), pl.BlockSpec((1, LANES), const)],
        out_specs=[pl.BlockSpec((tm, D_MODEL), row), pl.BlockSpec((tm * SLAB_ROWS, LANES), row),
                   pl.BlockSpec((tm, LANES), row)],
        out_shape=[jax.ShapeDtypeStruct((T, D_MODEL), F32), jax.ShapeDtypeStruct((T * SLAB_ROWS, LANES), jnp.uint32),
                   jax.ShapeDtypeStruct((T, LANES), F32)],
        compiler_params=_params("parallel"),
        name="ln1_router",
    )(mix, x2, g.reshape(1, -1), b.reshape(1, -1), wr, br)


MOE_TM = 256
W_PARTS = 4
W_STAGES = 2


def _moe_up_kernel(be_ref, nv_ref, tok_ref, h_hbm, wg_ref, wu_ref, hid_ref, xbuf, sem):
    b = pl.program_id(0)
    nv = nv_ref[0]
    words = MOE_TM // 2

    def row_copy(tok, slot, r):
        return pltpu.make_async_copy(h_hbm.at[tok], xbuf.at[slot, pl.ds(r * SLAB_ROWS, SLAB_ROWS), :], sem.at[slot])

    def start_pair(blk, slot, w):
        word = tok_ref[blk * words + w]
        row_copy(word & 0xFFFF, slot, 2 * w).start()
        row_copy((word >> 16) & 0xFFFF, slot, 2 * w + 1).start()

    def start_gather_loop(blk, slot):
        def body(w, c):
            start_pair(blk, slot, w)
            return c
        lax.fori_loop(0, words, body, 0, unroll=4)

    def start_gather_inline(blk, slot):
        for w in range(words):
            start_pair(blk, slot, w)

    def wait_gather(slot):
        def body(r, c):
            row_copy(0, slot, r).wait()
            return c
        lax.fori_loop(0, MOE_TM, body, 0, unroll=8)

    def compute(slot):
        halves = [_unpack_bf16_pairs(xbuf[slot, pl.ds(c, MOE_TM, stride=SLAB_ROWS), :]) for c in range(SLAB_ROWS)]
        x = jnp.concatenate([lo.astype(BF16) for lo, _ in halves] + [hi.astype(BF16) for _, hi in halves], axis=1)
        gate = jnp.dot(x, wg_ref[0].astype(BF16), preferred_element_type=F32)
        up = jnp.dot(x, wu_ref[0].astype(BF16), preferred_element_type=F32)
        hid_ref[...] = (gate * jax.nn.sigmoid(gate) * up).astype(hid_ref.dtype)

    @pl.when(b == 0)
    def _():
        start_gather_loop(0, 0)

    @pl.when(b + 1 < nv)
    def _():
        wait_gather(b % 2)
        start_gather_inline(b + 1, (b + 1) % 2)
        compute(b % 2)

    @pl.when(b + 1 == nv)
    def _():
        wait_gather(b % 2)
        compute(b % 2)

    @pl.when(b >= nv)
    def _():
        hid_ref[...] = jnp.zeros(hid_ref.shape, hid_ref.dtype)


def _expert_weight_copies(srcs, e, stage_ref, st, sem):
    copies = []
    for which, src in enumerate(srcs):
        rows = src.shape[1] // W_PARTS
        for p in range(W_PARTS):
            rs = pl.ds(p * rows, rows)
            copies.append(pltpu.make_async_copy(src.at[e, rs, :], stage_ref.at[st, which, rs, :], sem.at[st]))
    return copies


def _expert_weight_pipeline(b, first_ref, run_ref, rune_ref, srcs, stage_ref, bf16_refs, sem):
    @pl.when(b == 0)
    def _():
        for st in range(W_STAGES):
            @pl.when(rune_ref[st] < N_EXPERTS)
            def _(st=st):
                for c in _expert_weight_copies(srcs, rune_ref[st], stage_ref, st, sem):
                    c.start()

    @pl.when(first_ref[b] == 1)
    def _():
        k = run_ref[b]
        st = k % W_STAGES
        for c in _expert_weight_copies(srcs, 0, stage_ref, st, sem):
            c.wait()
        for which, dst in enumerate(bf16_refs):
            rows = dst.shape[0] // W_PARTS
            for p in range(W_PARTS):
                rs = slice(p * rows, (p + 1) * rows)
                dst[rs, :] = stage_ref[st, which, rs, :].astype(BF16)
        e_next = rune_ref[k + W_STAGES]

        @pl.when(e_next < N_EXPERTS)
        def _():
            for c in _expert_weight_copies(srcs, e_next, stage_ref, st, sem):
                c.start()


def _moe_down_kernel(first_ref, run_ref, rune_ref, nv_ref, hid_ref, rw_ref, wd_hbm, out_ref, wstage, wdb, wsem):
    b = pl.program_id(0)
    nv = nv_ref[0]
    _expert_weight_pipeline(b, first_ref, run_ref, rune_ref, (wd_hbm,), wstage, (wdb,), wsem)

    @pl.when(b < nv)
    def _():
        y = jnp.dot(hid_ref[...], wdb[...], preferred_element_type=F32)
        packed = _pack_bf16_pairs(y * rw_ref[...])
        for c in range(SLAB_ROWS):
            out_ref[pl.ds(c, MOE_TM, stride=SLAB_ROWS), :] = packed[:, c * LANES:(c + 1) * LANES]

    @pl.when(b >= nv)
    def _():
        out_ref[...] = jnp.zeros(out_ref.shape, out_ref.dtype)


def _moe_experts(h1p, tok_words, row_w, block_e, first, run_idx, run_e, n_valid, wg, wu, wd):
    rows = row_w.shape[0]
    nb = rows // MOE_TM
    hid = pl.pallas_call(
        _moe_up_kernel,
        grid_spec=pltpu.PrefetchScalarGridSpec(
            num_scalar_prefetch=3,
            grid=(nb,),
            in_specs=[pl.BlockSpec(memory_space=pl.ANY),
                      pl.BlockSpec((1, D_MODEL, D_EXPERT), lambda b, be, nv, tok: (be[b], 0, 0)),
                      pl.BlockSpec((1, D_MODEL, D_EXPERT), lambda b, be, nv, tok: (be[b], 0, 0))],
            out_specs=pl.BlockSpec((MOE_TM, D_EXPERT), lambda b, be, nv, tok: (b, 0)),
            scratch_shapes=[pltpu.VMEM((2, MOE_TM * SLAB_ROWS, LANES), jnp.uint32), pltpu.SemaphoreType.DMA((2,))]),
        out_shape=jax.ShapeDtypeStruct((rows, D_EXPERT), BF16),
        compiler_params=_params("arbitrary"),
        name="moe_up",
    )(block_e, n_valid, tok_words, h1p.reshape(-1, SLAB_ROWS, LANES), wg, wu)
    return pl.pallas_call(
        _moe_down_kernel,
        grid_spec=pltpu.PrefetchScalarGridSpec(
            num_scalar_prefetch=4,
            grid=(nb,),
            in_specs=[pl.BlockSpec((MOE_TM, D_EXPERT), lambda b, *_: (b, 0)),
                      pl.BlockSpec((MOE_TM, 1), lambda b, *_: (b, 0)),
                      pl.BlockSpec(memory_space=pl.ANY)],
            out_specs=pl.BlockSpec((MOE_TM * SLAB_ROWS, LANES), lambda b, *_: (b, 0)),
            scratch_shapes=[pltpu.VMEM((W_STAGES, 1, D_EXPERT, D_MODEL), F32), pltpu.VMEM((D_EXPERT, D_MODEL), BF16),
                            pltpu.SemaphoreType.DMA((W_STAGES,))]),
        out_shape=jax.ShapeDtypeStruct((rows * SLAB_ROWS, LANES), jnp.uint32),
        compiler_params=_params("arbitrary"),
        name="moe_down",
    )(first, run_idx, run_e, n_valid, hid, row_w.reshape(rows, 1), wd)


FIN_TM = 256


def _final_kernel(pos_ref, h_ref, y_hbm, g_ref, b_ref, out_ref, ybuf, sem):
    i = pl.program_id(0)
    n = pl.num_programs(0)

    def row_copy(src_row, slot, r):
        return pltpu.make_async_copy(y_hbm.at[src_row], ybuf.at[slot, pl.ds(r * SLAB_ROWS, SLAB_ROWS), :], sem.at[slot])

    def start_token(tile, slot, r):
        word = pos_ref[tile * FIN_TM + r]
        row_copy(word & 0xFFFF, slot, r).start()
        row_copy((word >> 16) & 0xFFFF, slot, FIN_TM + r).start()

    def start_gather_loop(tile, slot):
        def body(r, c):
            start_token(tile, slot, r)
            return c
        lax.fori_loop(0, FIN_TM, body, 0, unroll=4)

    def start_gather_inline(tile, slot):
        for r in range(FIN_TM):
            start_token(tile, slot, r)

    def wait_gather(slot):
        def body(r, c):
            row_copy(0, slot, r).wait()
            return c
        lax.fori_loop(0, TOP_K * FIN_TM, body, 0, unroll=8)

    def compute(slot):
        los, his = [], []
        for c in range(SLAB_ROWS):
            lo, hi = _unpack_bf16_pairs(ybuf[slot, pl.ds(c, FIN_TM, stride=SLAB_ROWS), :])
            for k in range(1, TOP_K):
                lo_k, hi_k = _unpack_bf16_pairs(
                    ybuf[slot, pl.ds(k * FIN_TM * SLAB_ROWS + c, FIN_TM, stride=SLAB_ROWS), :])
                lo, hi = lo + lo_k, hi + hi_k
            los.append(lo)
            his.append(hi)
        z = DEEPNORM_ALPHA * h_ref[...] + jnp.concatenate(los + his, axis=1)
        out_ref[...] = _layer_norm(z, g_ref[...], b_ref[...])

    @pl.when(i == 0)
    def _():
        start_gather_loop(0, 0)

    @pl.when(i + 1 < n)
    def _():
        wait_gather(i % 2)
        start_gather_inline(i + 1, (i + 1) % 2)
        compute(i % 2)

    @pl.when(i + 1 == n)
    def _():
        wait_gather(i % 2)
        compute(i % 2)


def _moe_combine_ln2(h1, moe_rows, pos_words, g, b):
    T = h1.shape[0]
    assert TOP_K == 2
    const = lambda i, pos: (0, 0)
    grid_spec = pltpu.PrefetchScalarGridSpec(
        num_scalar_prefetch=1,
        grid=(T // FIN_TM,),
        in_specs=[pl.BlockSpec((FIN_TM, D_MODEL), lambda i, pos: (i, 0)),
                  pl.BlockSpec(memory_space=pl.ANY),
                  pl.BlockSpec((1, D_MODEL), const), pl.BlockSpec((1, D_MODEL), const)],
        out_specs=pl.BlockSpec((FIN_TM, D_MODEL), lambda i, pos: (i, 0)),
        scratch_shapes=[pltpu.VMEM((2, TOP_K * FIN_TM * SLAB_ROWS, LANES), jnp.uint32),
                        pltpu.SemaphoreType.DMA((2,))],
    )
    return pl.pallas_call(
        _final_kernel,
        grid_spec=grid_spec,
        out_shape=jax.ShapeDtypeStruct((T, D_MODEL), F32),
        compiler_params=_params("arbitrary"),
        name="moe_combine_ln2",
    )(pos_words, h1, moe_rows.reshape(-1, SLAB_ROWS, LANES), g.reshape(1, -1), b.reshape(1, -1))


def _pack_u16_pairs(lo, hi):
    return (lo.astype(jnp.int32) | (hi.astype(jnp.int32) << 16)).astype(jnp.int32)


def _routing(logits):
    T = logits.shape[0]
    logits_g = logits[:, :N_GROUPS]
    logits_e = logits[:, N_GROUPS:N_GROUPS + N_EXPERTS].reshape(T, N_GROUPS, EXPERTS_PER_GROUP)
    p_g = jax.nn.softmax(logits_g, axis=-1)
    g_idx = jnp.argmax(logits_g, axis=-1)
    p_grp = jnp.take_along_axis(p_g, g_idx[:, None], axis=-1)
    le = jnp.take_along_axis(logits_e, g_idx[:, None, None], axis=1)[:, 0]
    top_v, top_i = lax.top_k(le, TOP_K)
    weight = p_grp * jax.nn.softmax(top_v, axis=-1)
    expert_id = g_idx[:, None].astype(jnp.int32) * EXPERTS_PER_GROUP + top_i.astype(jnp.int32)

    A = T * TOP_K
    flat_e = expert_id.reshape(-1)
    flat_w = weight.reshape(-1)
    experts = jnp.arange(N_EXPERTS, dtype=jnp.int32)
    counts = jnp.sum((flat_e[:, None] == experts[None, :]).astype(jnp.int32), axis=0)
    starts = jnp.cumsum(counts) - counts
    padded = (counts + MOE_TM - 1) // MOE_TM * MOE_TM
    pends = jnp.cumsum(padded)
    pstarts = pends - padded
    order = jnp.argsort(flat_e, stable=True).astype(jnp.int32)
    inv = jnp.argsort(order).astype(jnp.int32)
    pos = (inv + (pstarts - starts)[flat_e]).astype(jnp.int32)
    nb = A // MOE_TM + N_EXPERTS
    assert nb * MOE_TM < 2 ** 16 and T < 2 ** 16
    n_valid = (pends[-1] // MOE_TM).astype(jnp.int32)
    blk = jnp.arange(nb, dtype=jnp.int32)
    blk_start = jnp.minimum(blk, n_valid - 1) * MOE_TM
    block_e = jnp.minimum(jnp.sum((blk_start[:, None] >= pends[None, :]).astype(jnp.int32), axis=1), N_EXPERTS - 1)
    i_in = (blk * MOE_TM - pstarts[block_e])[:, None] + jnp.arange(MOE_TM, dtype=jnp.int32)[None, :]
    valid = i_in < counts[block_e][:, None]
    a_r = order[jnp.clip(starts[block_e][:, None] + i_in, 0, A - 1)]
    row_tok = jnp.where(valid, a_r // TOP_K, 0).astype(jnp.int32)
    row_w = jnp.where(valid, flat_w[a_r], 0.0).astype(F32).reshape(-1)
    tok_words = _pack_u16_pairs(row_tok[:, 0::2], row_tok[:, 1::2]).reshape(-1)
    pos2 = pos.reshape(T, TOP_K)
    pos_words = _pack_u16_pairs(pos2[:, 0], pos2[:, 1])
    prev_e = jnp.concatenate([jnp.full((1,), -1, jnp.int32), block_e[:-1]])
    first = jnp.logical_and(block_e != prev_e, blk < n_valid).astype(jnp.int32)
    run_idx = jnp.maximum(jnp.cumsum(first) - 1, 0).astype(jnp.int32)
    run_e = jnp.concatenate([jnp.sort(jnp.where(counts > 0, experts, N_EXPERTS)),
                             jnp.full((W_STAGES,), N_EXPERTS, jnp.int32)]).astype(jnp.int32)
    return tok_words, row_w, block_e, first, run_idx, run_e, n_valid.reshape(1), pos_words


def kernel(x, positions, w_in, b_gate, conv_w, conv_b, w_rg_a, b_rg_a, w_rg_x, b_rg_x, lru_lambda, w_attn_proj, w_rec_proj, w_out, ln1_g, ln1_b, w_router_group, b_router_group, w_router_expert, b_router_expert, w_gate, w_up, w_down, ln2_g, ln2_b):
    B, S, D = x.shape
    T = B * S
    h = x.reshape(T, D)
    c, sa, sb = _rope_tables(positions)
    for layer in range(DEPTH):
        xb = h.astype(BF16)
        os_, lses = [], []
        for g, (_, dilation) in enumerate(ATTN_GROUPS):
            qkv = _qkv_proj(xb, w_in[layer], c, sa, sb, g, dilation)
            o, lse = _attention_group(qkv, B, S, dilation)
            os_.append(o)
            lses.append(lse)
        rest = _rest_proj(xb, w_in[layer])
        rec = _rglru(rest, B, S, conv_w[layer], conv_b[layer], w_rg_a[layer].astype(BF16), b_rg_a[layer],
                     w_rg_x[layer].astype(BF16), b_rg_x[layer], lru_lambda[layer])
        merged = _merge(os_, lses, rec, rest, b_gate[layer], w_attn_proj[layer].astype(BF16),
                        w_rec_proj[layer].astype(BF16))
        w_router = jnp.concatenate([w_router_group[layer], w_router_expert[layer]], axis=1)
        w_router = jnp.pad(w_router, ((0, 0), (0, LANES - w_router.shape[1])))
        b_router = jnp.concatenate([b_router_group[layer], b_router_expert[layer]])
        b_router = jnp.pad(b_router, (0, LANES - b_router.shape[0])).reshape(1, LANES)
        mix = _out_matmul(merged, w_out[layer].astype(BF16))
        h1, h1p, logits = _ln1_router(mix, h, ln1_g[layer], ln1_b[layer], w_router.astype(BF16), b_router)
        tok_words, row_w, block_e, first, run_idx, run_e, n_valid, pos_words = _routing(logits)
        moe_rows = _moe_experts(h1p, tok_words, row_w, block_e, first, run_idx, run_e, n_valid,
                                w_gate[layer], w_up[layer], w_down[layer])
        h = _moe_combine_ln2(h1, moe_rows, pos_words, ln2_g[layer], ln2_b[layer])
    return h.reshape(B, S, D)
```
